```python
import jax, jax.numpy as jnp
from jax import lax
import numpy as np

D_MODEL = 2048
BATCH = 8
SEQ = 2048
DEPTH = 1

HEAD_DIM = 64
MIX_WIDTH = D_MODEL
N_HEADS_A = MIX_WIDTH // HEAD_DIM // 2
GQA_RATIO = 8
N_KV_A = N_HEADS_A // GQA_RATIO
GROUP_A = N_HEADS_A // N_KV_A
N_HEADS_B = MIX_WIDTH // HEAD_DIM - N_HEADS_A
WIDTH_A = N_HEADS_A * HEAD_DIM
WIDTH_KV_A = N_KV_A * HEAD_DIM
WIDTH_B = N_HEADS_B * HEAD_DIM
IN_SPLITS = (WIDTH_A, WIDTH_A + WIDTH_KV_A, WIDTH_A + 2 * WIDTH_KV_A, WIDTH_A + 2 * WIDTH_KV_A + WIDTH_B, WIDTH_A + 2 * WIDTH_KV_A + 2 * WIDTH_B)
IN_WIDTH = WIDTH_A + 2 * WIDTH_KV_A + 3 * WIDTH_B
WINDOW_A = 128
DILATED = ((128, 1), (512, 4), (2048, 16))
BLOCK = 128
ROPE_THETA = 500000.0
ROT_DIM = HEAD_DIM // 4
D_FF = 256 * ((8 * D_MODEL // 3 + 255) // 256)
FFN_RES = 0.5
N_SUB = 3
N_MOD = 3
EPS = 1e-5
MAX_START = 4096

kernel_name = 'hybrid_swa_sink_dilated_macaron'


def rmsnorm(x, g):
    xf = x.astype(jnp.float32)
    y = xf * lax.rsqrt(jnp.mean(xf * xf, axis=-1, keepdims=True) + EPS)
    return (y * g.astype(jnp.float32)).astype(x.dtype)


def modulate(h, shift, scale):
    return h * (1 + scale) + shift


def swiglu(h, w_gu, w_down):
    gate, up = jnp.split(h @ w_gu, 2, axis=-1)
    return (jax.nn.silu(gate) * up) @ w_down


def rope_tables(positions):
    inv_freq = ROPE_THETA ** (-(jnp.arange(0, ROT_DIM, 2, dtype=jnp.float32) / ROT_DIM))
    ang = positions.astype(jnp.float32)[..., None] * inv_freq
    return jnp.cos(ang)[:, :, None, :], jnp.sin(ang)[:, :, None, :]


def partial_rope(x, cos, sin):
    xr, xp = x[..., :ROT_DIM], x[..., ROT_DIM:]
    x1, x2 = xr[..., :ROT_DIM // 2], xr[..., ROT_DIM // 2:]
    rot = jnp.concatenate([x1 * cos - x2 * sin, x2 * cos + x1 * sin], axis=-1)
    return jnp.concatenate([rot.astype(x.dtype), xp], axis=-1)


def banded_attention(q, k, v, max_dist, sink=None):
    b, L, hk, g, hd = q.shape
    n_prev = -(-max_dist // BLOCK)
    nb = -(-L // BLOCK)
    pad = nb * BLOCK - L
    front = n_prev * BLOCK
    kw_len = (n_prev + 1) * BLOCK
    qb = jnp.pad(q, ((0, 0), (0, pad), (0, 0), (0, 0), (0, 0))).reshape(b, nb, BLOCK, hk, g, hd)
    kp = jnp.pad(k, ((0, 0), (front, pad), (0, 0), (0, 0))).reshape(b, nb + n_prev, BLOCK, hk, hd)
    vp = jnp.pad(v, ((0, 0), (front, pad), (0, 0), (0, 0))).reshape(b, nb + n_prev, BLOCK, hk, hd)
    kw = jnp.concatenate([kp[:, j:j + nb] for j in range(n_prev + 1)], axis=2)
    vw = jnp.concatenate([vp[:, j:j + nb] for j in range(n_prev + 1)], axis=2)
    qi = jnp.arange(nb)[:, None, None] * BLOCK + jnp.arange(BLOCK)[None, :, None]
    ki = jnp.arange(nb)[:, None, None] * BLOCK + jnp.arange(kw_len)[None, None, :] - front
    dist = qi - ki
    mask = ((dist >= 0) & (dist <= max_dist) & (ki >= 0))[None, :, None, None]
    s = jnp.einsum('bnqhgd,bnkhd->bnhgqk', qb, kw, preferred_element_type=jnp.float32) * (hd ** -0.5)
    s = jnp.where(mask, s, -jnp.inf)
    m = jnp.max(s, axis=-1, keepdims=True)
    if sink is not None:
        sink_l = sink.astype(jnp.float32).reshape(1, 1, hk, g, 1, 1)
        m = jnp.maximum(m, sink_l)
    p = jnp.exp(s - m)
    l = jnp.sum(p, axis=-1, keepdims=True)
    if sink is not None:
        l = l + jnp.exp(sink_l - m)
    o = jnp.einsum('bnhgqk,bnkhd->bnqhgd', p, vw.astype(jnp.float32))
    o = o / jnp.transpose(l, (0, 1, 4, 2, 3, 5))
    lse = jnp.transpose((m + jnp.log(l))[..., 0], (0, 1, 4, 2, 3))
    o = o.reshape(b, nb * BLOCK, hk, g, hd)[:, :L].astype(q.dtype)
    lse = lse.reshape(b, nb * BLOCK, hk, g)[:, :L]
    return o, lse


def to_residues(t, dil):
    b, s = t.shape[:2]
    rest = t.shape[2:]
    return t.reshape((b, s // dil, dil) + rest).swapaxes(1, 2).reshape((b * dil, s // dil) + rest)


def from_residues(t, dil, b):
    m = t.shape[1]
    rest = t.shape[2:]
    return t.reshape((b, dil, m) + rest).swapaxes(1, 2).reshape((b, dil * m) + rest)


def dilated_mixture(q, k, v):
    b, s, h, hd = q.shape
    outs, lses = [], []
    for window, dil in DILATED:
        o, lse = banded_attention(to_residues(q, dil)[:, :, :, None, :], to_residues(k, dil), to_residues(v, dil), window // dil)
        outs.append(from_residues(o[:, :, :, 0, :], dil, b))
        lses.append(from_residues(lse[..., 0], dil, b))
    w = jax.nn.softmax(jnp.stack(lses, axis=0), axis=0)
    out = jnp.sum(w[..., None] * jnp.stack(outs, axis=0).astype(jnp.float32), axis=0)
    return out.reshape(b, s, h * hd).astype(q.dtype)


def hybrid_mixer(h, cos, sin, w_in, b_in, sinks, g_out_a, g_out_b, w_out, b_out):
    b, s, _ = h.shape
    proj = h @ w_in + b_in
    qa, ka, va, qb, kb, vb = jnp.split(proj, IN_SPLITS, axis=-1)
    qa = partial_rope(qa.reshape(b, s, N_HEADS_A, HEAD_DIM), cos, sin).reshape(b, s, N_KV_A, GROUP_A, HEAD_DIM)
    ka = partial_rope(ka.reshape(b, s, N_KV_A, HEAD_DIM), cos, sin)
    va = va.reshape(b, s, N_KV_A, HEAD_DIM)
    out_a, _ = banded_attention(qa, ka, va, WINDOW_A - 1, sinks.reshape(N_KV_A, GROUP_A))
    out_a = out_a.reshape(b, s, WIDTH_A)
    qb = partial_rope(qb.reshape(b, s, N_HEADS_B, HEAD_DIM), cos, sin)
    kb = partial_rope(kb.reshape(b, s, N_HEADS_B, HEAD_DIM), cos, sin)
    vb = vb.reshape(b, s, N_HEADS_B, HEAD_DIM)
    out_b = dilated_mixture(qb, kb, vb)
    y = jnp.concatenate([rmsnorm(out_a, g_out_a), rmsnorm(out_b, g_out_b)], axis=-1)
    return y @ w_out + b_out


def setup_inputs(seed: int = 0) -> dict:
    key = jax.random.key(seed)
    ks = jax.random.split(key, 20)
    nrm = jax.random.normal
    f32 = jnp.float32
    x = nrm(ks[0], (BATCH, SEQ, D_MODEL), f32)
    c = nrm(ks[1], (BATCH, D_MODEL), f32)
    start = jax.random.randint(ks[2], (BATCH, 1), 0, MAX_START, dtype=jnp.int32)
    positions = start + jnp.arange(SEQ, dtype=jnp.int32)[None, :]
    w_ada = nrm(ks[3], (DEPTH, D_MODEL, N_SUB * N_MOD * D_MODEL), f32) * D_MODEL ** -0.5
    b_ada = 0.02 * nrm(ks[4], (DEPTH, N_SUB * N_MOD * D_MODEL), f32)
    g_ffn1 = 1.0 + 0.02 * nrm(ks[5], (DEPTH, D_MODEL), f32)
    w_ffn1_in = nrm(ks[6], (DEPTH, D_MODEL, 2 * D_FF), f32) * D_MODEL ** -0.5
    w_ffn1_out = nrm(ks[7], (DEPTH, D_FF, D_MODEL), f32) * D_FF ** -0.5
    g_mix = 1.0 + 0.02 * nrm(ks[8], (DEPTH, D_MODEL), f32)
    w_in = nrm(ks[9], (DEPTH, D_MODEL, IN_WIDTH), f32) * D_MODEL ** -0.5
    b_in = 0.02 * nrm(ks[10], (DEPTH, IN_WIDTH), f32)
    sinks = nrm(ks[11], (DEPTH, N_HEADS_A), f32)
    g_out_a = 1.0 + 0.02 * nrm(ks[12], (DEPTH, WIDTH_A), f32)
    g_out_b = 1.0 + 0.02 * nrm(ks[13], (DEPTH, WIDTH_B), f32)
    w_out = nrm(ks[14], (DEPTH, MIX_WIDTH, D_MODEL), f32) * MIX_WIDTH ** -0.5
    b_out = 0.02 * nrm(ks[15], (DEPTH, D_MODEL), f32)
    g_ffn2 = 1.0 + 0.02 * nrm(ks[16], (DEPTH, D_MODEL), f32)
    w_ffn2_in = nrm(ks[17], (DEPTH, D_MODEL, 2 * D_FF), f32) * D_MODEL ** -0.5
    w_ffn2_out = nrm(ks[18], (DEPTH, D_FF, D_MODEL), f32) * D_FF ** -0.5
    g_final = 1.0 + 0.02 * nrm(ks[19], (D_MODEL,), f32)
    return {'x': x, 'c': c, 'positions': positions, 'w_ada': w_ada, 'b_ada': b_ada,
            'g_ffn1': g_ffn1, 'w_ffn1_in': w_ffn1_in, 'w_ffn1_out': w_ffn1_out,
            'g_mix': g_mix, 'w_in': w_in, 'b_in': b_in, 'sinks': sinks,
            'g_out_a': g_out_a, 'g_out_b': g_out_b, 'w_out': w_out, 'b_out': b_out,
            'g_ffn2': g_ffn2, 'w_ffn2_in': w_ffn2_in, 'w_ffn2_out': w_ffn2_out, 'g_final': g_final}


def reference(x, c, positions, w_ada, b_ada, g_ffn1, w_ffn1_in, w_ffn1_out, g_mix, w_in, b_in, sinks, g_out_a, g_out_b, w_out, b_out, g_ffn2, w_ffn2_in, w_ffn2_out, g_final):
    b = x.shape[0]
    cos, sin = rope_tables(positions)
    cond = jax.nn.silu(c)
    for layer in range(DEPTH):
        mod = (cond @ w_ada[layer] + b_ada[layer]).reshape(b, N_SUB, N_MOD, D_MODEL)
        shift = mod[:, :, 0, None, :]
        scale = mod[:, :, 1, None, :]
        gate = mod[:, :, 2, None, :]
        h = modulate(rmsnorm(x, g_ffn1[layer]), shift[:, 0], scale[:, 0])
        x = x + FFN_RES * gate[:, 0] * swiglu(h, w_ffn1_in[layer], w_ffn1_out[layer])
        h = modulate(rmsnorm(x, g_mix[layer]), shift[:, 1], scale[:, 1])
        x = x + gate[:, 1] * hybrid_mixer(h, cos, sin, w_in[layer], b_in[layer], sinks[layer], g_out_a[layer], g_out_b[layer], w_out[layer], b_out[layer])
        h = modulate(rmsnorm(x, g_ffn2[layer]), shift[:, 2], scale[:, 2])
        x = x + FFN_RES * gate[:, 2] * swiglu(h, w_ffn2_in[layer], w_ffn2_out[layer])
    return rmsnorm(x, g_final)
```

```python
import functools

import jax
import jax.numpy as jnp
from jax import lax
from jax.experimental import pallas as pl
from jax.experimental.pallas import tpu as pltpu

F32 = jnp.float32
BF16 = jnp.bfloat16

D_MODEL = 2048
HEAD_DIM = 64
N_HEADS_A = 16
N_KV_A = 2
GROUP_A = N_HEADS_A // N_KV_A
N_HEADS_B = 16
WIDTH_A = N_HEADS_A * HEAD_DIM
WIDTH_KV_A = N_KV_A * HEAD_DIM
WIDTH_B = N_HEADS_B * HEAD_DIM
IN_WIDTH = WIDTH_A + 2 * WIDTH_KV_A + 3 * WIDTH_B
WINDOW_A = 128
DILATED = ((128, 1), (512, 4), (2048, 16))
BLOCK = 128
ROPE_THETA = 500000.0
ROT_DIM = HEAD_DIM // 4
D_FF = 5632
FFN_RES = 0.5
N_SUB = 3
N_MOD = 3
EPS = 1e-5

LANES = 128
MXU_COLS = 256
NEG_BIG = -1e30
VMEM_LIMIT = 56 * 1024 * 1024

TM_FFN = 512
TF_FFN = 512
TM_PROJ = 512
TN_ADA = 1024


def _silu(x):
    return x * (1.0 / (1.0 + jnp.exp(-x)))


def _rms(x, g):
    ms = jnp.mean(x * x, axis=-1, keepdims=True)
    return (x * lax.rsqrt(ms + EPS)) * g


def _ada_kernel(c_ref, w_ref, b_ref, o_ref):
    cond = _silu(c_ref[...]).astype(BF16)
    o_ref[...] = jnp.dot(cond, w_ref[...].astype(BF16), preferred_element_type=F32) + b_ref[...]


def _ada_call(c, w_ada, b_ada):
    b, d = c.shape
    n = w_ada.shape[1]
    return pl.pallas_call(
        _ada_kernel,
        grid=(n // TN_ADA,),
        in_specs=[
            pl.BlockSpec((b, d), lambda j: (0, 0)),
            pl.BlockSpec((d, TN_ADA), lambda j: (0, j)),
            pl.BlockSpec((1, TN_ADA), lambda j: (0, j)),
        ],
        out_specs=pl.BlockSpec((b, TN_ADA), lambda j: (0, j)),
        out_shape=jax.ShapeDtypeStruct((b, n), F32),
        compiler_params=pltpu.CompilerParams(
            dimension_semantics=("arbitrary",), vmem_limit_bytes=VMEM_LIMIT),
        name="adaln_mod",
    )(c, w_ada, b_ada.reshape(1, n))


def _ffn_kernel(x_ref, mod_ref, g_ref, wg_ref, wu_ref, wd_ref, gfin_ref, o_ref, h_ref, acc_ref,
                *, sub, n_chunks, final_norm):
    j = pl.program_id(1)

    @pl.when(j == 0)
    def _():
        shift = mod_ref[0, pl.ds(3 * sub + 0, 1), :]
        scale = mod_ref[0, pl.ds(3 * sub + 1, 1), :]
        h = _rms(x_ref[...], g_ref[...]) * (1.0 + scale) + shift
        h_ref[...] = h.astype(BF16)
        acc_ref[...] = jnp.zeros_like(acc_ref)

    h = h_ref[...]
    gate = jnp.dot(h, wg_ref[...], preferred_element_type=F32)
    up = jnp.dot(h, wu_ref[...], preferred_element_type=F32)
    act = (_silu(gate) * up).astype(BF16)
    acc_ref[...] += jnp.dot(act, wd_ref[...], preferred_element_type=F32)

    @pl.when(j == n_chunks - 1)
    def _():
        gmod = mod_ref[0, pl.ds(3 * sub + 2, 1), :]
        out = x_ref[...] + (FFN_RES * gmod) * acc_ref[...]
        if final_norm:
            out = _rms(out, gfin_ref[...])
        o_ref[...] = out


def _ffn_call(x, mod, g, w_gu, w_down, g_final, *, sub, seq, final_norm):
    t, d = x.shape
    dff = w_down.shape[0]
    n_chunks = dff // TF_FFN
    tiles_per_batch = seq // TM_FFN
    kern = functools.partial(_ffn_kernel, sub=sub, n_chunks=n_chunks, final_norm=final_norm)
    return pl.pallas_call(
        kern,
        grid=(t // TM_FFN, n_chunks),
        in_specs=[
            pl.BlockSpec((TM_FFN, d), lambda i, j: (i, 0)),
            pl.BlockSpec((1, N_SUB * N_MOD, d), lambda i, j: (i // tiles_per_batch, 0, 0)),
            pl.BlockSpec((1, d), lambda i, j: (0, 0)),
            pl.BlockSpec((d, TF_FFN), lambda i, j: (0, j)),
            pl.BlockSpec((d, TF_FFN), lambda i, j: (0, n_chunks + j)),
            pl.BlockSpec((TF_FFN, d), lambda i, j: (j, 0)),
            pl.BlockSpec((1, d), lambda i, j: (0, 0)),
        ],
        out_specs=pl.BlockSpec((TM_FFN, d), lambda i, j: (i, 0)),
        out_shape=jax.ShapeDtypeStruct((t, d), F32),
        scratch_shapes=[pltpu.VMEM((TM_FFN, d), BF16), pltpu.VMEM((TM_FFN, d), F32)],
        compiler_params=pltpu.CompilerParams(
            dimension_semantics=("arbitrary", "arbitrary"), vmem_limit_bytes=VMEM_LIMIT),
        name="ffn_sub%d" % sub,
    )(x, mod, g.reshape(1, d), w_gu, w_gu, w_down, g_final.reshape(1, d))


def _rope_tables(pos_ref, invf_ref):
    lane = lax.broadcasted_iota(jnp.int32, (1, LANES), 1)
    d = lane % HEAD_DIM
    ang = pos_ref[...].astype(F32) * invf_ref[...]
    c = jnp.cos(ang)
    s = jnp.sin(ang)
    half = ROT_DIM // 2
    coef_self = jnp.where(d < ROT_DIM, c, 1.0)
    coef_next = jnp.where(d < half, -s, 0.0)
    coef_prev = jnp.where((d >= half) & (d < ROT_DIM), s, 0.0)
    return coef_self, coef_next, coef_prev


def _rope_slab(z, tables):
    coef_self, coef_next, coef_prev = tables
    half = ROT_DIM // 2
    nxt = pltpu.roll(z, LANES - half, axis=1)
    prv = pltpu.roll(z, half, axis=1)
    return z * coef_self + nxt * coef_next + prv * coef_prev


def _inproj_kernel(x_ref, mod_ref, g_ref, w_ref, b_ref, pos_ref, invf_ref,
                   qa_ref, ka_ref, va_ref, qb_ref, kb_ref, vb_ref, *, sub):
    shift = mod_ref[0, pl.ds(3 * sub + 0, 1), :]
    scale = mod_ref[0, pl.ds(3 * sub + 1, 1), :]
    h = (_rms(x_ref[...], g_ref[...]) * (1.0 + scale) + shift).astype(BF16)
    tables = _rope_tables(pos_ref, invf_ref)
    lane = lax.broadcasted_iota(jnp.int32, (1, LANES), 1)
    lo = lane < HEAD_DIM
    qscale = HEAD_DIM ** -0.5

    def proj_slabs(col, width):
        for c0 in range(0, width, MXU_COLS):
            z = (jnp.dot(h, w_ref[:, col + c0:col + c0 + MXU_COLS], preferred_element_type=F32)
                 + b_ref[:, col + c0:col + c0 + MXU_COLS])
            for half in range(MXU_COLS // LANES):
                yield c0 // LANES + half, z[:, half * LANES:(half + 1) * LANES]

    def dup_heads(slab):
        swapped = pltpu.roll(slab, HEAD_DIM, axis=1)
        return jnp.where(lo, slab, swapped), jnp.where(lo, swapped, slab)

    col = 0
    for k, z in proj_slabs(col, WIDTH_A):
        qa_ref[:, k * LANES:(k + 1) * LANES] = (_rope_slab(z, tables) * qscale).astype(BF16)
    col += WIDTH_A
    for k, z in proj_slabs(col, 2 * WIDTH_KV_A):
        dst = ka_ref if k == 0 else va_ref
        h0, h1 = dup_heads(_rope_slab(z, tables) if k == 0 else z)
        dst[:, 0:LANES] = h0.astype(BF16)
        dst[:, LANES:2 * LANES] = h1.astype(BF16)
    col += 2 * WIDTH_KV_A
    for k, z in proj_slabs(col, WIDTH_B):
        qb_ref[:, k * LANES:(k + 1) * LANES] = _rope_slab(z, tables) * qscale
    col += WIDTH_B
    for k, z in proj_slabs(col, WIDTH_B):
        kb_ref[:, k * LANES:(k + 1) * LANES] = _rope_slab(z, tables)
    col += WIDTH_B
    for k, z in proj_slabs(col, WIDTH_B):
        vb_ref[:, k * LANES:(k + 1) * LANES] = z


def _inproj_call(x, mod, g, w_in, b_in, pos, invf, *, sub, seq):
    t, d = x.shape
    tm = TM_PROJ
    tiles_per_batch = seq // tm
    row = lambda i: (i, 0)
    const = lambda i: (0, 0)
    return pl.pallas_call(
        functools.partial(_inproj_kernel, sub=sub),
        grid=(t // tm,),
        in_specs=[
            pl.BlockSpec((tm, d), row),
            pl.BlockSpec((1, N_SUB * N_MOD, d), lambda i: (i // tiles_per_batch, 0, 0)),
            pl.BlockSpec((1, d), const),
            pl.BlockSpec((d, IN_WIDTH), const, pipeline_mode=pl.Buffered(1)),
            pl.BlockSpec((1, IN_WIDTH), const),
            pl.BlockSpec((tm, 1), row),
            pl.BlockSpec((1, LANES), const),
        ],
        out_specs=[
            pl.BlockSpec((tm, WIDTH_A), row),
            pl.BlockSpec((tm, 2 * LANES), row),
            pl.BlockSpec((tm, 2 * LANES), row),
            pl.BlockSpec((tm, WIDTH_B), row),
            pl.BlockSpec((tm, WIDTH_B), row),
            pl.BlockSpec((tm, WIDTH_B), row),
        ],
        out_shape=[
            jax.ShapeDtypeStruct((t, WIDTH_A), BF16),
            jax.ShapeDtypeStruct((t, 2 * LANES), BF16),
            jax.ShapeDtypeStruct((t, 2 * LANES), BF16),
            jax.ShapeDtypeStruct((t, WIDTH_B), F32),
            jax.ShapeDtypeStruct((t, WIDTH_B), F32),
            jax.ShapeDtypeStruct((t, WIDTH_B), F32),
        ],
        compiler_params=pltpu.CompilerParams(
            dimension_semantics=("arbitrary",), vmem_limit_bytes=VMEM_LIMIT),
        name="inproj_rope",
    )(x, mod, g.reshape(1, d), w_in, b_in.reshape(1, IN_WIDTH), pos, invf)


def _attn_a_kernel(sink_ref, q_ref, kp_ref, kc_ref, vp_ref, vc_ref, o_ref):
    i = pl.program_id(1)
    q = q_ref[...]
    lane = lax.broadcasted_iota(jnp.int32, (1, LANES), 1)
    lo = lane < HEAD_DIM
    qi = lax.broadcasted_iota(jnp.int32, (BLOCK, 2 * BLOCK), 0)
    kj = lax.broadcasted_iota(jnp.int32, (BLOCK, 2 * BLOCK), 1)
    dist = qi + BLOCK - kj
    valid = (dist >= 0) & (dist <= WINDOW_A - 1) & ((kj >= BLOCK) | (i > 0))
    bias = jnp.where(valid, 0.0, NEG_BIG).astype(F32)
    zero = jnp.zeros((BLOCK, LANES), BF16)
    pairs = GROUP_A // 2
    for hk in range(N_KV_A):
        kk = jnp.concatenate([kp_ref[:, hk * LANES:(hk + 1) * LANES],
                              kc_ref[:, hk * LANES:(hk + 1) * LANES]], axis=0)
        vv = jnp.concatenate([vp_ref[:, hk * LANES:(hk + 1) * LANES],
                              vc_ref[:, hk * LANES:(hk + 1) * LANES]], axis=0)
        rows = []
        for jp in range(pairs):
            qp = q[:, (hk * pairs + jp) * LANES:(hk * pairs + jp + 1) * LANES]
            rows.append(jnp.where(lo, qp, zero))
            rows.append(jnp.where(lo, zero, qp))
        lhs = jnp.concatenate(rows, axis=0)
        s_all = lax.dot_general(lhs, kk, (((1,), (1,)), ((), ())),
                                preferred_element_type=F32)
        ps, inv_ls = [], []
        for g in range(GROUP_A):
            sink = sink_ref[hk * GROUP_A + g]
            s = s_all[g * BLOCK:(g + 1) * BLOCK, :] + bias
            m = jnp.maximum(jnp.max(s, axis=-1, keepdims=True), sink)
            p = jnp.exp(s - m)
            l = jnp.sum(p, axis=-1, keepdims=True) + jnp.exp(sink - m)
            ps.append(p.astype(BF16))
            inv_ls.append(1.0 / l)
        o_all = jnp.dot(jnp.concatenate(ps, axis=0), vv, preferred_element_type=F32)
        for jp in range(pairs):
            g0, g1 = 2 * jp, 2 * jp + 1
            o0 = o_all[g0 * BLOCK:(g0 + 1) * BLOCK, :] * inv_ls[g0]
            o1 = o_all[g1 * BLOCK:(g1 + 1) * BLOCK, :] * inv_ls[g1]
            c0 = (hk * pairs + jp) * LANES
            o_ref[:, c0:c0 + LANES] = jnp.where(lo, o0, o1)


def _attn_a_call(sinks, qa, ka, va, *, batch, seq):
    nb = seq // BLOCK
    cur = lambda b, i: (b * nb + i, 0)
    prev = lambda b, i: (b * nb + jnp.maximum(i - 1, 0), 0)
    return pl.pallas_call(
        _attn_a_kernel,
        grid=(batch, nb),
        in_specs=[
            pl.BlockSpec(memory_space=pltpu.SMEM),
            pl.BlockSpec((BLOCK, WIDTH_A), cur),
            pl.BlockSpec((BLOCK, 2 * LANES), prev),
            pl.BlockSpec((BLOCK, 2 * LANES), cur),
            pl.BlockSpec((BLOCK, 2 * LANES), prev),
            pl.BlockSpec((BLOCK, 2 * LANES), cur),
        ],
        out_specs=pl.BlockSpec((BLOCK, WIDTH_A), cur),
        out_shape=jax.ShapeDtypeStruct((batch * seq, WIDTH_A), F32),
        compiler_params=pltpu.CompilerParams(
            dimension_semantics=("arbitrary", "arbitrary"), vmem_limit_bytes=VMEM_LIMIT),
        name="attn_swa_gqa",
    )(sinks, qa, ka, ka, va, va)


def _attn_b_kernel(q_ref, k_ref, v_ref, o_ref, m_run, l_run, acc_run, *, seq):
    lane = lax.broadcasted_iota(jnp.int32, (1, LANES), 1)
    lo = lane < HEAD_DIM
    qi = lax.broadcasted_iota(jnp.int32, (BLOCK, BLOCK), 0)
    kj = lax.broadcasted_iota(jnp.int32, (BLOCK, BLOCK), 1)
    bias_cur = jnp.where(kj <= qi, 0.0, NEG_BIG).astype(F32)
    bias_prev = jnp.where(kj >= qi, 0.0, NEG_BIG).astype(F32)
    zero = jnp.zeros((BLOCK, LANES), BF16)

    def rows(ref, start, dil):
        if dil == 1:
            return ref[0, pl.ds(start, BLOCK), :]
        return ref[0, pl.ds(start, BLOCK, stride=dil), :]

    def store_rows(ref, start, dil, val):
        if dil == 1:
            ref[pl.ds(start, BLOCK), :] = val
        else:
            ref[pl.ds(start, BLOCK, stride=dil), :] = val

    def load_rows(ref, start, dil):
        if dil == 1:
            return ref[pl.ds(start, BLOCK), :]
        return ref[pl.ds(start, BLOCK, stride=dil), :]

    def block_stats(q, kcat, vcat, bias):
        qb = q.astype(BF16)
        lhs = jnp.concatenate([jnp.where(lo, qb, zero), jnp.where(lo, zero, qb)], axis=0)
        s = lax.dot_general(lhs, kcat.astype(BF16), (((1,), (1,)), ((), ())),
                            preferred_element_type=F32)
        s0 = s[:BLOCK, :] + bias
        s1 = s[BLOCK:, :] + bias
        m0 = jnp.max(s0, axis=-1, keepdims=True)
        m1 = jnp.max(s1, axis=-1, keepdims=True)
        p0 = jnp.exp(s0 - m0)
        p1 = jnp.exp(s1 - m1)
        l0 = jnp.sum(p0, axis=-1, keepdims=True)
        l1 = jnp.sum(p1, axis=-1, keepdims=True)
        pv = jnp.dot(jnp.concatenate([p0, p1], axis=0).astype(BF16), vcat.astype(BF16),
                     preferred_element_type=F32)
        acc = jnp.where(lo, pv[:BLOCK, :], pv[BLOCK:, :])
        m = jnp.where(lo, m0, m1)
        l = jnp.where(lo, l0, l1)
        return m, l, acc

    def merge_store(start, dil, m, l, acc, first):
        if not first:
            m_old = load_rows(m_run, start, dil)
            l_old = load_rows(l_run, start, dil)
            acc_old = load_rows(acc_run, start, dil)
            m_new = jnp.maximum(m_old, m)
            a = jnp.exp(m_old - m_new)
            b = jnp.exp(m - m_new)
            l = a * l_old + b * l
            acc = a * acc_old + b * acc
            m = m_new
        store_rows(m_run, start, dil, m)
        store_rows(l_run, start, dil, l)
        store_rows(acc_run, start, dil, acc)

    for cfg, (window, dil) in enumerate(DILATED):
        n_class = seq // dil
        nblk = n_class // BLOCK
        assert window // dil == BLOCK and n_class % BLOCK == 0
        first = cfg == 0

        def body(idx, carry, dil=dil, nblk=nblk, first=first):
            r = idx // nblk
            n = idx % nblk
            start = r + n * (BLOCK * dil)
            if dil == 1:
                start = pl.multiple_of(start, BLOCK)
            q = rows(q_ref, start, dil)
            kc = rows(k_ref, start, dil)
            vc = rows(v_ref, start, dil)
            if nblk == 1:
                m, l, acc = block_stats(q, kc, vc, bias_cur)
            else:
                pstart = jnp.maximum(start - BLOCK * dil, r)
                if dil == 1:
                    pstart = pl.multiple_of(pstart, BLOCK)
                kp = rows(k_ref, pstart, dil)
                vp = rows(v_ref, pstart, dil)
                off = jnp.where(n > 0, 0.0, NEG_BIG).astype(F32)
                bias = jnp.concatenate([bias_prev + off, bias_cur], axis=1)
                m, l, acc = block_stats(q, jnp.concatenate([kp, kc], axis=0),
                                        jnp.concatenate([vp, vc], axis=0), bias)
            merge_store(start, dil, m, l, acc, first)
            return carry

        lax.fori_loop(0, dil * nblk, body, 0)

    o_ref[0, :, :] = acc_run[...] * (1.0 / l_run[...])


def _attn_b_call(qb, kb, vb, *, batch, seq):
    spec = pl.BlockSpec((1, seq, LANES), lambda b, hp: (b, 0, hp))
    shape3 = (batch, seq, WIDTH_B)
    return pl.pallas_call(
        functools.partial(_attn_b_kernel, seq=seq),
        grid=(batch, WIDTH_B // LANES),
        in_specs=[spec, spec, spec],
        out_specs=spec,
        out_shape=jax.ShapeDtypeStruct(shape3, F32),
        scratch_shapes=[pltpu.VMEM((seq, LANES), F32)] * 3,
        compiler_params=pltpu.CompilerParams(
            dimension_semantics=("arbitrary", "arbitrary"), vmem_limit_bytes=VMEM_LIMIT),
        name="attn_dilated",
    )(qb.reshape(shape3), kb.reshape(shape3), vb.reshape(shape3))


def _outproj_kernel(oa_ref, ob_ref, x_ref, mod_ref, ga_ref, gb_ref, w_ref, b_ref, o_ref, *, sub):
    ya = _rms(oa_ref[...], ga_ref[...]).astype(BF16)
    yb = _rms(ob_ref[...], gb_ref[...]).astype(BF16)
    y = (jnp.dot(ya, w_ref[0:WIDTH_A, :], preferred_element_type=F32)
         + jnp.dot(yb, w_ref[WIDTH_A:WIDTH_A + WIDTH_B, :], preferred_element_type=F32)
         + b_ref[...])
    gmod = mod_ref[0, pl.ds(3 * sub + 2, 1), :]
    o_ref[...] = x_ref[...] + gmod * y


def _outproj_call(out_a, out_b, x, mod, g_a, g_b, w_out, b_out, *, sub, seq):
    t, d = x.shape
    tm = TM_PROJ
    tiles_per_batch = seq // tm
    row = lambda i: (i, 0)
    const = lambda i: (0, 0)
    return pl.pallas_call(
        functools.partial(_outproj_kernel, sub=sub),
        grid=(t // tm,),
        in_specs=[
            pl.BlockSpec((tm, WIDTH_A), row),
            pl.BlockSpec((tm, WIDTH_B), row),
            pl.BlockSpec((tm, d), row),
            pl.BlockSpec((1, N_SUB * N_MOD, d), lambda i: (i // tiles_per_batch, 0, 0)),
            pl.BlockSpec((1, WIDTH_A), const),
            pl.BlockSpec((1, WIDTH_B), const),
            pl.BlockSpec((WIDTH_A + WIDTH_B, d), const),
            pl.BlockSpec((1, d), const),
        ],
        out_specs=pl.BlockSpec((tm, d), row),
        out_shape=jax.ShapeDtypeStruct((t, d), F32),
        compiler_params=pltpu.CompilerParams(
            dimension_semantics=("arbitrary",), vmem_limit_bytes=VMEM_LIMIT),
        name="outproj_residual",
    )(out_a, out_b, x, mod, g_a.reshape(1, WIDTH_A), g_b.reshape(1, WIDTH_B), w_out,
      b_out.reshape(1, d))


def kernel(x, c, positions, w_ada, b_ada, g_ffn1, w_ffn1_in, w_ffn1_out, g_mix, w_in, b_in, sinks,
           g_out_a, g_out_b, w_out, b_out, g_ffn2, w_ffn2_in, w_ffn2_out, g_final):
    batch, seq, d = x.shape
    depth = w_ada.shape[0]
    t = batch * seq
    xt = x.reshape(t, d)
    pos = positions.reshape(t, 1)
    lane_dim = jnp.arange(LANES, dtype=jnp.int32) % (ROT_DIM // 2)
    invf = (ROPE_THETA ** (-(2.0 * lane_dim.astype(F32)) / ROT_DIM)).reshape(1, LANES)
    for layer in range(depth):
        mod = _ada_call(c, w_ada[layer], b_ada[layer]).reshape(batch, N_SUB * N_MOD, d)
        last = layer == depth - 1
        xt = _ffn_call(xt, mod, g_ffn1[layer], w_ffn1_in[layer].astype(BF16),
                       w_ffn1_out[layer].astype(BF16), g_final, sub=0, seq=seq, final_norm=False)
        qa, ka, va, qb, kb, vb = _inproj_call(xt, mod, g_mix[layer], w_in[layer].astype(BF16),
                                              b_in[layer], pos, invf, sub=1, seq=seq)
        out_a = _attn_a_call(sinks[layer], qa, ka, va, batch=batch, seq=seq)
        out_b = _attn_b_call(qb, kb, vb, batch=batch, seq=seq).reshape(t, WIDTH_B)
        xt = _outproj_call(out_a, out_b, xt, mod, g_out_a[layer], g_out_b[layer],
                           w_out[layer].astype(BF16), b_out[layer], sub=1, seq=seq)
        xt = _ffn_call(xt, mod, g_ffn2[layer], w_ffn2_in[layer].astype(BF16),
                       w_ffn2_out[layer].astype(BF16), g_final, sub=2, seq=seq, final_norm=last)
    if depth == 0:
        raise ValueError("depth must be >= 1")
    return xt.reshape(batch, seq, d)
```

```python
import functools

import jax
import jax.numpy as jnp
from jax import lax
from jax.experimental import pallas as pl
from jax.experimental.pallas import tpu as pltpu

F32 = jnp.float32
BF16 = jnp.bfloat16

D_MODEL = 2048
HEAD_DIM = 64
N_HEADS_A = 16
N_KV_A = 2
GROUP_A = N_HEADS_A // N_KV_A
N_HEADS_B = 16
WIDTH_A = N_HEADS_A * HEAD_DIM
WIDTH_KV_A = N_KV_A * HEAD_DIM
WIDTH_B = N_HEADS_B * HEAD_DIM
IN_WIDTH = WIDTH_A + 2 * WIDTH_KV_A + 3 * WIDTH_B
WINDOW_A = 128
DILATED = ((128, 1), (512, 4), (2048, 16))
BLOCK = 128
ROPE_THETA = 500000.0
ROT_DIM = HEAD_DIM // 4
D_FF = 5632
FFN_RES = 0.5
N_SUB = 3
N_MOD = 3
EPS = 1e-5

LANES = 128
MXU_COLS = 256
NEG_BIG = -1e30
VMEM_LIMIT = 56 * 1024 * 1024

TM_FFN = 512
TF_FFN = 512
TM_PROJ = 512
TN_ADA = 1024
BF16_ROWS = 16
NORM_ROWS_STEP = 48
UNROLL_B = 4


def _silu(x):
    return x * (1.0 / (1.0 + jnp.exp(-x)))


def _rms(x, g):
    ms = jnp.mean(x * x, axis=-1, keepdims=True)
    return (x * lax.rsqrt(ms + EPS)) * g


def _modulated_norm(x, g_ref, mod_ref, sub):
    shift = mod_ref[0, pl.ds(3 * sub + 0, 1), :]
    scale = mod_ref[0, pl.ds(3 * sub + 1, 1), :]
    return (_rms(x, g_ref[...]) * (1.0 + scale) + shift).astype(BF16)


def _ada_kernel(c_ref, w_ref, b_ref, o_ref):
    cond = _silu(c_ref[...]).astype(BF16)
    o_ref[...] = jnp.dot(cond, w_ref[...].astype(BF16), preferred_element_type=F32) + b_ref[...]


def _ada_call(c, w_ada, b_ada):
    b, d = c.shape
    n = w_ada.shape[1]
    return pl.pallas_call(
        _ada_kernel,
        grid=(n // TN_ADA,),
        in_specs=[
            pl.BlockSpec((b, d), lambda j: (0, 0)),
            pl.BlockSpec((d, TN_ADA), lambda j: (0, j)),
            pl.BlockSpec((1, TN_ADA), lambda j: (0, j)),
        ],
        out_specs=pl.BlockSpec((b, TN_ADA), lambda j: (0, j)),
        out_shape=jax.ShapeDtypeStruct((b, n), F32),
        compiler_params=pltpu.CompilerParams(
            dimension_semantics=("arbitrary",), vmem_limit_bytes=VMEM_LIMIT),
        name="adaln_mod",
    )(c, w_ada, b_ada.reshape(1, n))


def _ffn_kernel(x_ref, xn_ref, mod_ref, modn_ref, g_ref, wg_ref, wu_ref, wd_ref, gfin_ref, o_ref,
                h_ref, hn_ref, acc_ref, *, sub, n_chunks, final_norm):
    i = pl.program_id(0)
    j = pl.program_id(1)
    tm = x_ref.shape[0]

    @pl.when((j == 0) & (i == 0))
    def _():
        h_ref[...] = _modulated_norm(x_ref[...], g_ref, mod_ref, sub)

    @pl.when((j == 0) & (i > 0))
    def _():
        h_ref[...] = hn_ref[...]

    @pl.when(j == 0)
    def _():
        acc_ref[...] = jnp.zeros_like(acc_ref)

    h = h_ref[...]
    gate = jnp.dot(h, wg_ref[...], preferred_element_type=F32)
    up = jnp.dot(h, wu_ref[...], preferred_element_type=F32)
    act = (_silu(gate) * up).astype(BF16)
    acc_ref[...] += jnp.dot(act, wd_ref[...], preferred_element_type=F32)

    start = pl.multiple_of(jnp.minimum(j * NORM_ROWS_STEP, tm - NORM_ROWS_STEP), BF16_ROWS)
    rows = pl.ds(start, NORM_ROWS_STEP)
    hn_ref[rows, :] = _modulated_norm(xn_ref[rows, :], g_ref, modn_ref, sub)

    @pl.when(j == n_chunks - 1)
    def _():
        gmod = mod_ref[0, pl.ds(3 * sub + 2, 1), :]
        out = x_ref[...] + (FFN_RES * gmod) * acc_ref[...]
        if final_norm:
            out = _rms(out, gfin_ref[...])
        o_ref[...] = out


def _ffn_call(x, mod, g, w_gu, w_down, g_final, *, sub, seq, final_norm):
    t, d = x.shape
    dff = w_down.shape[0]
    tm = TM_FFN
    n_chunks = dff // TF_FFN
    n_tiles = t // tm
    tiles_per_batch = seq // tm
    assert NORM_ROWS_STEP * n_chunks >= tm and NORM_ROWS_STEP % BF16_ROWS == 0
    nxt = lambda i: jnp.minimum(i + 1, n_tiles - 1)
    kern = functools.partial(_ffn_kernel, sub=sub, n_chunks=n_chunks, final_norm=final_norm)
    return pl.pallas_call(
        kern,
        grid=(n_tiles, n_chunks),
        in_specs=[
            pl.BlockSpec((tm, d), lambda i, j: (i, 0)),
            pl.BlockSpec((tm, d), lambda i, j: (nxt(i), 0)),
            pl.BlockSpec((1, N_SUB * N_MOD, d), lambda i, j: (i // tiles_per_batch, 0, 0)),
            pl.BlockSpec((1, N_SUB * N_MOD, d), lambda i, j: (nxt(i) // tiles_per_batch, 0, 0)),
            pl.BlockSpec((1, d), lambda i, j: (0, 0)),
            pl.BlockSpec((d, TF_FFN), lambda i, j: (0, j)),
            pl.BlockSpec((d, TF_FFN), lambda i, j: (0, n_chunks + j)),
            pl.BlockSpec((TF_FFN, d), lambda i, j: (j, 0)),
            pl.BlockSpec((1, d), lambda i, j: (0, 0)),
        ],
        out_specs=pl.BlockSpec((tm, d), lambda i, j: (i, 0)),
        out_shape=jax.ShapeDtypeStruct((t, d), F32),
        scratch_shapes=[pltpu.VMEM((tm, d), BF16), pltpu.VMEM((tm, d), BF16),
                        pltpu.VMEM((tm, d), F32)],
        compiler_params=pltpu.CompilerParams(
            dimension_semantics=("arbitrary", "arbitrary"), vmem_limit_bytes=VMEM_LIMIT),
        name="ffn_sub%d" % sub,
    )(x, x, mod, mod, g.reshape(1, d), w_gu, w_gu, w_down, g_final.reshape(1, d))


def _rope_tables(pos_ref, invf_ref):
    lane = lax.broadcasted_iota(jnp.int32, (1, LANES), 1)
    d = lane % HEAD_DIM
    ang = pos_ref[...].astype(F32) * invf_ref[...]
    c = jnp.cos(ang)
    s = jnp.sin(ang)
    half = ROT_DIM // 2
    coef_self = jnp.where(d < ROT_DIM, c, 1.0)
    coef_next = jnp.where(d < half, -s, 0.0)
    coef_prev = jnp.where((d >= half) & (d < ROT_DIM), s, 0.0)
    return coef_self, coef_next, coef_prev


def _rope_slab(z, tables):
    coef_self, coef_next, coef_prev = tables
    half = ROT_DIM // 2
    nxt = pltpu.roll(z, LANES - half, axis=1)
    prv = pltpu.roll(z, half, axis=1)
    return z * coef_self + nxt * coef_next + prv * coef_prev


def _inproj_kernel(x_ref, mod_ref, g_ref, w_ref, b_ref, pos_ref, invf_ref,
                   qa_ref, ka_ref, va_ref, qb_ref, kb_ref, vb_ref, *, sub):
    h = _modulated_norm(x_ref[...], g_ref, mod_ref, sub)
    tables = _rope_tables(pos_ref, invf_ref)
    lane = lax.broadcasted_iota(jnp.int32, (1, LANES), 1)
    lo = lane < HEAD_DIM
    qscale = HEAD_DIM ** -0.5

    def proj_slabs(col, width):
        for c0 in range(0, width, MXU_COLS):
            z = (jnp.dot(h, w_ref[:, col + c0:col + c0 + MXU_COLS], preferred_element_type=F32)
                 + b_ref[:, col + c0:col + c0 + MXU_COLS])
            for half in range(MXU_COLS // LANES):
                yield c0 // LANES + half, z[:, half * LANES:(half + 1) * LANES]

    def dup_heads(slab):
        swapped = pltpu.roll(slab, HEAD_DIM, axis=1)
        return jnp.where(lo, slab, swapped), jnp.where(lo, swapped, slab)

    col = 0
    for k, z in proj_slabs(col, WIDTH_A):
        qa_ref[:, k * LANES:(k + 1) * LANES] = (_rope_slab(z, tables) * qscale).astype(BF16)
    col += WIDTH_A
    for k, z in proj_slabs(col, 2 * WIDTH_KV_A):
        dst = ka_ref if k == 0 else va_ref
        h0, h1 = dup_heads(_rope_slab(z, tables) if k == 0 else z)
        dst[:, 0:LANES] = h0.astype(BF16)
        dst[:, LANES:2 * LANES] = h1.astype(BF16)
    col += 2 * WIDTH_KV_A
    for k, z in proj_slabs(col, WIDTH_B):
        qb_ref[:, k * LANES:(k + 1) * LANES] = _rope_slab(z, tables) * qscale
    col += WIDTH_B
    for k, z in proj_slabs(col, WIDTH_B):
        kb_ref[:, k * LANES:(k + 1) * LANES] = _rope_slab(z, tables)
    col += WIDTH_B
    for k, z in proj_slabs(col, WIDTH_B):
        vb_ref[:, k * LANES:(k + 1) * LANES] = z


def _inproj_call(x, mod, g, w_in, b_in, pos, invf, *, sub, seq):
    t, d = x.shape
    tm = TM_PROJ
    tiles_per_batch = seq // tm
    row = lambda i: (i, 0)
    const = lambda i: (0, 0)
    return pl.pallas_call(
        functools.partial(_inproj_kernel, sub=sub),
        grid=(t // tm,),
        in_specs=[
            pl.BlockSpec((tm, d), row),
            pl.BlockSpec((1, N_SUB * N_MOD, d), lambda i: (i // tiles_per_batch, 0, 0)),
            pl.BlockSpec((1, d), const),
            pl.BlockSpec((d, IN_WIDTH), const, pipeline_mode=pl.Buffered(1)),
            pl.BlockSpec((1, IN_WIDTH), const),
            pl.BlockSpec((tm, 1), row),
            pl.BlockSpec((1, LANES), const),
        ],
        out_specs=[
            pl.BlockSpec((tm, WIDTH_A), row),
            pl.BlockSpec((tm, 2 * LANES), row),
            pl.BlockSpec((tm, 2 * LANES), row),
            pl.BlockSpec((tm, WIDTH_B), row),
            pl.BlockSpec((tm, WIDTH_B), row),
            pl.BlockSpec((tm, WIDTH_B), row),
        ],
        out_shape=[
            jax.ShapeDtypeStruct((t, WIDTH_A), BF16),
            jax.ShapeDtypeStruct((t, 2 * LANES), BF16),
            jax.ShapeDtypeStruct((t, 2 * LANES), BF16),
            jax.ShapeDtypeStruct((t, WIDTH_B), F32),
            jax.ShapeDtypeStruct((t, WIDTH_B), F32),
            jax.ShapeDtypeStruct((t, WIDTH_B), F32),
        ],
        compiler_params=pltpu.CompilerParams(
            dimension_semantics=("arbitrary",), vmem_limit_bytes=VMEM_LIMIT),
        name="inproj_rope",
    )(x, mod, g.reshape(1, d), w_in, b_in.reshape(1, IN_WIDTH), pos, invf)


def _attn_a_kernel(sink_ref, q_ref, kp_ref, kc_ref, vp_ref, vc_ref, o_ref):
    i = pl.program_id(1)
    q = q_ref[...]
    lane = lax.broadcasted_iota(jnp.int32, (1, LANES), 1)
    lo = lane < HEAD_DIM
    qi = lax.broadcasted_iota(jnp.int32, (BLOCK, 2 * BLOCK), 0)
    kj = lax.broadcasted_iota(jnp.int32, (BLOCK, 2 * BLOCK), 1)
    dist = qi + BLOCK - kj
    valid = (dist >= 0) & (dist <= WINDOW_A - 1) & ((kj >= BLOCK) | (i > 0))
    bias = jnp.where(valid, 0.0, NEG_BIG).astype(F32)
    zero = jnp.zeros((BLOCK, LANES), BF16)
    pairs = GROUP_A // 2
    for hk in range(N_KV_A):
        kk = jnp.concatenate([kp_ref[:, hk * LANES:(hk + 1) * LANES],
                              kc_ref[:, hk * LANES:(hk + 1) * LANES]], axis=0)
        vv = jnp.concatenate([vp_ref[:, hk * LANES:(hk + 1) * LANES],
                              vc_ref[:, hk * LANES:(hk + 1) * LANES]], axis=0)
        rows = []
        for jp in range(pairs):
            qp = q[:, (hk * pairs + jp) * LANES:(hk * pairs + jp + 1) * LANES]
            rows.append(jnp.where(lo, qp, zero))
            rows.append(jnp.where(lo, zero, qp))
        lhs = jnp.concatenate(rows, axis=0)
        s_all = lax.dot_general(lhs, kk, (((1,), (1,)), ((), ())),
                                preferred_element_type=F32)
        ps, inv_ls = [], []
        for g in range(GROUP_A):
            sink = sink_ref[hk * GROUP_A + g]
            s = s_all[g * BLOCK:(g + 1) * BLOCK, :] + bias
            m = jnp.maximum(jnp.max(s, axis=-1, keepdims=True), sink)
            p = jnp.exp(s - m)
            l = jnp.sum(p, axis=-1, keepdims=True) + jnp.exp(sink - m)
            ps.append(p.astype(BF16))
            inv_ls.append(1.0 / l)
        o_all = jnp.dot(jnp.concatenate(ps, axis=0), vv, preferred_element_type=F32)
        for jp in range(pairs):
            g0, g1 = 2 * jp, 2 * jp + 1
            o0 = o_all[g0 * BLOCK:(g0 + 1) * BLOCK, :] * inv_ls[g0]
            o1 = o_all[g1 * BLOCK:(g1 + 1) * BLOCK, :] * inv_ls[g1]
            c0 = (hk * pairs + jp) * LANES
            o_ref[:, c0:c0 + LANES] = jnp.where(lo, o0, o1)


def _attn_a_call(sinks, qa, ka, va, *, batch, seq):
    nb = seq // BLOCK
    cur = lambda b, i: (b * nb + i, 0)
    prev = lambda b, i: (b * nb + jnp.maximum(i - 1, 0), 0)
    return pl.pallas_call(
        _attn_a_kernel,
        grid=(batch, nb),
        in_specs=[
            pl.BlockSpec(memory_space=pltpu.SMEM),
            pl.BlockSpec((BLOCK, WIDTH_A), cur),
            pl.BlockSpec((BLOCK, 2 * LANES), prev),
            pl.BlockSpec((BLOCK, 2 * LANES), cur),
            pl.BlockSpec((BLOCK, 2 * LANES), prev),
            pl.BlockSpec((BLOCK, 2 * LANES), cur),
        ],
        out_specs=pl.BlockSpec((BLOCK, WIDTH_A), cur),
        out_shape=jax.ShapeDtypeStruct((batch * seq, WIDTH_A), F32),
        compiler_params=pltpu.CompilerParams(
            dimension_semantics=("arbitrary", "arbitrary"), vmem_limit_bytes=VMEM_LIMIT),
        name="attn_swa_gqa",
    )(sinks, qa, ka, ka, va, va)


def _attn_b_kernel(q_ref, k_ref, v_ref, o_ref, m_sc, l_sc, acc_sc, *, seq):
    lane = lax.broadcasted_iota(jnp.int32, (1, LANES), 1)
    lo = lane < HEAD_DIM
    qi = lax.broadcasted_iota(jnp.int32, (BLOCK, BLOCK), 0)
    kj = lax.broadcasted_iota(jnp.int32, (BLOCK, BLOCK), 1)
    bias_cur = jnp.where(kj <= qi, 0.0, NEG_BIG).astype(F32)
    bias_prev = jnp.where(kj >= qi, 0.0, NEG_BIG).astype(F32)
    zero = jnp.zeros((BLOCK, LANES), BF16)

    def rows(dil, start):
        if dil == 1:
            return pl.ds(pl.multiple_of(start, BLOCK), BLOCK)
        return pl.ds(start, BLOCK, stride=dil)

    def block_stats(q, kcat, vcat, bias):
        qb = q.astype(BF16)
        lhs = jnp.concatenate([jnp.where(lo, qb, zero), jnp.where(lo, zero, qb)], axis=0)
        s = lax.dot_general(lhs, kcat.astype(BF16), (((1,), (1,)), ((), ())),
                            preferred_element_type=F32)
        s0 = s[:BLOCK, :] + bias
        s1 = s[BLOCK:, :] + bias
        m0 = jnp.max(s0, axis=-1, keepdims=True)
        m1 = jnp.max(s1, axis=-1, keepdims=True)
        p0 = jnp.exp(s0 - m0)
        p1 = jnp.exp(s1 - m1)
        l0 = jnp.sum(p0, axis=-1, keepdims=True)
        l1 = jnp.sum(p1, axis=-1, keepdims=True)
        pv = jnp.dot(jnp.concatenate([p0, p1], axis=0).astype(BF16), vcat.astype(BF16),
                     preferred_element_type=F32)
        acc = jnp.where(lo, pv[:BLOCK, :], pv[BLOCK:, :])
        m = jnp.where(lo, m0, m1)
        l = jnp.where(lo, l0, l1)
        return m, l, acc

    n_iter = seq // BLOCK
    for window, dil in DILATED:
        assert window // dil == BLOCK and (seq // dil) % BLOCK == 0

    def body(idx, carry):
        for cfg, (_, dil) in enumerate(DILATED):
            nblk = seq // dil // BLOCK
            r = idx // nblk
            n = idx % nblk
            start = r + n * (BLOCK * dil)
            cur = rows(dil, start)
            q = q_ref[0, cur, :]
            kc = k_ref[0, cur, :]
            vc = v_ref[0, cur, :]
            if nblk == 1:
                m, l, acc = block_stats(q, kc, vc, bias_cur)
            else:
                prev = rows(dil, jnp.maximum(start - BLOCK * dil, r))
                kp = k_ref[0, prev, :]
                vp = v_ref[0, prev, :]
                off = jnp.where(n > 0, 0.0, NEG_BIG).astype(F32)
                bias = jnp.concatenate([bias_prev + off, bias_cur], axis=1)
                m, l, acc = block_stats(q, jnp.concatenate([kp, kc], axis=0),
                                        jnp.concatenate([vp, vc], axis=0), bias)
            m_sc[cfg, cur, :] = m
            l_sc[cfg, cur, :] = l
            acc_sc[cfg, cur, :] = acc
        return carry

    lax.fori_loop(0, n_iter, body, 0, unroll=UNROLL_B)

    def merge(i, carry):
        sl = pl.ds(pl.multiple_of(i * BLOCK, BLOCK), BLOCK)
        ms = [m_sc[c, sl, :] for c in range(len(DILATED))]
        m_all = functools.reduce(jnp.maximum, ms)
        ws = [jnp.exp(m - m_all) for m in ms]
        l_all = sum(w * l_sc[c, sl, :] for c, w in enumerate(ws))
        acc_all = sum(w * acc_sc[c, sl, :] for c, w in enumerate(ws))
        o_ref[0, sl, :] = acc_all * (1.0 / l_all)
        return carry

    lax.fori_loop(0, n_iter, merge, 0, unroll=2)


def _attn_b_call(qb, kb, vb, *, batch, seq):
    spec = pl.BlockSpec((1, seq, LANES), lambda b, hp: (b, 0, hp))
    shape3 = (batch, seq, WIDTH_B)
    return pl.pallas_call(
        functools.partial(_attn_b_kernel, seq=seq),
        grid=(batch, WIDTH_B // LANES),
        in_specs=[spec, spec, spec],
        out_specs=spec,
        out_shape=jax.ShapeDtypeStruct(shape3, F32),
        scratch_shapes=[pltpu.VMEM((len(DILATED), seq, LANES), F32)] * 3,
        compiler_params=pltpu.CompilerParams(
            dimension_semantics=("arbitrary", "arbitrary"), vmem_limit_bytes=VMEM_LIMIT),
        name="attn_dilated",
    )(qb.reshape(shape3), kb.reshape(shape3), vb.reshape(shape3))


def _outproj_kernel(oa_ref, ob_ref, x_ref, mod_ref, ga_ref, gb_ref, w_ref, b_ref, o_ref, *, sub):
    ya = _rms(oa_ref[...], ga_ref[...]).astype(BF16)
    yb = _rms(ob_ref[...], gb_ref[...]).astype(BF16)
    y = (jnp.dot(ya, w_ref[0:WIDTH_A, :], preferred_element_type=F32)
         + jnp.dot(yb, w_ref[WIDTH_A:WIDTH_A + WIDTH_B, :], preferred_element_type=F32)
         + b_ref[...])
    gmod = mod_ref[0, pl.ds(3 * sub + 2, 1), :]
    o_ref[...] = x_ref[...] + gmod * y


def _outproj_call(out_a, out_b, x, mod, g_a, g_b, w_out, b_out, *, sub, seq):
    t, d = x.shape
    tm = TM_PROJ
    tiles_per_batch = seq // tm
    row = lambda i: (i, 0)
    const = lambda i: (0, 0)
    return pl.pallas_call(
        functools.partial(_outproj_kernel, sub=sub),
        grid=(t // tm,),
        in_specs=[
            pl.BlockSpec((tm, WIDTH_A), row),
            pl.BlockSpec((tm, WIDTH_B), row),
            pl.BlockSpec((tm, d), row),
            pl.BlockSpec((1, N_SUB * N_MOD, d), lambda i: (i // tiles_per_batch, 0, 0)),
            pl.BlockSpec((1, WIDTH_A), const),
            pl.BlockSpec((1, WIDTH_B), const),
            pl.BlockSpec((WIDTH_A + WIDTH_B, d), const),
            pl.BlockSpec((1, d), const),
        ],
        out_specs=pl.BlockSpec((tm, d), row),
        out_shape=jax.ShapeDtypeStruct((t, d), F32),
        compiler_params=pltpu.CompilerParams(
            dimension_semantics=("arbitrary",), vmem_limit_bytes=VMEM_LIMIT),
        name="outproj_residual",
    )(out_a, out_b, x, mod, g_a.reshape(1, WIDTH_A), g_b.reshape(1, WIDTH_B), w_out,
      b_out.reshape(1, d))


def kernel(x, c, positions, w_ada, b_ada, g_ffn1, w_ffn1_in, w_ffn1_out, g_mix, w_in, b_in, sinks,
           g_out_a, g_out_b, w_out, b_out, g_ffn2, w_ffn2_in, w_ffn2_out, g_final):
    batch, seq, d = x.shape
    depth = w_ada.shape[0]
    t = batch * seq
    xt = x.reshape(t, d)
    pos = positions.reshape(t, 1)
    lane_dim = jnp.arange(LANES, dtype=jnp.int32) % (ROT_DIM // 2)
    invf = (ROPE_THETA ** (-(2.0 * lane_dim.astype(F32)) / ROT_DIM)).reshape(1, LANES)
    for layer in range(depth):
        mod = _ada_call(c, w_ada[layer], b_ada[layer]).reshape(batch, N_SUB * N_MOD, d)
        last = layer == depth - 1
        xt = _ffn_call(xt, mod, g_ffn1[layer], w_ffn1_in[layer].astype(BF16),
                       w_ffn1_out[layer].astype(BF16), g_final, sub=0, seq=seq, final_norm=False)
        qa, ka, va, qb, kb, vb = _inproj_call(xt, mod, g_mix[layer], w_in[layer].astype(BF16),
                                              b_in[layer], pos, invf, sub=1, seq=seq)
        out_a = _attn_a_call(sinks[layer], qa, ka, va, batch=batch, seq=seq)
        out_b = _attn_b_call(qb, kb, vb, batch=batch, seq=seq).reshape(t, WIDTH_B)
        xt = _outproj_call(out_a, out_b, xt, mod, g_out_a[layer], g_out_b[layer],
                           w_out[layer].astype(BF16), b_out[layer], sub=1, seq=seq)
        xt = _ffn_call(xt, mod, g_ffn2[layer], w_ffn2_in[layer].astype(BF16),
                       w_ffn2_out[layer].astype(BF16), g_final, sub=2, seq=seq, final_norm=last)
    if depth == 0:
        raise ValueError("depth must be >= 1")
    return xt.reshape(batch, seq, d)
```

```python
import functools

import jax
import jax.numpy as jnp
from jax import lax
from jax.experimental import pallas as pl
from jax.experimental.pallas import tpu as pltpu

F32 = jnp.float32
BF16 = jnp.bfloat16

D_MODEL = 2048
HEAD_DIM = 64
N_HEADS_A = 16
N_KV_A = 2
GROUP_A = N_HEADS_A // N_KV_A
N_HEADS_B = 16
WIDTH_A = N_HEADS_A * HEAD_DIM
WIDTH_KV_A = N_KV_A * HEAD_DIM
WIDTH_B = N_HEADS_B * HEAD_DIM
IN_WIDTH = WIDTH_A + 2 * WIDTH_KV_A + 3 * WIDTH_B
WINDOW_A = 128
DILATED = ((128, 1), (512, 4), (2048, 16))
BLOCK = 128
ROPE_THETA = 500000.0
ROT_DIM = HEAD_DIM // 4
D_FF = 5632
FFN_RES = 0.5
N_SUB = 3
N_MOD = 3
EPS = 1e-5

LANES = 128
MXU_COLS = 256
NEG_BIG = -1e30
VMEM_LIMIT = 56 * 1024 * 1024

TM_FFN = 512
TF_FFN = 512
TM_PROJ = 512
TN_ADA = 1024
NORM_ROWS_STEP = 64
QBLOCKS_A = 2
UNROLL_B = 4


def _silu(x):
    return x * (1.0 / (1.0 + jnp.exp(-x)))


def _rms(x, g):
    ms = jnp.mean(x * x, axis=-1, keepdims=True)
    return (x * lax.rsqrt(ms + EPS)) * g


def _modulated_norm(x, g_ref, mod_ref, sub):
    shift = mod_ref[0, pl.ds(3 * sub + 0, 1), :]
    scale = mod_ref[0, pl.ds(3 * sub + 1, 1), :]
    return (_rms(x, g_ref[...]) * (1.0 + scale) + shift).astype(BF16)


def _ada_kernel(c_ref, w_ref, b_ref, o_ref):
    cond = _silu(c_ref[...]).astype(BF16)
    o_ref[...] = jnp.dot(cond, w_ref[...].astype(BF16), preferred_element_type=F32) + b_ref[...]


def _ada_call(c, w_ada, b_ada):
    b, d = c.shape
    n = w_ada.shape[1]
    return pl.pallas_call(
        _ada_kernel,
        grid=(n // TN_ADA,),
        in_specs=[
            pl.BlockSpec((b, d), lambda j: (0, 0)),
            pl.BlockSpec((d, TN_ADA), lambda j: (0, j)),
            pl.BlockSpec((1, TN_ADA), lambda j: (0, j)),
        ],
        out_specs=pl.BlockSpec((b, TN_ADA), lambda j: (0, j)),
        out_shape=jax.ShapeDtypeStruct((b, n), F32),
        compiler_params=pltpu.CompilerParams(
            dimension_semantics=("arbitrary",), vmem_limit_bytes=VMEM_LIMIT),
        name="adaln_mod",
    )(c, w_ada, b_ada.reshape(1, n))


def _ffn_kernel(x_ref, xn_ref, mod_ref, modn_ref, g_ref, wg_ref, wu_ref, wd_ref, gfin_ref, o_ref,
                h_ref, hn_ref, acc_ref, *, sub, n_chunks, final_norm):
    i = pl.program_id(0)
    j = pl.program_id(1)
    tm = x_ref.shape[0]

    @pl.when((j == 0) & (i == 0))
    def _():
        h_ref[...] = _modulated_norm(x_ref[...], g_ref, mod_ref, sub)

    @pl.when((j == 0) & (i > 0))
    def _():
        h_ref[...] = hn_ref[...]

    @pl.when(j == 0)
    def _():
        acc_ref[...] = jnp.zeros_like(acc_ref)

    h = h_ref[...]
    gate = jnp.dot(h, wg_ref[...], preferred_element_type=F32)
    up = jnp.dot(h, wu_ref[...], preferred_element_type=F32)
    act = (_silu(gate) * up).astype(BF16)
    acc_ref[...] += jnp.dot(act, wd_ref[...], preferred_element_type=F32)

    slot = jnp.minimum(j, tm // NORM_ROWS_STEP - 1)
    rows = pl.ds(pl.multiple_of(slot * NORM_ROWS_STEP, NORM_ROWS_STEP), NORM_ROWS_STEP)
    hn_ref[rows, :] = _modulated_norm(xn_ref[...], g_ref, modn_ref, sub)

    @pl.when(j == n_chunks - 1)
    def _():
        gmod = mod_ref[0, pl.ds(3 * sub + 2, 1), :]
        out = x_ref[...] + (FFN_RES * gmod) * acc_ref[...]
        if final_norm:
            out = _rms(out, gfin_ref[...])
        o_ref[...] = out


def _ffn_call(x, mod, g, w_gu, w_down, g_final, *, sub, seq, final_norm):
    t, d = x.shape
    dff = w_down.shape[0]
    tm = TM_FFN
    n_chunks = dff // TF_FFN
    n_tiles = t // tm
    tiles_per_batch = seq // tm
    slices = tm // NORM_ROWS_STEP
    assert slices <= n_chunks and tm % NORM_ROWS_STEP == 0
    nxt = lambda i: jnp.minimum(i + 1, n_tiles - 1)
    kern = functools.partial(_ffn_kernel, sub=sub, n_chunks=n_chunks, final_norm=final_norm)
    return pl.pallas_call(
        kern,
        grid=(n_tiles, n_chunks),
        in_specs=[
            pl.BlockSpec((tm, d), lambda i, j: (i, 0)),
            pl.BlockSpec((NORM_ROWS_STEP, d),
                         lambda i, j: (nxt(i) * slices + jnp.minimum(j, slices - 1), 0)),
            pl.BlockSpec((1, N_SUB * N_MOD, d), lambda i, j: (i // tiles_per_batch, 0, 0)),
            pl.BlockSpec((1, N_SUB * N_MOD, d), lambda i, j: (nxt(i) // tiles_per_batch, 0, 0)),
            pl.BlockSpec((1, d), lambda i, j: (0, 0)),
            pl.BlockSpec((d, TF_FFN), lambda i, j: (0, j)),
            pl.BlockSpec((d, TF_FFN), lambda i, j: (0, n_chunks + j)),
            pl.BlockSpec((TF_FFN, d), lambda i, j: (j, 0)),
            pl.BlockSpec((1, d), lambda i, j: (0, 0)),
        ],
        out_specs=pl.BlockSpec((tm, d), lambda i, j: (i, 0)),
        out_shape=jax.ShapeDtypeStruct((t, d), F32),
        scratch_shapes=[pltpu.VMEM((tm, d), BF16), pltpu.VMEM((tm, d), BF16),
                        pltpu.VMEM((tm, d), F32)],
        compiler_params=pltpu.CompilerParams(
            dimension_semantics=("arbitrary", "arbitrary"), vmem_limit_bytes=VMEM_LIMIT),
        name="ffn_sub%d" % sub,
    )(x, x, mod, mod, g.reshape(1, d), w_gu, w_gu, w_down, g_final.reshape(1, d))


def _rope_tables(pos_ref, invf_ref):
    lane = lax.broadcasted_iota(jnp.int32, (1, LANES), 1)
    d = lane % HEAD_DIM
    ang = pos_ref[...].astype(F32) * invf_ref[...]
    c = jnp.cos(ang)
    s = jnp.sin(ang)
    half = ROT_DIM // 2
    coef_self = jnp.where(d < ROT_DIM, c, 1.0)
    coef_next = jnp.where(d < half, -s, 0.0)
    coef_prev = jnp.where((d >= half) & (d < ROT_DIM), s, 0.0)
    return coef_self, coef_next, coef_prev


def _rope_slab(z, tables):
    coef_self, coef_next, coef_prev = tables
    half = ROT_DIM // 2
    nxt = pltpu.roll(z, LANES - half, axis=1)
    prv = pltpu.roll(z, half, axis=1)
    return z * coef_self + nxt * coef_next + prv * coef_prev


def _inproj_kernel(x_ref, mod_ref, g_ref, w_ref, b_ref, pos_ref, invf_ref,
                   qa_ref, ka_ref, va_ref, qb_ref, kb_ref, vb_ref, *, sub):
    h = _modulated_norm(x_ref[...], g_ref, mod_ref, sub)
    tables = _rope_tables(pos_ref, invf_ref)
    lane = lax.broadcasted_iota(jnp.int32, (1, LANES), 1)
    lo = lane < HEAD_DIM
    qscale = HEAD_DIM ** -0.5

    def proj_slabs(col, width):
        for c0 in range(0, width, MXU_COLS):
            z = (jnp.dot(h, w_ref[:, col + c0:col + c0 + MXU_COLS], preferred_element_type=F32)
                 + b_ref[:, col + c0:col + c0 + MXU_COLS])
            for half in range(MXU_COLS // LANES):
                yield c0 // LANES + half, z[:, half * LANES:(half + 1) * LANES]

    def dup_heads(slab):
        swapped = pltpu.roll(slab, HEAD_DIM, axis=1)
        return jnp.where(lo, slab, swapped), jnp.where(lo, swapped, slab)

    col = 0
    for k, z in proj_slabs(col, WIDTH_A):
        qa_ref[:, k * LANES:(k + 1) * LANES] = (_rope_slab(z, tables) * qscale).astype(BF16)
    col += WIDTH_A
    for k, z in proj_slabs(col, 2 * WIDTH_KV_A):
        dst = ka_ref if k == 0 else va_ref
        h0, h1 = dup_heads(_rope_slab(z, tables) if k == 0 else z)
        dst[:, 0:LANES] = h0.astype(BF16)
        dst[:, LANES:2 * LANES] = h1.astype(BF16)
    col += 2 * WIDTH_KV_A
    for k, z in proj_slabs(col, WIDTH_B):
        qb_ref[:, k * LANES:(k + 1) * LANES] = _rope_slab(z, tables) * qscale
    col += WIDTH_B
    for k, z in proj_slabs(col, WIDTH_B):
        kb_ref[:, k * LANES:(k + 1) * LANES] = _rope_slab(z, tables)
    col += WIDTH_B
    for k, z in proj_slabs(col, WIDTH_B):
        vb_ref[:, k * LANES:(k + 1) * LANES] = z


def _inproj_call(x, mod, g, w_in, b_in, pos, invf, *, sub, seq):
    t, d = x.shape
    tm = TM_PROJ
    tiles_per_batch = seq // tm
    row = lambda i: (i, 0)
    const = lambda i: (0, 0)
    return pl.pallas_call(
        functools.partial(_inproj_kernel, sub=sub),
        grid=(t // tm,),
        in_specs=[
            pl.BlockSpec((tm, d), row),
            pl.BlockSpec((1, N_SUB * N_MOD, d), lambda i: (i // tiles_per_batch, 0, 0)),
            pl.BlockSpec((1, d), const),
            pl.BlockSpec((d, IN_WIDTH), const, pipeline_mode=pl.Buffered(1)),
            pl.BlockSpec((1, IN_WIDTH), const),
            pl.BlockSpec((tm, 1), row),
            pl.BlockSpec((1, LANES), const),
        ],
        out_specs=[
            pl.BlockSpec((tm, WIDTH_A), row),
            pl.BlockSpec((tm, 2 * LANES), row),
            pl.BlockSpec((tm, 2 * LANES), row),
            pl.BlockSpec((tm, WIDTH_B), row),
            pl.BlockSpec((tm, WIDTH_B), row),
            pl.BlockSpec((tm, WIDTH_B), row),
        ],
        out_shape=[
            jax.ShapeDtypeStruct((t, WIDTH_A), BF16),
            jax.ShapeDtypeStruct((t, 2 * LANES), BF16),
            jax.ShapeDtypeStruct((t, 2 * LANES), BF16),
            jax.ShapeDtypeStruct((t, WIDTH_B), F32),
            jax.ShapeDtypeStruct((t, WIDTH_B), F32),
            jax.ShapeDtypeStruct((t, WIDTH_B), F32),
        ],
        compiler_params=pltpu.CompilerParams(
            dimension_semantics=("arbitrary",), vmem_limit_bytes=VMEM_LIMIT),
        name="inproj_rope",
    )(x, mod, g.reshape(1, d), w_in, b_in.reshape(1, IN_WIDTH), pos, invf)


def _attn_a_kernel(sink_ref, q_ref, kp_ref, kc_ref, vp_ref, vc_ref, o_ref):
    i = pl.program_id(1)
    lane = lax.broadcasted_iota(jnp.int32, (1, LANES), 1)
    lo = lane < HEAD_DIM
    qi = lax.broadcasted_iota(jnp.int32, (BLOCK, 2 * BLOCK), 0)
    kj = lax.broadcasted_iota(jnp.int32, (BLOCK, 2 * BLOCK), 1)
    dist = qi + BLOCK - kj
    band = (dist >= 0) & (dist <= WINDOW_A - 1)
    bias_inner = jnp.where(band, 0.0, NEG_BIG).astype(F32)
    bias_first = jnp.where(band & ((kj >= BLOCK) | (i > 0)), 0.0, NEG_BIG).astype(F32)
    zero = jnp.zeros((BLOCK, LANES), BF16)
    ones = jnp.ones((2 * BLOCK, LANES), BF16)
    pairs = GROUP_A // 2
    for u in range(QBLOCKS_A):
        r0 = u * BLOCK
        bias = bias_first if u == 0 else bias_inner
        for hk in range(N_KV_A):
            cols = slice(hk * LANES, (hk + 1) * LANES)
            k_prev = kp_ref[:, cols] if u == 0 else kc_ref[r0 - BLOCK:r0, cols]
            v_prev = vp_ref[:, cols] if u == 0 else vc_ref[r0 - BLOCK:r0, cols]
            kk = jnp.concatenate([k_prev, kc_ref[r0:r0 + BLOCK, cols]], axis=0)
            vv = jnp.concatenate([v_prev, vc_ref[r0:r0 + BLOCK, cols]], axis=0)
            rows = []
            for jp in range(pairs):
                c0 = (hk * pairs + jp) * LANES
                qp = q_ref[r0:r0 + BLOCK, c0:c0 + LANES]
                rows.append(jnp.where(lo, qp, zero))
                rows.append(jnp.where(lo, zero, qp))
            lhs = jnp.concatenate(rows, axis=0)
            s_all = lax.dot_general(lhs, kk, (((1,), (1,)), ((), ())),
                                    preferred_element_type=F32)
            ps, ms = [], []
            for g in range(GROUP_A):
                sink = sink_ref[hk * GROUP_A + g]
                s = s_all[g * BLOCK:(g + 1) * BLOCK, :] + bias
                m = jnp.maximum(jnp.max(s, axis=-1, keepdims=True), sink)
                ps.append(jnp.exp(s - m).astype(BF16))
                ms.append(m)
            v3 = jnp.concatenate([vv, ones], axis=1)
            pv = jnp.dot(jnp.concatenate(ps, axis=0), v3, preferred_element_type=F32)
            outs = []
            for g in range(GROUP_A):
                sink = sink_ref[hk * GROUP_A + g]
                blk = pv[g * BLOCK:(g + 1) * BLOCK, :]
                l = blk[:, LANES:] + jnp.exp(sink - ms[g])
                outs.append(blk[:, :LANES] * (1.0 / l))
            for jp in range(pairs):
                c0 = (hk * pairs + jp) * LANES
                o_ref[r0:r0 + BLOCK, c0:c0 + LANES] = jnp.where(lo, outs[2 * jp], outs[2 * jp + 1])


def _attn_a_call(sinks, qa, ka, va, *, batch, seq):
    nb = seq // BLOCK
    rows = QBLOCKS_A * BLOCK
    steps = nb // QBLOCKS_A
    cur = lambda b, i: (b * steps + i, 0)
    prev = lambda b, i: (b * nb + jnp.maximum(QBLOCKS_A * i - 1, 0), 0)
    return pl.pallas_call(
        _attn_a_kernel,
        grid=(batch, steps),
        in_specs=[
            pl.BlockSpec(memory_space=pltpu.SMEM),
            pl.BlockSpec((rows, WIDTH_A), cur),
            pl.BlockSpec((BLOCK, 2 * LANES), prev),
            pl.BlockSpec((rows, 2 * LANES), cur),
            pl.BlockSpec((BLOCK, 2 * LANES), prev),
            pl.BlockSpec((rows, 2 * LANES), cur),
        ],
        out_specs=pl.BlockSpec((rows, WIDTH_A), cur),
        out_shape=jax.ShapeDtypeStruct((batch * seq, WIDTH_A), F32),
        compiler_params=pltpu.CompilerParams(
            dimension_semantics=("arbitrary", "arbitrary"), vmem_limit_bytes=VMEM_LIMIT),
        name="attn_swa_gqa",
    )(sinks, qa, ka, ka, va, va)


def _attn_b_kernel(q_ref, k_ref, v_ref, o_ref, x4_ref, m_sc, l_sc, acc_sc, *, seq):
    (_, d1), (_, d2), (_, d3) = DILATED
    assert d1 == 1 and d3 == d2 * d2 and all(w // dil == BLOCK for w, dil in DILATED)
    n_iter = seq // BLOCK
    cls = seq // d2
    nblk2 = cls // BLOCK
    assert seq // d3 == BLOCK and n_iter == d2 * nblk2 == d3

    lane = lax.broadcasted_iota(jnp.int32, (1, LANES), 1)
    lo = lane < HEAD_DIM
    qi = lax.broadcasted_iota(jnp.int32, (BLOCK, BLOCK), 0)
    kj = lax.broadcasted_iota(jnp.int32, (BLOCK, BLOCK), 1)
    bias_cur = jnp.where(kj <= qi, 0.0, NEG_BIG).astype(F32)
    bias_prev = jnp.where(kj >= qi, 0.0, NEG_BIG).astype(F32)
    zero = jnp.zeros((BLOCK, LANES), BF16)
    ones = jnp.ones((2 * BLOCK, LANES), BF16)

    for a, ref in enumerate((q_ref, k_ref, v_ref)):
        for r in range(d2):
            x4_ref[a, r * cls:(r + 1) * cls, :] = ref[0, pl.ds(r, cls, stride=d2), :]

    def block_stats(q, kcat, vcat, bias):
        nk = kcat.shape[0]
        qb = q.astype(BF16)
        lhs = jnp.concatenate([jnp.where(lo, qb, zero), jnp.where(lo, zero, qb)], axis=0)
        s = lax.dot_general(lhs, kcat.astype(BF16), (((1,), (1,)), ((), ())),
                            preferred_element_type=F32)
        s0 = s[:BLOCK, :] + bias
        s1 = s[BLOCK:, :] + bias
        m0 = jnp.max(s0, axis=-1, keepdims=True)
        m1 = jnp.max(s1, axis=-1, keepdims=True)
        p = jnp.concatenate([jnp.exp(s0 - m0), jnp.exp(s1 - m1)], axis=0).astype(BF16)
        v3 = jnp.concatenate([vcat.astype(BF16), ones[:nk, :]], axis=1)
        pv = jnp.dot(p, v3, preferred_element_type=F32)
        acc = jnp.where(lo, pv[:BLOCK, :LANES], pv[BLOCK:, :LANES])
        l = jnp.where(lo, pv[:BLOCK, LANES:], pv[BLOCK:, LANES:])
        m = jnp.where(lo, m0, m1)
        return m, l, acc

    def banded(cfg, qkv, cur, prev, has_prev):
        off = jnp.where(has_prev, 0.0, NEG_BIG).astype(F32)
        bias = jnp.concatenate([bias_prev + off, bias_cur], axis=1)
        stats = block_stats(qkv(0, cur),
                            jnp.concatenate([qkv(1, prev), qkv(1, cur)], axis=0),
                            jnp.concatenate([qkv(2, prev), qkv(2, cur)], axis=0), bias)
        for ref, val in zip((m_sc, l_sc, acc_sc), stats):
            ref[cfg, cur, :] = val

    natural = lambda a, rows: (q_ref, k_ref, v_ref)[a][0, rows, :]
    by_class = lambda a, rows: x4_ref[a, rows, :]

    def body(idx, carry):
        start = idx * BLOCK
        banded(0, natural, pl.ds(pl.multiple_of(start, BLOCK), BLOCK),
               pl.ds(pl.multiple_of(jnp.maximum(start - BLOCK, 0), BLOCK), BLOCK), idx > 0)
        n = idx % nblk2
        banded(1, by_class, pl.ds(pl.multiple_of(start, BLOCK), BLOCK),
               pl.ds(pl.multiple_of(jnp.maximum(start - BLOCK, start - n * BLOCK), BLOCK), BLOCK),
               n > 0)
        rows3 = pl.ds((idx % d2) * cls + idx // d2, BLOCK, stride=d2)
        stats = block_stats(by_class(0, rows3), by_class(1, rows3), by_class(2, rows3), bias_cur)
        for ref, val in zip((m_sc, l_sc, acc_sc), stats):
            ref[2, rows3, :] = val
        return carry

    lax.fori_loop(0, n_iter, body, 0, unroll=UNROLL_B)

    def merge(i, carry):
        start = i * BLOCK
        xr = pl.ds(pl.multiple_of(start, BLOCK), BLOCK)
        nat = pl.ds(i // nblk2 + (i % nblk2) * (BLOCK * d2), BLOCK, stride=d2)
        sel = (nat, xr, xr)
        ms = [m_sc[c, sel[c], :] for c in range(len(DILATED))]
        m_all = functools.reduce(jnp.maximum, ms)
        ws = [jnp.exp(m - m_all) for m in ms]
        l_all = sum(w * l_sc[c, sel[c], :] for c, w in enumerate(ws))
        acc_all = sum(w * acc_sc[c, sel[c], :] for c, w in enumerate(ws))
        o_ref[0, nat, :] = acc_all * (1.0 / l_all)
        return carry

    lax.fori_loop(0, n_iter, merge, 0, unroll=2)


def _attn_b_call(qb, kb, vb, *, batch, seq):
    spec = pl.BlockSpec((1, seq, LANES), lambda b, hp: (b, 0, hp))
    shape3 = (batch, seq, WIDTH_B)
    return pl.pallas_call(
        functools.partial(_attn_b_kernel, seq=seq),
        grid=(batch, WIDTH_B // LANES),
        in_specs=[spec, spec, spec],
        out_specs=spec,
        out_shape=jax.ShapeDtypeStruct(shape3, F32),
        scratch_shapes=[pltpu.VMEM((3, seq, LANES), F32)]
        + [pltpu.VMEM((len(DILATED), seq, LANES), F32)] * 3,
        compiler_params=pltpu.CompilerParams(
            dimension_semantics=("arbitrary", "arbitrary"), vmem_limit_bytes=VMEM_LIMIT),
        name="attn_dilated",
    )(qb.reshape(shape3), kb.reshape(shape3), vb.reshape(shape3))


def _outproj_kernel(oa_ref, ob_ref, x_ref, mod_ref, ga_ref, gb_ref, w_ref, b_ref, o_ref, *, sub):
    ya = _rms(oa_ref[...], ga_ref[...]).astype(BF16)
    yb = _rms(ob_ref[...], gb_ref[...]).astype(BF16)
    y = (jnp.dot(ya, w_ref[0:WIDTH_A, :], preferred_element_type=F32)
         + jnp.dot(yb, w_ref[WIDTH_A:WIDTH_A + WIDTH_B, :], preferred_element_type=F32)
         + b_ref[...])
    gmod = mod_ref[0, pl.ds(3 * sub + 2, 1), :]
    o_ref[...] = x_ref[...] + gmod * y


def _outproj_call(out_a, out_b, x, mod, g_a, g_b, w_out, b_out, *, sub, seq):
    t, d = x.shape
    tm = TM_PROJ
    tiles_per_batch = seq // tm
    row = lambda i: (i, 0)
    const = lambda i: (0, 0)
    return pl.pallas_call(
        functools.partial(_outproj_kernel, sub=sub),
        grid=(t // tm,),
        in_specs=[
            pl.BlockSpec((tm, WIDTH_A), row),
            pl.BlockSpec((tm, WIDTH_B), row),
            pl.BlockSpec((tm, d), row),
            pl.BlockSpec((1, N_SUB * N_MOD, d), lambda i: (i // tiles_per_batch, 0, 0)),
            pl.BlockSpec((1, WIDTH_A), const),
            pl.BlockSpec((1, WIDTH_B), const),
            pl.BlockSpec((WIDTH_A + WIDTH_B, d), const),
            pl.BlockSpec((1, d), const),
        ],
        out_specs=pl.BlockSpec((tm, d), row),
        out_shape=jax.ShapeDtypeStruct((t, d), F32),
        compiler_params=pltpu.CompilerParams(
            dimension_semantics=("arbitrary",), vmem_limit_bytes=VMEM_LIMIT),
        name="outproj_residual",
    )(out_a, out_b, x, mod, g_a.reshape(1, WIDTH_A), g_b.reshape(1, WIDTH_B), w_out,
      b_out.reshape(1, d))


def kernel(x, c, positions, w_ada, b_ada, g_ffn1, w_ffn1_in, w_ffn1_out, g_mix, w_in, b_in, sinks,
           g_out_a, g_out_b, w_out, b_out, g_ffn2, w_ffn2_in, w_ffn2_out, g_final):
    batch, seq, d = x.shape
    depth = w_ada.shape[0]
    t = batch * seq
    xt = x.reshape(t, d)
    pos = positions.reshape(t, 1)
    lane_dim = jnp.arange(LANES, dtype=jnp.int32) % (ROT_DIM // 2)
    invf = (ROPE_THETA ** (-(2.0 * lane_dim.astype(F32)) / ROT_DIM)).reshape(1, LANES)
    for layer in range(depth):
        mod = _ada_call(c, w_ada[layer], b_ada[layer]).reshape(batch, N_SUB * N_MOD, d)
        last = layer == depth - 1
        xt = _ffn_call(xt, mod, g_ffn1[layer], w_ffn1_in[layer].astype(BF16),
                       w_ffn1_out[layer].astype(BF16), g_final, sub=0, seq=seq, final_norm=False)
        qa, ka, va, qb, kb, vb = _inproj_call(xt, mod, g_mix[layer], w_in[layer].astype(BF16),
                                              b_in[layer], pos, invf, sub=1, seq=seq)
        out_a = _attn_a_call(sinks[layer], qa, ka, va, batch=batch, seq=seq)
        out_b = _attn_b_call(qb, kb, vb, batch=batch, seq=seq).reshape(t, WIDTH_B)
        xt = _outproj_call(out_a, out_b, xt, mod, g_out_a[layer], g_out_b[layer],
                           w_out[layer].astype(BF16), b_out[layer], sub=1, seq=seq)
        xt = _ffn_call(xt, mod, g_ffn2[layer], w_ffn2_in[layer].astype(BF16),
                       w_ffn2_out[layer].astype(BF16), g_final, sub=2, seq=seq, final_norm=last)
    if depth == 0:
        raise ValueError("depth must be >= 1")
    return xt.reshape(batch, seq, d)
```

```python
import functools

import jax
import jax.numpy as jnp
from jax import lax
from jax.experimental import pallas as pl
from jax.experimental.pallas import tpu as pltpu

F32 = jnp.float32
BF16 = jnp.bfloat16

D_MODEL = 2048
HEAD_DIM = 64
N_HEADS_A = 16
N_KV_A = 2
GROUP_A = N_HEADS_A // N_KV_A
N_HEADS_B = 16
WIDTH_A = N_HEADS_A * HEAD_DIM
WIDTH_KV_A = N_KV_A * HEAD_DIM
WIDTH_B = N_HEADS_B * HEAD_DIM
IN_WIDTH = WIDTH_A + 2 * WIDTH_KV_A + 3 * WIDTH_B
WINDOW_A = 128
DILATED = ((128, 1), (512, 4), (2048, 16))
BLOCK = 128
ROPE_THETA = 500000.0
ROT_DIM = HEAD_DIM // 4
D_FF = 5632
FFN_RES = 0.5
N_SUB = 3
N_MOD = 3
EPS = 1e-5

LANES = 128
MXU_COLS = 256
NEG_BIG = -1e30
VMEM_LIMIT = 56 * 1024 * 1024

TM_FFN = 512
TF_FFN = 512
TM_PROJ = 512
TN_ADA = 1024
BF16_ROWS = 16
NORM_ROWS_STEP = 64
QBLOCKS_A = 2
UNROLL_B = 4


def _silu(x):
    return x * (1.0 / (1.0 + jnp.exp(-x)))


def _rms(x, g):
    ms = jnp.mean(x * x, axis=-1, keepdims=True)
    return (x * lax.rsqrt(ms + EPS)) * g


def _modulated_norm(x, g_ref, mod_ref, sub):
    shift = mod_ref[0, pl.ds(3 * sub + 0, 1), :]
    scale = mod_ref[0, pl.ds(3 * sub + 1, 1), :]
    return (_rms(x, g_ref[...]) * (1.0 + scale) + shift).astype(BF16)


def _ada_kernel(c_ref, w_ref, b_ref, o_ref):
    cond = _silu(c_ref[...]).astype(BF16)
    o_ref[...] = jnp.dot(cond, w_ref[...].astype(BF16), preferred_element_type=F32) + b_ref[...]


def _ada_call(c, w_ada, b_ada):
    b, d = c.shape
    n = w_ada.shape[1]
    return pl.pallas_call(
        _ada_kernel,
        grid=(n // TN_ADA,),
        in_specs=[
            pl.BlockSpec((b, d), lambda j: (0, 0)),
            pl.BlockSpec((d, TN_ADA), lambda j: (0, j)),
            pl.BlockSpec((1, TN_ADA), lambda j: (0, j)),
        ],
        out_specs=pl.BlockSpec((b, TN_ADA), lambda j: (0, j)),
        out_shape=jax.ShapeDtypeStruct((b, n), F32),
        compiler_params=pltpu.CompilerParams(
            dimension_semantics=("arbitrary",), vmem_limit_bytes=VMEM_LIMIT),
        name="adaln_mod",
    )(c, w_ada, b_ada.reshape(1, n))


def _ffn_kernel(x_ref, xn_ref, mod_ref, modn_ref, g_ref, wg_ref, wu_ref, wd_ref, gfin_ref, *rest,
                sub, n_chunks, final_norm, n_casts):
    cast_in, rest = rest[:n_casts], rest[n_casts:]
    o_ref, cast_out = rest[0], rest[1:1 + n_casts]
    h_ref, hn_ref, acc_ref = rest[1 + n_casts:]
    i = pl.program_id(0)
    j = pl.program_id(1)
    tm = x_ref.shape[0]

    @pl.when((j == 0) & (i == 0))
    def _():
        h_ref[...] = _modulated_norm(x_ref[...], g_ref, mod_ref, sub)

    @pl.when((j == 0) & (i > 0))
    def _():
        h_ref[...] = hn_ref[...]

    @pl.when(j == 0)
    def _():
        acc_ref[...] = jnp.zeros_like(acc_ref)

    h = h_ref[...]
    gate = jnp.dot(h, wg_ref[...], preferred_element_type=F32)
    up = jnp.dot(h, wu_ref[...], preferred_element_type=F32)
    act = (_silu(gate) * up).astype(BF16)
    acc_ref[...] += jnp.dot(act, wd_ref[...], preferred_element_type=F32)

    slot = jnp.minimum(j, tm // NORM_ROWS_STEP - 1)
    rows = pl.ds(pl.multiple_of(slot * NORM_ROWS_STEP, NORM_ROWS_STEP), NORM_ROWS_STEP)
    hn_ref[rows, :] = _modulated_norm(xn_ref[...], g_ref, modn_ref, sub)

    for src, dst in zip(cast_in, cast_out):
        dst[...] = src[...].astype(BF16)

    @pl.when(j == n_chunks - 1)
    def _():
        gmod = mod_ref[0, pl.ds(3 * sub + 2, 1), :]
        out = x_ref[...] + (FFN_RES * gmod) * acc_ref[...]
        if final_norm:
            out = _rms(out, gfin_ref[...])
        o_ref[...] = out


def _ffn_grid(t, dff):
    return t // TM_FFN, dff // TF_FFN


def _cast_jobs(weights, n_tiles, n_chunks):
    jobs = []
    for w in weights:
        rows, cols = w.shape
        if rows % (n_tiles * n_chunks) == 0 and (rows // (n_tiles * n_chunks)) % BF16_ROWS == 0:
            jobs.append((w, (rows // (n_tiles * n_chunks), cols),
                         lambda i, j: (i * n_chunks + j, 0)))
        elif cols % (n_chunks * LANES) == 0:
            jobs.append((w, (rows // n_tiles, cols // n_chunks), lambda i, j: (i, j)))
        else:
            sub = rows // n_tiles // BF16_ROWS
            assert 0 < sub <= n_chunks and rows == n_tiles * sub * BF16_ROWS
            jobs.append((w, (BF16_ROWS, cols),
                         lambda i, j, sub=sub: (i * sub + jnp.minimum(j, sub - 1), 0)))
        assert rows % n_tiles == 0 and jobs[-1][1][0] % BF16_ROWS == 0
    return jobs


def _ffn_call(x, mod, g, w_gu, w_down, g_final, *, sub, seq, final_norm, cast_weights=()):
    t, d = x.shape
    dff = w_down.shape[0]
    tm = TM_FFN
    n_tiles, n_chunks = _ffn_grid(t, dff)
    tiles_per_batch = seq // tm
    slices = tm // NORM_ROWS_STEP
    assert slices <= n_chunks and tm % NORM_ROWS_STEP == 0
    nxt = lambda i: jnp.minimum(i + 1, n_tiles - 1)
    jobs = _cast_jobs(cast_weights, n_tiles, n_chunks)
    kern = functools.partial(_ffn_kernel, sub=sub, n_chunks=n_chunks, final_norm=final_norm,
                             n_casts=len(jobs))
    return pl.pallas_call(
        kern,
        grid=(n_tiles, n_chunks),
        in_specs=[
            pl.BlockSpec((tm, d), lambda i, j: (i, 0)),
            pl.BlockSpec((NORM_ROWS_STEP, d),
                         lambda i, j: (nxt(i) * slices + jnp.minimum(j, slices - 1), 0)),
            pl.BlockSpec((1, N_SUB * N_MOD, d), lambda i, j: (i // tiles_per_batch, 0, 0)),
            pl.BlockSpec((1, N_SUB * N_MOD, d), lambda i, j: (nxt(i) // tiles_per_batch, 0, 0)),
            pl.BlockSpec((1, d), lambda i, j: (0, 0)),
            pl.BlockSpec((d, TF_FFN), lambda i, j: (0, j)),
            pl.BlockSpec((d, TF_FFN), lambda i, j: (0, n_chunks + j)),
            pl.BlockSpec((TF_FFN, d), lambda i, j: (j, 0)),
            pl.BlockSpec((1, d), lambda i, j: (0, 0)),
        ] + [pl.BlockSpec(block, index) for _, block, index in jobs],
        out_specs=[pl.BlockSpec((tm, d), lambda i, j: (i, 0))]
        + [pl.BlockSpec(block, index) for _, block, index in jobs],
        out_shape=[jax.ShapeDtypeStruct((t, d), F32)]
        + [jax.ShapeDtypeStruct(w.shape, BF16) for w, _, _ in jobs],
        scratch_shapes=[pltpu.VMEM((tm, d), BF16), pltpu.VMEM((tm, d), BF16),
                        pltpu.VMEM((tm, d), F32)],
        compiler_params=pltpu.CompilerParams(
            dimension_semantics=("arbitrary", "arbitrary"), vmem_limit_bytes=VMEM_LIMIT),
        name="ffn_sub%d" % sub,
    )(x, x, mod, mod, g.reshape(1, d), w_gu, w_gu, w_down, g_final.reshape(1, d),
      *[w for w, _, _ in jobs])


def _rope_tables(pos_ref, invf_ref):
    lane = lax.broadcasted_iota(jnp.int32, (1, LANES), 1)
    d = lane % HEAD_DIM
    ang = pos_ref[...].astype(F32) * invf_ref[...]
    c = jnp.cos(ang)
    s = jnp.sin(ang)
    half = ROT_DIM // 2
    coef_self = jnp.where(d < ROT_DIM, c, 1.0)
    coef_next = jnp.where(d < half, -s, 0.0)
    coef_prev = jnp.where((d >= half) & (d < ROT_DIM), s, 0.0)
    return coef_self, coef_next, coef_prev


def _rope_slab(z, tables):
    coef_self, coef_next, coef_prev = tables
    half = ROT_DIM // 2
    nxt = pltpu.roll(z, LANES - half, axis=1)
    prv = pltpu.roll(z, half, axis=1)
    return z * coef_self + nxt * coef_next + prv * coef_prev


def _inproj_kernel(x_ref, mod_ref, g_ref, w_ref, b_ref, pos_ref, invf_ref,
                   qa_ref, ka_ref, va_ref, qb_ref, kb_ref, vb_ref, *, sub):
    h = _modulated_norm(x_ref[...], g_ref, mod_ref, sub)
    tables = _rope_tables(pos_ref, invf_ref)
    lane = lax.broadcasted_iota(jnp.int32, (1, LANES), 1)
    lo = lane < HEAD_DIM
    qscale = HEAD_DIM ** -0.5

    def proj_slabs(col, width):
        for c0 in range(0, width, MXU_COLS):
            z = (jnp.dot(h, w_ref[:, col + c0:col + c0 + MXU_COLS], preferred_element_type=F32)
                 + b_ref[:, col + c0:col + c0 + MXU_COLS])
            for half in range(MXU_COLS // LANES):
                yield c0 // LANES + half, z[:, half * LANES:(half + 1) * LANES]

    def dup_heads(slab):
        swapped = pltpu.roll(slab, HEAD_DIM, axis=1)
        return jnp.where(lo, slab, swapped), jnp.where(lo, swapped, slab)

    col = 0
    for k, z in proj_slabs(col, WIDTH_A):
        qa_ref[:, k * LANES:(k + 1) * LANES] = (_rope_slab(z, tables) * qscale).astype(BF16)
    col += WIDTH_A
    for k, z in proj_slabs(col, 2 * WIDTH_KV_A):
        dst = ka_ref if k == 0 else va_ref
        h0, h1 = dup_heads(_rope_slab(z, tables) if k == 0 else z)
        dst[:, 0:LANES] = h0.astype(BF16)
        dst[:, LANES:2 * LANES] = h1.astype(BF16)
    col += 2 * WIDTH_KV_A
    for k, z in proj_slabs(col, WIDTH_B):
        qb_ref[:, k * LANES:(k + 1) * LANES] = _rope_slab(z, tables) * qscale
    col += WIDTH_B
    for k, z in proj_slabs(col, WIDTH_B):
        kb_ref[:, k * LANES:(k + 1) * LANES] = _rope_slab(z, tables)
    col += WIDTH_B
    for k, z in proj_slabs(col, WIDTH_B):
        vb_ref[:, k * LANES:(k + 1) * LANES] = z


def _inproj_call(x, mod, g, w_in, b_in, pos, invf, *, sub, seq):
    t, d = x.shape
    tm = TM_PROJ
    tiles_per_batch = seq // tm
    row = lambda i: (i, 0)
    const = lambda i: (0, 0)
    return pl.pallas_call(
        functools.partial(_inproj_kernel, sub=sub),
        grid=(t // tm,),
        in_specs=[
            pl.BlockSpec((tm, d), row),
            pl.BlockSpec((1, N_SUB * N_MOD, d), lambda i: (i // tiles_per_batch, 0, 0)),
            pl.BlockSpec((1, d), const),
            pl.BlockSpec((d, IN_WIDTH), const, pipeline_mode=pl.Buffered(1)),
            pl.BlockSpec((1, IN_WIDTH), const),
            pl.BlockSpec((tm, 1), row),
            pl.BlockSpec((1, LANES), const),
        ],
        out_specs=[
            pl.BlockSpec((tm, WIDTH_A), row),
            pl.BlockSpec((tm, 2 * LANES), row),
            pl.BlockSpec((tm, 2 * LANES), row),
            pl.BlockSpec((tm, WIDTH_B), row),
            pl.BlockSpec((tm, WIDTH_B), row),
            pl.BlockSpec((tm, WIDTH_B), row),
        ],
        out_shape=[
            jax.ShapeDtypeStruct((t, WIDTH_A), BF16),
            jax.ShapeDtypeStruct((t, 2 * LANES), BF16),
            jax.ShapeDtypeStruct((t, 2 * LANES), BF16),
            jax.ShapeDtypeStruct((t, WIDTH_B), F32),
            jax.ShapeDtypeStruct((t, WIDTH_B), F32),
            jax.ShapeDtypeStruct((t, WIDTH_B), F32),
        ],
        compiler_params=pltpu.CompilerParams(
            dimension_semantics=("arbitrary",), vmem_limit_bytes=VMEM_LIMIT),
        name="inproj_rope",
    )(x, mod, g.reshape(1, d), w_in, b_in.reshape(1, IN_WIDTH), pos, invf)


def _attn_a_kernel(sink_ref, q_ref, kp_ref, kc_ref, vp_ref, vc_ref, o_ref):
    i = pl.program_id(1)
    lane = lax.broadcasted_iota(jnp.int32, (1, LANES), 1)
    lo = lane < HEAD_DIM
    qi = lax.broadcasted_iota(jnp.int32, (BLOCK, 2 * BLOCK), 0)
    kj = lax.broadcasted_iota(jnp.int32, (BLOCK, 2 * BLOCK), 1)
    dist = qi + BLOCK - kj
    band = (dist >= 0) & (dist <= WINDOW_A - 1)
    bias_inner = jnp.where(band, 0.0, NEG_BIG).astype(F32)
    bias_first = jnp.where(band & ((kj >= BLOCK) | (i > 0)), 0.0, NEG_BIG).astype(F32)
    zero = jnp.zeros((BLOCK, LANES), BF16)
    ones = jnp.ones((2 * BLOCK, LANES), BF16)
    pairs = GROUP_A // 2
    for u in range(QBLOCKS_A):
        r0 = u * BLOCK
        bias = bias_first if u == 0 else bias_inner
        for hk in range(N_KV_A):
            cols = slice(hk * LANES, (hk + 1) * LANES)
            k_prev = kp_ref[:, cols] if u == 0 else kc_ref[r0 - BLOCK:r0, cols]
            v_prev = vp_ref[:, cols] if u == 0 else vc_ref[r0 - BLOCK:r0, cols]
            kk = jnp.concatenate([k_prev, kc_ref[r0:r0 + BLOCK, cols]], axis=0)
            vv = jnp.concatenate([v_prev, vc_ref[r0:r0 + BLOCK, cols]], axis=0)
            rows = []
            for jp in range(pairs):
                c0 = (hk * pairs + jp) * LANES
                qp = q_ref[r0:r0 + BLOCK, c0:c0 + LANES]
                rows.append(jnp.where(lo, qp, zero))
                rows.append(jnp.where(lo, zero, qp))
            lhs = jnp.concatenate(rows, axis=0)
            s_all = lax.dot_general(lhs, kk, (((1,), (1,)), ((), ())),
                                    preferred_element_type=F32)
            ps, ms = [], []
            for g in range(GROUP_A):
                sink = sink_ref[hk * GROUP_A + g]
                s = s_all[g * BLOCK:(g + 1) * BLOCK, :] + bias
                m = jnp.maximum(jnp.max(s, axis=-1, keepdims=True), sink)
                ps.append(jnp.exp(s - m).astype(BF16))
                ms.append(m)
            v3 = jnp.concatenate([vv, ones], axis=1)
            pv = jnp.dot(jnp.concatenate(ps, axis=0), v3, preferred_element_type=F32)
            outs = []
            for g in range(GROUP_A):
                sink = sink_ref[hk * GROUP_A + g]
                blk = pv[g * BLOCK:(g + 1) * BLOCK, :]
                l = blk[:, LANES:] + jnp.exp(sink - ms[g])
                outs.append(blk[:, :LANES] * (1.0 / l))
            for jp in range(pairs):
                c0 = (hk * pairs + jp) * LANES
                o_ref[r0:r0 + BLOCK, c0:c0 + LANES] = jnp.where(lo, outs[2 * jp], outs[2 * jp + 1])


def _attn_a_call(sinks, qa, ka, va, *, batch, seq):
    nb = seq // BLOCK
    rows = QBLOCKS_A * BLOCK
    steps = nb // QBLOCKS_A
    cur = lambda b, i: (b * steps + i, 0)
    prev = lambda b, i: (b * nb + jnp.maximum(QBLOCKS_A * i - 1, 0), 0)
    return pl.pallas_call(
        _attn_a_kernel,
        grid=(batch, steps),
        in_specs=[
            pl.BlockSpec(memory_space=pltpu.SMEM),
            pl.BlockSpec((rows, WIDTH_A), cur),
            pl.BlockSpec((BLOCK, 2 * LANES), prev),
            pl.BlockSpec((rows, 2 * LANES), cur),
            pl.BlockSpec((BLOCK, 2 * LANES), prev),
            pl.BlockSpec((rows, 2 * LANES), cur),
        ],
        out_specs=pl.BlockSpec((rows, WIDTH_A), cur),
        out_shape=jax.ShapeDtypeStruct((batch * seq, WIDTH_A), F32),
        compiler_params=pltpu.CompilerParams(
            dimension_semantics=("arbitrary", "arbitrary"), vmem_limit_bytes=VMEM_LIMIT),
        name="attn_swa_gqa",
    )(sinks, qa, ka, ka, va, va)


def _attn_b_kernel(q_ref, k_ref, v_ref, o_ref, x4_ref, m_sc, l_sc, acc_sc, *, seq):
    (_, d1), (_, d2), (_, d3) = DILATED
    assert d1 == 1 and d3 == d2 * d2 and all(w // dil == BLOCK for w, dil in DILATED)
    n_iter = seq // BLOCK
    cls = seq // d2
    nblk2 = cls // BLOCK
    assert seq // d3 == BLOCK and n_iter == d2 * nblk2 == d3

    lane = lax.broadcasted_iota(jnp.int32, (1, LANES), 1)
    lo = lane < HEAD_DIM
    qi = lax.broadcasted_iota(jnp.int32, (BLOCK, BLOCK), 0)
    kj = lax.broadcasted_iota(jnp.int32, (BLOCK, BLOCK), 1)
    bias_cur = jnp.where(kj <= qi, 0.0, NEG_BIG).astype(F32)
    bias_prev = jnp.where(kj >= qi, 0.0, NEG_BIG).astype(F32)
    zero = jnp.zeros((BLOCK, LANES), BF16)
    ones = jnp.ones((2 * BLOCK, LANES), BF16)

    for a, ref in enumerate((q_ref, k_ref, v_ref)):
        for r in range(d2):
            x4_ref[a, r * cls:(r + 1) * cls, :] = ref[0, pl.ds(r, cls, stride=d2), :]

    def block_stats(q, kcat, vcat, bias):
        nk = kcat.shape[0]
        qb = q.astype(BF16)
        lhs = jnp.concatenate([jnp.where(lo, qb, zero), jnp.where(lo, zero, qb)], axis=0)
        s = lax.dot_general(lhs, kcat.astype(BF16), (((1,), (1,)), ((), ())),
                            preferred_element_type=F32)
        s0 = s[:BLOCK, :] + bias
        s1 = s[BLOCK:, :] + bias
        m0 = jnp.max(s0, axis=-1, keepdims=True)
        m1 = jnp.max(s1, axis=-1, keepdims=True)
        p = jnp.concatenate([jnp.exp(s0 - m0), jnp.exp(s1 - m1)], axis=0).astype(BF16)
        v3 = jnp.concatenate([vcat.astype(BF16), ones[:nk, :]], axis=1)
        pv = jnp.dot(p, v3, preferred_element_type=F32)
        acc = jnp.where(lo, pv[:BLOCK, :LANES], pv[BLOCK:, :LANES])
        l = jnp.where(lo, pv[:BLOCK, LANES:], pv[BLOCK:, LANES:])
        m = jnp.where(lo, m0, m1)
        return m, l, acc

    def banded(cfg, qkv, cur, prev, has_prev):
        off = jnp.where(has_prev, 0.0, NEG_BIG).astype(F32)
        bias = jnp.concatenate([bias_prev + off, bias_cur], axis=1)
        stats = block_stats(qkv(0, cur),
                            jnp.concatenate([qkv(1, prev), qkv(1, cur)], axis=0),
                            jnp.concatenate([qkv(2, prev), qkv(2, cur)], axis=0), bias)
        for ref, val in zip((m_sc, l_sc, acc_sc), stats):
            ref[cfg, cur, :] = val

    natural = lambda a, rows: (q_ref, k_ref, v_ref)[a][0, rows, :]
    by_class = lambda a, rows: x4_ref[a, rows, :]

    def body(idx, carry):
        start = idx * BLOCK
        banded(0, natural, pl.ds(pl.multiple_of(start, BLOCK), BLOCK),
               pl.ds(pl.multiple_of(jnp.maximum(start - BLOCK, 0), BLOCK), BLOCK), idx > 0)
        n = idx % nblk2
        banded(1, by_class, pl.ds(pl.multiple_of(start, BLOCK), BLOCK),
               pl.ds(pl.multiple_of(jnp.maximum(start - BLOCK, start - n * BLOCK), BLOCK), BLOCK),
               n > 0)
        rows3 = pl.ds((idx % d2) * cls + idx // d2, BLOCK, stride=d2)
        stats = block_stats(by_class(0, rows3), by_class(1, rows3), by_class(2, rows3), bias_cur)
        for ref, val in zip((m_sc, l_sc, acc_sc), stats):
            ref[2, rows3, :] = val
        return carry

    lax.fori_loop(0, n_iter, body, 0, unroll=UNROLL_B)

    def merge(i, carry):
        start = i * BLOCK
        xr = pl.ds(pl.multiple_of(start, BLOCK), BLOCK)
        nat = pl.ds(i // nblk2 + (i % nblk2) * (BLOCK * d2), BLOCK, stride=d2)
        sel = (nat, xr, xr)
        ms = [m_sc[c, sel[c], :] for c in range(len(DILATED))]
        m_all = functools.reduce(jnp.maximum, ms)
        ws = [jnp.exp(m - m_all) for m in ms]
        l_all = sum(w * l_sc[c, sel[c], :] for c, w in enumerate(ws))
        acc_all = sum(w * acc_sc[c, sel[c], :] for c, w in enumerate(ws))
        o_ref[0, nat, :] = acc_all * (1.0 / l_all)
        return carry

    lax.fori_loop(0, n_iter, merge, 0, unroll=2)


def _attn_b_call(qb, kb, vb, *, batch, seq):
    spec = pl.BlockSpec((1, seq, LANES), lambda b, hp: (b, 0, hp))
    shape3 = (batch, seq, WIDTH_B)
    return pl.pallas_call(
        functools.partial(_attn_b_kernel, seq=seq),
        grid=(batch, WIDTH_B // LANES),
        in_specs=[spec, spec, spec],
        out_specs=spec,
        out_shape=jax.ShapeDtypeStruct(shape3, F32),
        scratch_shapes=[pltpu.VMEM((3, seq, LANES), F32)]
        + [pltpu.VMEM((len(DILATED), seq, LANES), F32)] * 3,
        compiler_params=pltpu.CompilerParams(
            dimension_semantics=("arbitrary", "arbitrary"), vmem_limit_bytes=VMEM_LIMIT),
        name="attn_dilated",
    )(qb.reshape(shape3), kb.reshape(shape3), vb.reshape(shape3))


def _outproj_kernel(oa_ref, ob_ref, x_ref, mod_ref, ga_ref, gb_ref, w_ref, b_ref, o_ref, *, sub):
    ya = _rms(oa_ref[...], ga_ref[...]).astype(BF16)
    yb = _rms(ob_ref[...], gb_ref[...]).astype(BF16)
    y = (jnp.dot(ya, w_ref[0:WIDTH_A, :], preferred_element_type=F32)
         + jnp.dot(yb, w_ref[WIDTH_A:WIDTH_A + WIDTH_B, :], preferred_element_type=F32)
         + b_ref[...])
    gmod = mod_ref[0, pl.ds(3 * sub + 2, 1), :]
    o_ref[...] = x_ref[...] + gmod * y


def _outproj_call(out_a, out_b, x, mod, g_a, g_b, w_out, b_out, *, sub, seq):
    t, d = x.shape
    tm = TM_PROJ
    tiles_per_batch = seq // tm
    row = lambda i: (i, 0)
    const = lambda i: (0, 0)
    return pl.pallas_call(
        functools.partial(_outproj_kernel, sub=sub),
        grid=(t // tm,),
        in_specs=[
            pl.BlockSpec((tm, WIDTH_A), row),
            pl.BlockSpec((tm, WIDTH_B), row),
            pl.BlockSpec((tm, d), row),
            pl.BlockSpec((1, N_SUB * N_MOD, d), lambda i: (i // tiles_per_batch, 0, 0)),
            pl.BlockSpec((1, WIDTH_A), const),
            pl.BlockSpec((1, WIDTH_B), const),
            pl.BlockSpec((WIDTH_A + WIDTH_B, d), const),
            pl.BlockSpec((1, d), const),
        ],
        out_specs=pl.BlockSpec((tm, d), row),
        out_shape=jax.ShapeDtypeStruct((t, d), F32),
        compiler_params=pltpu.CompilerParams(
            dimension_semantics=("arbitrary",), vmem_limit_bytes=VMEM_LIMIT),
        name="outproj_residual",
    )(out_a, out_b, x, mod, g_a.reshape(1, WIDTH_A), g_b.reshape(1, WIDTH_B), w_out,
      b_out.reshape(1, d))


def kernel(x, c, positions, w_ada, b_ada, g_ffn1, w_ffn1_in, w_ffn1_out, g_mix, w_in, b_in, sinks,
           g_out_a, g_out_b, w_out, b_out, g_ffn2, w_ffn2_in, w_ffn2_out, g_final):
    batch, seq, d = x.shape
    depth = w_ada.shape[0]
    t = batch * seq
    xt = x.reshape(t, d)
    pos = positions.reshape(t, 1)
    lane_dim = jnp.arange(LANES, dtype=jnp.int32) % (ROT_DIM // 2)
    invf = (ROPE_THETA ** (-(2.0 * lane_dim.astype(F32)) / ROT_DIM)).reshape(1, LANES)
    for layer in range(depth):
        mod = _ada_call(c, w_ada[layer], b_ada[layer]).reshape(batch, N_SUB * N_MOD, d)
        last = layer == depth - 1
        xt, w_in_bf, w_out_bf, w_gu2_bf, w_down2_bf = _ffn_call(
            xt, mod, g_ffn1[layer], w_ffn1_in[layer].astype(BF16), w_ffn1_out[layer].astype(BF16),
            g_final, sub=0, seq=seq, final_norm=False,
            cast_weights=(w_in[layer], w_out[layer], w_ffn2_in[layer], w_ffn2_out[layer]))
        qa, ka, va, qb, kb, vb = _inproj_call(xt, mod, g_mix[layer], w_in_bf, b_in[layer], pos, invf,
                                              sub=1, seq=seq)
        out_a = _attn_a_call(sinks[layer], qa, ka, va, batch=batch, seq=seq)
        out_b = _attn_b_call(qb, kb, vb, batch=batch, seq=seq).reshape(t, WIDTH_B)
        xt = _outproj_call(out_a, out_b, xt, mod, g_out_a[layer], g_out_b[layer], w_out_bf,
                           b_out[layer], sub=1, seq=seq)
        xt, = _ffn_call(xt, mod, g_ffn2[layer], w_gu2_bf, w_down2_bf, g_final, sub=2, seq=seq,
                        final_norm=last)
    if depth == 0:
        raise ValueError("depth must be >= 1")
    return xt.reshape(batch, seq, d)
```

```python
import functools

import jax
import jax.numpy as jnp
from jax import lax
from jax.experimental import pallas as pl
from jax.experimental.pallas import tpu as pltpu

F32 = jnp.float32
BF16 = jnp.bfloat16

D_MODEL = 2048
HEAD_DIM = 64
N_HEADS_A = 16
N_KV_A = 2
GROUP_A = N_HEADS_A // N_KV_A
N_HEADS_B = 16
WIDTH_A = N_HEADS_A * HEAD_DIM
WIDTH_KV_A = N_KV_A * HEAD_DIM
WIDTH_B = N_HEADS_B * HEAD_DIM
IN_WIDTH = WIDTH_A + 2 * WIDTH_KV_A + 3 * WIDTH_B
WINDOW_A = 128
DILATED = ((128, 1), (512, 4), (2048, 16))
BLOCK = 128
ROPE_THETA = 500000.0
ROT_DIM = HEAD_DIM // 4
D_FF = 5632
FFN_RES = 0.5
N_SUB = 3
N_MOD = 3
EPS = 1e-5

LANES = 128
MXU_COLS = 256
NEG_BIG = -1e30
VMEM_LIMIT = 56 * 1024 * 1024

TM_FFN = 512
TF_FFN = 512
TM_PROJ = 512
TN_ADA = 1024
WEIGHT_SLOTS = 3
BF16_ROWS = 16
NORM_ROWS_STEP = 64
QBLOCKS_A = 2
UNROLL_B = 4


def _silu(x):
    return x * (1.0 / (1.0 + jnp.exp(-x)))


def _rms(x, g):
    ms = jnp.mean(x * x, axis=-1, keepdims=True)
    return (x * lax.rsqrt(ms + EPS)) * g


def _modulated_norm(x, g_ref, mod_ref, sub):
    shift = mod_ref[0, pl.ds(3 * sub + 0, 1), :]
    scale = mod_ref[0, pl.ds(3 * sub + 1, 1), :]
    return (_rms(x, g_ref[...]) * (1.0 + scale) + shift).astype(BF16)


def _ada_kernel(c_ref, w_ref, b_ref, o_ref):
    cond = _silu(c_ref[...]).astype(BF16)
    o_ref[...] = jnp.dot(cond, w_ref[...].astype(BF16), preferred_element_type=F32) + b_ref[...]


def _ada_call(c, w_ada, b_ada):
    b, d = c.shape
    n = w_ada.shape[1]
    return pl.pallas_call(
        _ada_kernel,
        grid=(n // TN_ADA,),
        in_specs=[
            pl.BlockSpec((b, d), lambda j: (0, 0)),
            pl.BlockSpec((d, TN_ADA), lambda j: (0, j)),
            pl.BlockSpec((1, TN_ADA), lambda j: (0, j)),
        ],
        out_specs=pl.BlockSpec((b, TN_ADA), lambda j: (0, j)),
        out_shape=jax.ShapeDtypeStruct((b, n), F32),
        compiler_params=pltpu.CompilerParams(
            dimension_semantics=("arbitrary",), vmem_limit_bytes=VMEM_LIMIT),
        name="adaln_mod",
    )(c, w_ada, b_ada.reshape(1, n))


def _ffn_kernel(x_hbm, mod_ref, modn_ref, g_ref, wgu_hbm, wd_hbm, gfin_ref, *rest,
                sub, n_tiles, n_chunks, final_norm, n_casts):
    cast_in, rest = rest[:n_casts], rest[n_casts:]
    o_ref, cast_out = rest[0], rest[1:1 + n_casts]
    xbuf, h_ref, hn_ref, acc_ref, wg_buf, wu_buf, wd_buf, xsem, wsem = rest[1 + n_casts:]
    i = pl.program_id(0)
    tm = h_ref.shape[0]
    tf = wd_buf.shape[1]
    dff = n_chunks * tf
    cur = i % 2
    nxt = 1 - cur
    norm_start = n_chunks - tm // NORM_ROWS_STEP

    def aligned(start, size):
        return start if isinstance(start, int) else pl.multiple_of(start, size)

    def x_copy(tile, slot):
        return pltpu.make_async_copy(x_hbm.at[pl.ds(aligned(tile * tm, tm), tm), :],
                                     xbuf.at[slot], xsem.at[slot])

    def w_copies(chunk, slot):
        col = aligned(chunk * tf, tf)
        return (
            pltpu.make_async_copy(wgu_hbm.at[:, pl.ds(col, tf)], wg_buf.at[slot],
                                  wsem.at[slot]),
            pltpu.make_async_copy(wgu_hbm.at[:, pl.ds(aligned(dff + col, tf), tf)],
                                  wu_buf.at[slot], wsem.at[WEIGHT_SLOTS + slot]),
            pltpu.make_async_copy(wd_hbm.at[pl.ds(col, tf), :], wd_buf.at[slot],
                                  wsem.at[2 * WEIGHT_SLOTS + slot]),
        )

    @pl.when(i == 0)
    def _():
        x_copy(0, 0).start()
        for c in w_copies(0, 0):
            c.start()
        x_copy(0, 0).wait()
        h_ref[...] = _modulated_norm(xbuf[0], g_ref, mod_ref, sub)

    @pl.when(i > 0)
    def _():
        h_ref[...] = hn_ref[...]

    next_tile = jnp.minimum(i + 1, n_tiles - 1)
    x_copy(next_tile, nxt).start()
    acc_ref[...] = jnp.zeros_like(acc_ref)
    for src, dst in zip(cast_in, cast_out):
        dst[...] = src[...].astype(BF16)

    def chunk(j, with_norm):
        slot = j % WEIGHT_SLOTS
        for c in w_copies(j, slot):
            c.wait()
        ahead = jnp.where(j + 1 == n_chunks, 0, j + 1)
        for c in w_copies(ahead, ahead % WEIGHT_SLOTS):
            c.start()
        h = h_ref[...]
        gate = jnp.dot(h, wg_buf[slot], preferred_element_type=F32)
        up = jnp.dot(h, wu_buf[slot], preferred_element_type=F32)
        act = (_silu(gate) * up).astype(BF16)
        acc_ref[...] += jnp.dot(act, wd_buf[slot], preferred_element_type=F32)
        if with_norm:
            rows = pl.ds(pl.multiple_of((j - norm_start) * NORM_ROWS_STEP, NORM_ROWS_STEP),
                         NORM_ROWS_STEP)
            hn_ref[rows, :] = _modulated_norm(xbuf[nxt, rows, :], g_ref, modn_ref, sub)

    def loop(lo, hi, with_norm):
        def body(j, carry):
            chunk(j, with_norm)
            return carry
        lax.fori_loop(lo, hi, body, 0)

    loop(0, norm_start, False)
    x_copy(next_tile, nxt).wait()
    loop(norm_start, n_chunks, True)

    @pl.when(i == n_tiles - 1)
    def _():
        for c in w_copies(0, 0):
            c.wait()

    gmod = mod_ref[0, pl.ds(3 * sub + 2, 1), :]
    out = xbuf[cur] + (FFN_RES * gmod) * acc_ref[...]
    if final_norm:
        out = _rms(out, gfin_ref[...])
    o_ref[...] = out


def _cast_specs(weights, n_steps):
    specs = []
    for w in weights:
        rows, cols = w.shape
        assert rows % (n_steps * BF16_ROWS) == 0
        specs.append(pl.BlockSpec((rows // n_steps, cols), lambda i: (i, 0)))
    return specs


def _cast_shapes(weights):
    return [jax.ShapeDtypeStruct(w.shape, BF16) for w in weights]


def _cast_blocks(cast_in, cast_out):
    for src, dst in zip(cast_in, cast_out):
        dst[...] = src[...].astype(BF16)


def _ffn_call(x, mod, g, w_gu, w_down, g_final, *, sub, seq, final_norm, cast_weights=()):
    t, d = x.shape
    dff = w_down.shape[0]
    tm, tf = TM_FFN, TF_FFN
    n_tiles, n_chunks = t // tm, dff // tf
    tiles_per_batch = seq // tm
    assert tm // NORM_ROWS_STEP <= n_chunks and tm % NORM_ROWS_STEP == 0
    assert n_chunks % WEIGHT_SLOTS != 1
    nxt = lambda i: jnp.minimum(i + 1, n_tiles - 1)
    const = lambda i: (0, 0)
    kern = functools.partial(_ffn_kernel, sub=sub, n_tiles=n_tiles, n_chunks=n_chunks,
                             final_norm=final_norm, n_casts=len(cast_weights))
    return pl.pallas_call(
        kern,
        grid=(n_tiles,),
        in_specs=[
            pl.BlockSpec(memory_space=pl.ANY),
            pl.BlockSpec((1, N_SUB * N_MOD, d), lambda i: (i // tiles_per_batch, 0, 0)),
            pl.BlockSpec((1, N_SUB * N_MOD, d), lambda i: (nxt(i) // tiles_per_batch, 0, 0)),
            pl.BlockSpec((1, d), const),
            pl.BlockSpec(memory_space=pl.ANY),
            pl.BlockSpec(memory_space=pl.ANY),
            pl.BlockSpec((1, d), const),
        ] + _cast_specs(cast_weights, n_tiles),
        out_specs=[pl.BlockSpec((tm, d), lambda i: (i, 0))] + _cast_specs(cast_weights, n_tiles),
        out_shape=[jax.ShapeDtypeStruct((t, d), F32)] + _cast_shapes(cast_weights),
        scratch_shapes=[
            pltpu.VMEM((2, tm, d), F32),
            pltpu.VMEM((tm, d), BF16),
            pltpu.VMEM((tm, d), BF16),
            pltpu.VMEM((tm, d), F32),
            pltpu.VMEM((WEIGHT_SLOTS, d, tf), BF16),
            pltpu.VMEM((WEIGHT_SLOTS, d, tf), BF16),
            pltpu.VMEM((WEIGHT_SLOTS, tf, d), BF16),
            pltpu.SemaphoreType.DMA((2,)),
            pltpu.SemaphoreType.DMA((3 * WEIGHT_SLOTS,)),
        ],
        compiler_params=pltpu.CompilerParams(
            dimension_semantics=("arbitrary",), vmem_limit_bytes=VMEM_LIMIT),
        name="ffn_sub%d" % sub,
    )(x, mod, mod, g.reshape(1, d), w_gu, w_down, g_final.reshape(1, d), *cast_weights)


def _rope_tables(pos_ref, invf_ref):
    lane = lax.broadcasted_iota(jnp.int32, (1, LANES), 1)
    d = lane % HEAD_DIM
    ang = pos_ref[...].astype(F32) * invf_ref[...]
    c = jnp.cos(ang)
    s = jnp.sin(ang)
    half = ROT_DIM // 2
    coef_self = jnp.where(d < ROT_DIM, c, 1.0)
    coef_next = jnp.where(d < half, -s, 0.0)
    coef_prev = jnp.where((d >= half) & (d < ROT_DIM), s, 0.0)
    return coef_self, coef_next, coef_prev


def _rope_slab(z, tables):
    coef_self, coef_next, coef_prev = tables
    half = ROT_DIM // 2
    nxt = pltpu.roll(z, LANES - half, axis=1)
    prv = pltpu.roll(z, half, axis=1)
    return z * coef_self + nxt * coef_next + prv * coef_prev


def _inproj_kernel(x_ref, mod_ref, g_ref, w_ref, b_ref, pos_ref, invf_ref, *rest, sub, n_casts):
    cast_in, rest = rest[:n_casts], rest[n_casts:]
    qa_ref, ka_ref, va_ref, qb_ref, kb_ref, vb_ref = rest[:6]
    _cast_blocks(cast_in, rest[6:])
    h = _modulated_norm(x_ref[...], g_ref, mod_ref, sub)
    tables = _rope_tables(pos_ref, invf_ref)
    lane = lax.broadcasted_iota(jnp.int32, (1, LANES), 1)
    lo = lane < HEAD_DIM
    qscale = HEAD_DIM ** -0.5

    def proj_slabs(col, width):
        for c0 in range(0, width, MXU_COLS):
            z = (jnp.dot(h, w_ref[:, col + c0:col + c0 + MXU_COLS], preferred_element_type=F32)
                 + b_ref[:, col + c0:col + c0 + MXU_COLS])
            for half in range(MXU_COLS // LANES):
                yield c0 // LANES + half, z[:, half * LANES:(half + 1) * LANES]

    def dup_heads(slab):
        swapped = pltpu.roll(slab, HEAD_DIM, axis=1)
        return jnp.where(lo, slab, swapped), jnp.where(lo, swapped, slab)

    col = 0
    for k, z in proj_slabs(col, WIDTH_A):
        qa_ref[:, k * LANES:(k + 1) * LANES] = (_rope_slab(z, tables) * qscale).astype(BF16)
    col += WIDTH_A
    for k, z in proj_slabs(col, 2 * WIDTH_KV_A):
        dst = ka_ref if k == 0 else va_ref
        h0, h1 = dup_heads(_rope_slab(z, tables) if k == 0 else z)
        dst[:, 0:LANES] = h0.astype(BF16)
        dst[:, LANES:2 * LANES] = h1.astype(BF16)
    col += 2 * WIDTH_KV_A
    for k, z in proj_slabs(col, WIDTH_B):
        qb_ref[:, k * LANES:(k + 1) * LANES] = _rope_slab(z, tables) * qscale
    col += WIDTH_B
    for k, z in proj_slabs(col, WIDTH_B):
        kb_ref[:, k * LANES:(k + 1) * LANES] = _rope_slab(z, tables)
    col += WIDTH_B
    for k, z in proj_slabs(col, WIDTH_B):
        vb_ref[:, k * LANES:(k + 1) * LANES] = z


def _inproj_call(x, mod, g, w_in, b_in, pos, invf, *, sub, seq, cast_weights=()):
    t, d = x.shape
    tm = TM_PROJ
    tiles_per_batch = seq // tm
    row = lambda i: (i, 0)
    const = lambda i: (0, 0)
    return pl.pallas_call(
        functools.partial(_inproj_kernel, sub=sub, n_casts=len(cast_weights)),
        grid=(t // tm,),
        in_specs=[
            pl.BlockSpec((tm, d), row),
            pl.BlockSpec((1, N_SUB * N_MOD, d), lambda i: (i // tiles_per_batch, 0, 0)),
            pl.BlockSpec((1, d), const),
            pl.BlockSpec((d, IN_WIDTH), const, pipeline_mode=pl.Buffered(1)),
            pl.BlockSpec((1, IN_WIDTH), const),
            pl.BlockSpec((tm, 1), row),
            pl.BlockSpec((1, LANES), const),
        ] + _cast_specs(cast_weights, t // tm),
        out_specs=[
            pl.BlockSpec((tm, WIDTH_A), row),
            pl.BlockSpec((tm, 2 * LANES), row),
            pl.BlockSpec((tm, 2 * LANES), row),
            pl.BlockSpec((tm, WIDTH_B), row),
            pl.BlockSpec((tm, WIDTH_B), row),
            pl.BlockSpec((tm, WIDTH_B), row),
        ] + _cast_specs(cast_weights, t // tm),
        out_shape=[
            jax.ShapeDtypeStruct((t, WIDTH_A), BF16),
            jax.ShapeDtypeStruct((t, 2 * LANES), BF16),
            jax.ShapeDtypeStruct((t, 2 * LANES), BF16),
            jax.ShapeDtypeStruct((t, WIDTH_B), F32),
            jax.ShapeDtypeStruct((t, WIDTH_B), F32),
            jax.ShapeDtypeStruct((t, WIDTH_B), F32),
        ] + _cast_shapes(cast_weights),
        compiler_params=pltpu.CompilerParams(
            dimension_semantics=("arbitrary",), vmem_limit_bytes=VMEM_LIMIT),
        name="inproj_rope",
    )(x, mod, g.reshape(1, d), w_in, b_in.reshape(1, IN_WIDTH), pos, invf, *cast_weights)


def _attn_a_kernel(sink_ref, q_ref, kp_ref, kc_ref, vp_ref, vc_ref, o_ref):
    i = pl.program_id(1)
    lane = lax.broadcasted_iota(jnp.int32, (1, LANES), 1)
    lo = lane < HEAD_DIM
    qi = lax.broadcasted_iota(jnp.int32, (BLOCK, 2 * BLOCK), 0)
    kj = lax.broadcasted_iota(jnp.int32, (BLOCK, 2 * BLOCK), 1)
    dist = qi + BLOCK - kj
    band = (dist >= 0) & (dist <= WINDOW_A - 1)
    bias_inner = jnp.where(band, 0.0, NEG_BIG).astype(F32)
    bias_first = jnp.where(band & ((kj >= BLOCK) | (i > 0)), 0.0, NEG_BIG).astype(F32)
    zero = jnp.zeros((BLOCK, LANES), BF16)
    ones = jnp.ones((2 * BLOCK, LANES), BF16)
    pairs = GROUP_A // 2
    for u in range(QBLOCKS_A):
        r0 = u * BLOCK
        bias = bias_first if u == 0 else bias_inner
        for hk in range(N_KV_A):
            cols = slice(hk * LANES, (hk + 1) * LANES)
            k_prev = kp_ref[:, cols] if u == 0 else kc_ref[r0 - BLOCK:r0, cols]
            v_prev = vp_ref[:, cols] if u == 0 else vc_ref[r0 - BLOCK:r0, cols]
            kk = jnp.concatenate([k_prev, kc_ref[r0:r0 + BLOCK, cols]], axis=0)
            vv = jnp.concatenate([v_prev, vc_ref[r0:r0 + BLOCK, cols]], axis=0)
            rows = []
            for jp in range(pairs):
                c0 = (hk * pairs + jp) * LANES
                qp = q_ref[r0:r0 + BLOCK, c0:c0 + LANES]
                rows.append(jnp.where(lo, qp, zero))
                rows.append(jnp.where(lo, zero, qp))
            lhs = jnp.concatenate(rows, axis=0)
            s_all = lax.dot_general(lhs, kk, (((1,), (1,)), ((), ())),
                                    preferred_element_type=F32)
            ps, ms = [], []
            for g in range(GROUP_A):
                sink = sink_ref[hk * GROUP_A + g]
                s = s_all[g * BLOCK:(g + 1) * BLOCK, :] + bias
                m = jnp.maximum(jnp.max(s, axis=-1, keepdims=True), sink)
                ps.append(jnp.exp(s - m).astype(BF16))
                ms.append(m)
            v3 = jnp.concatenate([vv, ones], axis=1)
            pv = jnp.dot(jnp.concatenate(ps, axis=0), v3, preferred_element_type=F32)
            outs = []
            for g in range(GROUP_A):
                sink = sink_ref[hk * GROUP_A + g]
                blk = pv[g * BLOCK:(g + 1) * BLOCK, :]
                l = blk[:, LANES:] + jnp.exp(sink - ms[g])
                outs.append(blk[:, :LANES] * (1.0 / l))
            for jp in range(pairs):
                c0 = (hk * pairs + jp) * LANES
                o_ref[r0:r0 + BLOCK, c0:c0 + LANES] = jnp.where(lo, outs[2 * jp], outs[2 * jp + 1])


def _attn_a_call(sinks, qa, ka, va, *, batch, seq):
    nb = seq // BLOCK
    rows = QBLOCKS_A * BLOCK
    steps = nb // QBLOCKS_A
    cur = lambda b, i: (b * steps + i, 0)
    prev = lambda b, i: (b * nb + jnp.maximum(QBLOCKS_A * i - 1, 0), 0)
    return pl.pallas_call(
        _attn_a_kernel,
        grid=(batch, steps),
        in_specs=[
            pl.BlockSpec(memory_space=pltpu.SMEM),
            pl.BlockSpec((rows, WIDTH_A), cur),
            pl.BlockSpec((BLOCK, 2 * LANES), prev),
            pl.BlockSpec((rows, 2 * LANES), cur),
            pl.BlockSpec((BLOCK, 2 * LANES), prev),
            pl.BlockSpec((rows, 2 * LANES), cur),
        ],
        out_specs=pl.BlockSpec((rows, WIDTH_A), cur),
        out_shape=jax.ShapeDtypeStruct((batch * seq, WIDTH_A), F32),
        compiler_params=pltpu.CompilerParams(
            dimension_semantics=("arbitrary", "arbitrary"), vmem_limit_bytes=VMEM_LIMIT),
        name="attn_swa_gqa",
    )(sinks, qa, ka, ka, va, va)


def _attn_b_kernel(q_ref, k_ref, v_ref, o_ref, x4_ref, m_sc, l_sc, acc_sc, *, seq):
    (_, d1), (_, d2), (_, d3) = DILATED
    assert d1 == 1 and d3 == d2 * d2 and all(w // dil == BLOCK for w, dil in DILATED)
    n_iter = seq // BLOCK
    cls = seq // d2
    nblk2 = cls // BLOCK
    assert seq // d3 == BLOCK and n_iter == d2 * nblk2 == d3

    lane = lax.broadcasted_iota(jnp.int32, (1, LANES), 1)
    lo = lane < HEAD_DIM
    qi = lax.broadcasted_iota(jnp.int32, (BLOCK, BLOCK), 0)
    kj = lax.broadcasted_iota(jnp.int32, (BLOCK, BLOCK), 1)
    bias_cur = jnp.where(kj <= qi, 0.0, NEG_BIG).astype(F32)
    bias_prev = jnp.where(kj >= qi, 0.0, NEG_BIG).astype(F32)
    zero = jnp.zeros((BLOCK, LANES), BF16)
    ones = jnp.ones((2 * BLOCK, LANES), BF16)

    for a, ref in enumerate((q_ref, k_ref, v_ref)):
        for r in range(d2):
            x4_ref[a, r * cls:(r + 1) * cls, :] = ref[0, pl.ds(r, cls, stride=d2), :]

    def block_stats(q, kcat, vcat, bias):
        nk = kcat.shape[0]
        qb = q.astype(BF16)
        lhs = jnp.concatenate([jnp.where(lo, qb, zero), jnp.where(lo, zero, qb)], axis=0)
        s = lax.dot_general(lhs, kcat.astype(BF16), (((1,), (1,)), ((), ())),
                            preferred_element_type=F32)
        s0 = s[:BLOCK, :] + bias
        s1 = s[BLOCK:, :] + bias
        m0 = jnp.max(s0, axis=-1, keepdims=True)
        m1 = jnp.max(s1, axis=-1, keepdims=True)
        p = jnp.concatenate([jnp.exp(s0 - m0), jnp.exp(s1 - m1)], axis=0).astype(BF16)
        v3 = jnp.concatenate([vcat.astype(BF16), ones[:nk, :]], axis=1)
        pv = jnp.dot(p, v3, preferred_element_type=F32)
        acc = jnp.where(lo, pv[:BLOCK, :LANES], pv[BLOCK:, :LANES])
        l = jnp.where(lo, pv[:BLOCK, LANES:], pv[BLOCK:, LANES:])
        m = jnp.where(lo, m0, m1)
        return m, l, acc

    def banded(cfg, qkv, cur, prev, has_prev):
        off = jnp.where(has_prev, 0.0, NEG_BIG).astype(F32)
        bias = jnp.concatenate([bias_prev + off, bias_cur], axis=1)
        stats = block_stats(qkv(0, cur),
                            jnp.concatenate([qkv(1, prev), qkv(1, cur)], axis=0),
                            jnp.concatenate([qkv(2, prev), qkv(2, cur)], axis=0), bias)
        for ref, val in zip((m_sc, l_sc, acc_sc), stats):
            ref[cfg, cur, :] = val

    natural = lambda a, rows: (q_ref, k_ref, v_ref)[a][0, rows, :]
    by_class = lambda a, rows: x4_ref[a, rows, :]

    def body(idx, carry):
        start = idx * BLOCK
        banded(0, natural, pl.ds(pl.multiple_of(start, BLOCK), BLOCK),
               pl.ds(pl.multiple_of(jnp.maximum(start - BLOCK, 0), BLOCK), BLOCK), idx > 0)
        n = idx % nblk2
        banded(1, by_class, pl.ds(pl.multiple_of(start, BLOCK), BLOCK),
               pl.ds(pl.multiple_of(jnp.maximum(start - BLOCK, start - n * BLOCK), BLOCK), BLOCK),
               n > 0)
        rows3 = pl.ds((idx % d2) * cls + idx // d2, BLOCK, stride=d2)
        stats = block_stats(by_class(0, rows3), by_class(1, rows3), by_class(2, rows3), bias_cur)
        for ref, val in zip((m_sc, l_sc, acc_sc), stats):
            ref[2, rows3, :] = val
        return carry

    lax.fori_loop(0, n_iter, body, 0, unroll=UNROLL_B)

    def merge(i, carry):
        start = i * BLOCK
        xr = pl.ds(pl.multiple_of(start, BLOCK), BLOCK)
        nat = pl.ds(i // nblk2 + (i % nblk2) * (BLOCK * d2), BLOCK, stride=d2)
        sel = (nat, xr, xr)
        ms = [m_sc[c, sel[c], :] for c in range(len(DILATED))]
        m_all = functools.reduce(jnp.maximum, ms)
        ws = [jnp.exp(m - m_all) for m in ms]
        l_all = sum(w * l_sc[c, sel[c], :] for c, w in enumerate(ws))
        acc_all = sum(w * acc_sc[c, sel[c], :] for c, w in enumerate(ws))
        o_ref[0, nat, :] = acc_all * (1.0 / l_all)
        return carry

    lax.fori_loop(0, n_iter, merge, 0, unroll=2)


def _attn_b_call(qb, kb, vb, *, batch, seq):
    spec = pl.BlockSpec((1, seq, LANES), lambda b, hp: (b, 0, hp))
    shape3 = (batch, seq, WIDTH_B)
    return pl.pallas_call(
        functools.partial(_attn_b_kernel, seq=seq),
        grid=(batch, WIDTH_B // LANES),
        in_specs=[spec, spec, spec],
        out_specs=spec,
        out_shape=jax.ShapeDtypeStruct(shape3, F32),
        scratch_shapes=[pltpu.VMEM((3, seq, LANES), F32)]
        + [pltpu.VMEM((len(DILATED), seq, LANES), F32)] * 3,
        compiler_params=pltpu.CompilerParams(
            dimension_semantics=("arbitrary", "arbitrary"), vmem_limit_bytes=VMEM_LIMIT),
        name="attn_dilated",
    )(qb.reshape(shape3), kb.reshape(shape3), vb.reshape(shape3))


def _outproj_kernel(oa_ref, ob_ref, x_ref, mod_ref, ga_ref, gb_ref, w_ref, b_ref, *rest,
                    sub, n_casts):
    o_ref = rest[n_casts]
    _cast_blocks(rest[:n_casts], rest[n_casts + 1:])
    ya = _rms(oa_ref[...], ga_ref[...]).astype(BF16)
    yb = _rms(ob_ref[...], gb_ref[...]).astype(BF16)
    y = (jnp.dot(ya, w_ref[0:WIDTH_A, :], preferred_element_type=F32)
         + jnp.dot(yb, w_ref[WIDTH_A:WIDTH_A + WIDTH_B, :], preferred_element_type=F32)
         + b_ref[...])
    gmod = mod_ref[0, pl.ds(3 * sub + 2, 1), :]
    o_ref[...] = x_ref[...] + gmod * y


def _outproj_call(out_a, out_b, x, mod, g_a, g_b, w_out, b_out, *, sub, seq, cast_weights=()):
    t, d = x.shape
    tm = TM_PROJ
    tiles_per_batch = seq // tm
    row = lambda i: (i, 0)
    const = lambda i: (0, 0)
    return pl.pallas_call(
        functools.partial(_outproj_kernel, sub=sub, n_casts=len(cast_weights)),
        grid=(t // tm,),
        in_specs=[
            pl.BlockSpec((tm, WIDTH_A), row),
            pl.BlockSpec((tm, WIDTH_B), row),
            pl.BlockSpec((tm, d), row),
            pl.BlockSpec((1, N_SUB * N_MOD, d), lambda i: (i // tiles_per_batch, 0, 0)),
            pl.BlockSpec((1, WIDTH_A), const),
            pl.BlockSpec((1, WIDTH_B), const),
            pl.BlockSpec((WIDTH_A + WIDTH_B, d), const),
            pl.BlockSpec((1, d), const),
        ] + _cast_specs(cast_weights, t // tm),
        out_specs=[pl.BlockSpec((tm, d), row)] + _cast_specs(cast_weights, t // tm),
        out_shape=[jax.ShapeDtypeStruct((t, d), F32)] + _cast_shapes(cast_weights),
        compiler_params=pltpu.CompilerParams(
            dimension_semantics=("arbitrary",), vmem_limit_bytes=VMEM_LIMIT),
        name="outproj_residual",
    )(out_a, out_b, x, mod, g_a.reshape(1, WIDTH_A), g_b.reshape(1, WIDTH_B), w_out,
      b_out.reshape(1, d), *cast_weights)


def kernel(x, c, positions, w_ada, b_ada, g_ffn1, w_ffn1_in, w_ffn1_out, g_mix, w_in, b_in, sinks,
           g_out_a, g_out_b, w_out, b_out, g_ffn2, w_ffn2_in, w_ffn2_out, g_final):
    batch, seq, d = x.shape
    depth = w_ada.shape[0]
    t = batch * seq
    xt = x.reshape(t, d)
    pos = positions.reshape(t, 1)
    lane_dim = jnp.arange(LANES, dtype=jnp.int32) % (ROT_DIM // 2)
    invf = (ROPE_THETA ** (-(2.0 * lane_dim.astype(F32)) / ROT_DIM)).reshape(1, LANES)
    for layer in range(depth):
        mod = _ada_call(c, w_ada[layer], b_ada[layer]).reshape(batch, N_SUB * N_MOD, d)
        last = layer == depth - 1
        xt, w_in_bf = _ffn_call(
            xt, mod, g_ffn1[layer], w_ffn1_in[layer].astype(BF16), w_ffn1_out[layer].astype(BF16),
            g_final, sub=0, seq=seq, final_norm=False, cast_weights=(w_in[layer],))
        qa, ka, va, qb, kb, vb, w_out_bf, w_down2_bf = _inproj_call(
            xt, mod, g_mix[layer], w_in_bf, b_in[layer], pos, invf, sub=1, seq=seq,
            cast_weights=(w_out[layer], w_ffn2_out[layer]))
        out_a = _attn_a_call(sinks[layer], qa, ka, va, batch=batch, seq=seq)
        out_b = _attn_b_call(qb, kb, vb, batch=batch, seq=seq).reshape(t, WIDTH_B)
        xt, w_gu2_bf = _outproj_call(out_a, out_b, xt, mod, g_out_a[layer], g_out_b[layer],
                                     w_out_bf, b_out[layer], sub=1, seq=seq,
                                     cast_weights=(w_ffn2_in[layer],))
        xt, = _ffn_call(xt, mod, g_ffn2[layer], w_gu2_bf, w_down2_bf, g_final, sub=2, seq=seq,
                        final_norm=last)
    if depth == 0:
        raise ValueError("depth must be >= 1")
    return xt.reshape(batch, seq, d)
```

```python
import functools

import jax
import jax.numpy as jnp
from jax import lax
from jax.experimental import pallas as pl
from jax.experimental.pallas import tpu as pltpu

F32 = jnp.float32
BF16 = jnp.bfloat16

D_MODEL = 2048
HEAD_DIM = 64
N_HEADS_A = 16
N_KV_A = 2
GROUP_A = N_HEADS_A // N_KV_A
N_HEADS_B = 16
WIDTH_A = N_HEADS_A * HEAD_DIM
WIDTH_KV_A = N_KV_A * HEAD_DIM
WIDTH_B = N_HEADS_B * HEAD_DIM
IN_WIDTH = WIDTH_A + 2 * WIDTH_KV_A + 3 * WIDTH_B
WINDOW_A = 128
DILATED = ((128, 1), (512, 4), (2048, 16))
BLOCK = 128
ROPE_THETA = 500000.0
ROT_DIM = HEAD_DIM // 4
D_FF = 5632
FFN_RES = 0.5
N_SUB = 3
N_MOD = 3
EPS = 1e-5

LANES = 128
MXU_COLS = 256
NEG_BIG = -1e30
VMEM_LIMIT = 56 * 1024 * 1024

TM_FFN = 512
TF_FFN = 512
TM_PROJ = 512
TN_ADA = 1024
WEIGHT_SLOTS = 3
BF16_ROWS = 16
NORM_ROWS_STEP = 64
QBLOCKS_A = 2
UNROLL_B = 7


def _silu(x):
    return x * (1.0 / (1.0 + jnp.exp(-x)))


def _rms(x, g):
    ms = jnp.mean(x * x, axis=-1, keepdims=True)
    return (x * lax.rsqrt(ms + EPS)) * g


def _modulated_norm(x, g_ref, mod_ref, sub):
    shift = mod_ref[0, pl.ds(3 * sub + 0, 1), :]
    scale = mod_ref[0, pl.ds(3 * sub + 1, 1), :]
    return (_rms(x, g_ref[...]) * (1.0 + scale) + shift).astype(BF16)


def _ada_kernel(c_ref, w_ref, b_ref, o_ref):
    cond = _silu(c_ref[...]).astype(BF16)
    o_ref[...] = jnp.dot(cond, w_ref[...].astype(BF16), preferred_element_type=F32) + b_ref[...]


def _ada_call(c, w_ada, b_ada):
    b, d = c.shape
    n = w_ada.shape[1]
    return pl.pallas_call(
        _ada_kernel,
        grid=(n // TN_ADA,),
        in_specs=[
            pl.BlockSpec((b, d), lambda j: (0, 0)),
            pl.BlockSpec((d, TN_ADA), lambda j: (0, j)),
            pl.BlockSpec((1, TN_ADA), lambda j: (0, j)),
        ],
        out_specs=pl.BlockSpec((b, TN_ADA), lambda j: (0, j)),
        out_shape=jax.ShapeDtypeStruct((b, n), F32),
        compiler_params=pltpu.CompilerParams(
            dimension_semantics=("arbitrary",), vmem_limit_bytes=VMEM_LIMIT),
        name="adaln_mod",
    )(c, w_ada, b_ada.reshape(1, n))


def _ffn_kernel(x_hbm, mod_ref, modn_ref, g_ref, wgu_hbm, wd_hbm, gfin_ref, *rest,
                sub, n_tiles, n_chunks, final_norm, n_casts):
    cast_in, rest = rest[:n_casts], rest[n_casts:]
    o_ref, cast_out = rest[0], rest[1:1 + n_casts]
    xbuf, h_ref, hn_ref, acc_ref, wg_buf, wu_buf, wd_buf, xsem, wsem = rest[1 + n_casts:]
    i = pl.program_id(0)
    tm = h_ref.shape[0]
    tf = wd_buf.shape[1]
    dff = n_chunks * tf
    cur = i % 2
    nxt = 1 - cur
    norm_start = n_chunks - tm // NORM_ROWS_STEP

    def aligned(start, size):
        return start if isinstance(start, int) else pl.multiple_of(start, size)

    def x_copy(tile, slot):
        return pltpu.make_async_copy(x_hbm.at[pl.ds(aligned(tile * tm, tm), tm), :],
                                     xbuf.at[slot], xsem.at[slot])

    def w_copies(chunk, slot):
        col = aligned(chunk * tf, tf)
        return (
            pltpu.make_async_copy(wgu_hbm.at[:, pl.ds(col, tf)], wg_buf.at[slot],
                                  wsem.at[slot]),
            pltpu.make_async_copy(wgu_hbm.at[:, pl.ds(aligned(dff + col, tf), tf)],
                                  wu_buf.at[slot], wsem.at[WEIGHT_SLOTS + slot]),
            pltpu.make_async_copy(wd_hbm.at[pl.ds(col, tf), :], wd_buf.at[slot],
                                  wsem.at[2 * WEIGHT_SLOTS + slot]),
        )

    @pl.when(i == 0)
    def _():
        x_copy(0, 0).start()
        for c in w_copies(0, 0):
            c.start()
        x_copy(0, 0).wait()
        h_ref[...] = _modulated_norm(xbuf[0], g_ref, mod_ref, sub)

    @pl.when(i > 0)
    def _():
        h_ref[...] = hn_ref[...]

    next_tile = jnp.minimum(i + 1, n_tiles - 1)
    x_copy(next_tile, nxt).start()
    acc_ref[...] = jnp.zeros_like(acc_ref)
    _cast_blocks(cast_in, cast_out)

    def chunk(j, with_norm):
        slot = j % WEIGHT_SLOTS
        for c in w_copies(j, slot):
            c.wait()
        ahead = jnp.where(j + 1 == n_chunks, 0, j + 1)
        for c in w_copies(ahead, ahead % WEIGHT_SLOTS):
            c.start()
        h = h_ref[...]
        gate = jnp.dot(h, wg_buf[slot], preferred_element_type=F32)
        up = jnp.dot(h, wu_buf[slot], preferred_element_type=F32)
        act = (_silu(gate) * up).astype(BF16)
        acc_ref[...] += jnp.dot(act, wd_buf[slot], preferred_element_type=F32)
        if with_norm:
            rows = pl.ds(pl.multiple_of((j - norm_start) * NORM_ROWS_STEP, NORM_ROWS_STEP),
                         NORM_ROWS_STEP)
            hn_ref[rows, :] = _modulated_norm(xbuf[nxt, rows, :], g_ref, modn_ref, sub)

    def loop(lo, hi, with_norm):
        def body(j, carry):
            chunk(j, with_norm)
            return carry
        lax.fori_loop(lo, hi, body, 0)

    loop(0, norm_start, False)
    x_copy(next_tile, nxt).wait()
    loop(norm_start, n_chunks, True)

    @pl.when(i == n_tiles - 1)
    def _():
        for c in w_copies(0, 0):
            c.wait()

    gmod = mod_ref[0, pl.ds(3 * sub + 2, 1), :]
    out = xbuf[cur] + (FFN_RES * gmod) * acc_ref[...]
    if final_norm:
        out = _rms(out, gfin_ref[...])
    o_ref[...] = out


def _cast_specs(weights, n_steps, step=lambda i: i):
    specs = []
    for w in weights:
        rows, cols = w.shape
        assert rows % (n_steps * BF16_ROWS) == 0
        specs.append(pl.BlockSpec((rows // n_steps, cols), lambda *ids: (step(*ids), 0)))
    return specs


def _cast_shapes(weights):
    return [jax.ShapeDtypeStruct(w.shape, BF16) for w in weights]


def _cast_blocks(cast_in, cast_out):
    for src, dst in zip(cast_in, cast_out):
        dst[...] = src[...].astype(BF16)


def _ffn_call(x, mod, g, w_gu, w_down, g_final, *, sub, seq, final_norm, cast_weights=()):
    t, d = x.shape
    dff = w_down.shape[0]
    tm, tf = TM_FFN, TF_FFN
    n_tiles, n_chunks = t // tm, dff // tf
    tiles_per_batch = seq // tm
    assert tm // NORM_ROWS_STEP <= n_chunks and tm % NORM_ROWS_STEP == 0
    assert n_chunks % WEIGHT_SLOTS != 1
    nxt = lambda i: jnp.minimum(i + 1, n_tiles - 1)
    const = lambda i: (0, 0)
    kern = functools.partial(_ffn_kernel, sub=sub, n_tiles=n_tiles, n_chunks=n_chunks,
                             final_norm=final_norm, n_casts=len(cast_weights))
    return pl.pallas_call(
        kern,
        grid=(n_tiles,),
        in_specs=[
            pl.BlockSpec(memory_space=pl.ANY),
            pl.BlockSpec((1, N_SUB * N_MOD, d), lambda i: (i // tiles_per_batch, 0, 0)),
            pl.BlockSpec((1, N_SUB * N_MOD, d), lambda i: (nxt(i) // tiles_per_batch, 0, 0)),
            pl.BlockSpec((1, d), const),
            pl.BlockSpec(memory_space=pl.ANY),
            pl.BlockSpec(memory_space=pl.ANY),
            pl.BlockSpec((1, d), const),
        ] + _cast_specs(cast_weights, n_tiles),
        out_specs=[pl.BlockSpec((tm, d), lambda i: (i, 0))] + _cast_specs(cast_weights, n_tiles),
        out_shape=[jax.ShapeDtypeStruct((t, d), F32)] + _cast_shapes(cast_weights),
        scratch_shapes=[
            pltpu.VMEM((2, tm, d), F32),
            pltpu.VMEM((tm, d), BF16),
            pltpu.VMEM((tm, d), BF16),
            pltpu.VMEM((tm, d), F32),
            pltpu.VMEM((WEIGHT_SLOTS, d, tf), BF16),
            pltpu.VMEM((WEIGHT_SLOTS, d, tf), BF16),
            pltpu.VMEM((WEIGHT_SLOTS, tf, d), BF16),
            pltpu.SemaphoreType.DMA((2,)),
            pltpu.SemaphoreType.DMA((3 * WEIGHT_SLOTS,)),
        ],
        compiler_params=pltpu.CompilerParams(
            dimension_semantics=("arbitrary",), vmem_limit_bytes=VMEM_LIMIT),
        name="ffn_sub%d" % sub,
    )(x, mod, mod, g.reshape(1, d), w_gu, w_down, g_final.reshape(1, d), *cast_weights)


def _rope_tables(pos_ref, invf_ref):
    lane = lax.broadcasted_iota(jnp.int32, (1, LANES), 1)
    d = lane % HEAD_DIM
    ang = pos_ref[...].astype(F32) * invf_ref[...]
    c = jnp.cos(ang)
    s = jnp.sin(ang)
    half = ROT_DIM // 2
    coef_self = jnp.where(d < ROT_DIM, c, 1.0)
    coef_next = jnp.where(d < half, -s, 0.0)
    coef_prev = jnp.where((d >= half) & (d < ROT_DIM), s, 0.0)
    return coef_self, coef_next, coef_prev


def _rope_slab(z, tables):
    coef_self, coef_next, coef_prev = tables
    half = ROT_DIM // 2
    nxt = pltpu.roll(z, LANES - half, axis=1)
    prv = pltpu.roll(z, half, axis=1)
    return z * coef_self + nxt * coef_next + prv * coef_prev


def _inproj_kernel(x_ref, mod_ref, g_ref, w_ref, b_ref, pos_ref, invf_ref, *rest, sub, n_casts):
    cast_in, rest = rest[:n_casts], rest[n_casts:]
    qa_ref, ka_ref, va_ref, qb_ref, kb_ref, vb_ref = rest[:6]
    _cast_blocks(cast_in, rest[6:])
    h = _modulated_norm(x_ref[...], g_ref, mod_ref, sub)
    tables = _rope_tables(pos_ref, invf_ref)
    lane = lax.broadcasted_iota(jnp.int32, (1, LANES), 1)
    lo = lane < HEAD_DIM
    qscale = HEAD_DIM ** -0.5

    def proj_slabs(col, width):
        for c0 in range(0, width, MXU_COLS):
            z = (jnp.dot(h, w_ref[:, col + c0:col + c0 + MXU_COLS], preferred_element_type=F32)
                 + b_ref[:, col + c0:col + c0 + MXU_COLS])
            for half in range(MXU_COLS // LANES):
                yield c0 // LANES + half, z[:, half * LANES:(half + 1) * LANES]

    def dup_heads(slab):
        swapped = pltpu.roll(slab, HEAD_DIM, axis=1)
        return jnp.where(lo, slab, swapped), jnp.where(lo, swapped, slab)

    col = 0
    for k, z in proj_slabs(col, WIDTH_A):
        qa_ref[:, k * LANES:(k + 1) * LANES] = (_rope_slab(z, tables) * qscale).astype(BF16)
    col += WIDTH_A
    for k, z in proj_slabs(col, 2 * WIDTH_KV_A):
        dst = ka_ref if k == 0 else va_ref
        h0, h1 = dup_heads(_rope_slab(z, tables) if k == 0 else z)
        dst[:, 0:LANES] = h0.astype(BF16)
        dst[:, LANES:2 * LANES] = h1.astype(BF16)
    col += 2 * WIDTH_KV_A
    for k, z in proj_slabs(col, WIDTH_B):
        qb_ref[:, k * LANES:(k + 1) * LANES] = _rope_slab(z, tables) * qscale
    col += WIDTH_B
    for k, z in proj_slabs(col, WIDTH_B):
        kb_ref[:, k * LANES:(k + 1) * LANES] = _rope_slab(z, tables)
    col += WIDTH_B
    for k, z in proj_slabs(col, WIDTH_B):
        vb_ref[:, k * LANES:(k + 1) * LANES] = z


def _inproj_call(x, mod, g, w_in, b_in, pos, invf, *, sub, seq, cast_weights=()):
    t, d = x.shape
    tm = TM_PROJ
    tiles_per_batch = seq // tm
    row = lambda i: (i, 0)
    const = lambda i: (0, 0)
    return pl.pallas_call(
        functools.partial(_inproj_kernel, sub=sub, n_casts=len(cast_weights)),
        grid=(t // tm,),
        in_specs=[
            pl.BlockSpec((tm, d), row),
            pl.BlockSpec((1, N_SUB * N_MOD, d), lambda i: (i // tiles_per_batch, 0, 0)),
            pl.BlockSpec((1, d), const),
            pl.BlockSpec((d, IN_WIDTH), const, pipeline_mode=pl.Buffered(1)),
            pl.BlockSpec((1, IN_WIDTH), const),
            pl.BlockSpec((tm, 1), row),
            pl.BlockSpec((1, LANES), const),
        ] + _cast_specs(cast_weights, t // tm),
        out_specs=[
            pl.BlockSpec((tm, WIDTH_A), row),
            pl.BlockSpec((tm, 2 * LANES), row),
            pl.BlockSpec((tm, 2 * LANES), row),
            pl.BlockSpec((tm, WIDTH_B), row),
            pl.BlockSpec((tm, WIDTH_B), row),
            pl.BlockSpec((tm, WIDTH_B), row),
        ] + _cast_specs(cast_weights, t // tm),
        out_shape=[
            jax.ShapeDtypeStruct((t, WIDTH_A), BF16),
            jax.ShapeDtypeStruct((t, 2 * LANES), BF16),
            jax.ShapeDtypeStruct((t, 2 * LANES), BF16),
            jax.ShapeDtypeStruct((t, WIDTH_B), F32),
            jax.ShapeDtypeStruct((t, WIDTH_B), F32),
            jax.ShapeDtypeStruct((t, WIDTH_B), F32),
        ] + _cast_shapes(cast_weights),
        compiler_params=pltpu.CompilerParams(
            dimension_semantics=("arbitrary",), vmem_limit_bytes=VMEM_LIMIT),
        name="inproj_rope",
    )(x, mod, g.reshape(1, d), w_in, b_in.reshape(1, IN_WIDTH), pos, invf, *cast_weights)


def _attn_a_kernel(sink_ref, q_ref, kp_ref, kc_ref, vp_ref, vc_ref, o_ref):
    i = pl.program_id(1)
    lane = lax.broadcasted_iota(jnp.int32, (1, LANES), 1)
    lo = lane < HEAD_DIM
    qi = lax.broadcasted_iota(jnp.int32, (BLOCK, 2 * BLOCK), 0)
    kj = lax.broadcasted_iota(jnp.int32, (BLOCK, 2 * BLOCK), 1)
    dist = qi + BLOCK - kj
    band = (dist >= 0) & (dist <= WINDOW_A - 1)
    bias_inner = jnp.where(band, 0.0, NEG_BIG).astype(F32)
    bias_first = jnp.where(band & ((kj >= BLOCK) | (i > 0)), 0.0, NEG_BIG).astype(F32)
    zero = jnp.zeros((BLOCK, LANES), BF16)
    ones = jnp.ones((2 * BLOCK, LANES), BF16)
    pairs = GROUP_A // 2
    for u in range(QBLOCKS_A):
        r0 = u * BLOCK
        bias = bias_first if u == 0 else bias_inner
        for hk in range(N_KV_A):
            cols = slice(hk * LANES, (hk + 1) * LANES)
            k_prev = kp_ref[:, cols] if u == 0 else kc_ref[r0 - BLOCK:r0, cols]
            v_prev = vp_ref[:, cols] if u == 0 else vc_ref[r0 - BLOCK:r0, cols]
            kk = jnp.concatenate([k_prev, kc_ref[r0:r0 + BLOCK, cols]], axis=0)
            vv = jnp.concatenate([v_prev, vc_ref[r0:r0 + BLOCK, cols]], axis=0)
            rows = []
            for jp in range(pairs):
                c0 = (hk * pairs + jp) * LANES
                qp = q_ref[r0:r0 + BLOCK, c0:c0 + LANES]
                rows.append(jnp.where(lo, qp, zero))
                rows.append(jnp.where(lo, zero, qp))
            lhs = jnp.concatenate(rows, axis=0)
            s_all = lax.dot_general(lhs, kk, (((1,), (1,)), ((), ())),
                                    preferred_element_type=F32)
            ps, ms = [], []
            for g in range(GROUP_A):
                sink = sink_ref[hk * GROUP_A + g]
                s = s_all[g * BLOCK:(g + 1) * BLOCK, :] + bias
                m = jnp.maximum(jnp.max(s, axis=-1, keepdims=True), sink)
                ps.append(jnp.exp(s - m).astype(BF16))
                ms.append(m)
            v3 = jnp.concatenate([vv, ones], axis=1)
            pv = jnp.dot(jnp.concatenate(ps, axis=0), v3, preferred_element_type=F32)
            outs = []
            for g in range(GROUP_A):
                sink = sink_ref[hk * GROUP_A + g]
                blk = pv[g * BLOCK:(g + 1) * BLOCK, :]
                l = blk[:, LANES:] + jnp.exp(sink - ms[g])
                outs.append(blk[:, :LANES] * (1.0 / l))
            for jp in range(pairs):
                c0 = (hk * pairs + jp) * LANES
                o_ref[r0:r0 + BLOCK, c0:c0 + LANES] = jnp.where(lo, outs[2 * jp], outs[2 * jp + 1])


def _attn_a_call(sinks, qa, ka, va, *, batch, seq):
    nb = seq // BLOCK
    rows = QBLOCKS_A * BLOCK
    steps = nb // QBLOCKS_A
    cur = lambda b, i: (b * steps + i, 0)
    prev = lambda b, i: (b * nb + jnp.maximum(QBLOCKS_A * i - 1, 0), 0)
    return pl.pallas_call(
        _attn_a_kernel,
        grid=(batch, steps),
        in_specs=[
            pl.BlockSpec(memory_space=pltpu.SMEM),
            pl.BlockSpec((rows, WIDTH_A), cur),
            pl.BlockSpec((BLOCK, 2 * LANES), prev),
            pl.BlockSpec((rows, 2 * LANES), cur),
            pl.BlockSpec((BLOCK, 2 * LANES), prev),
            pl.BlockSpec((rows, 2 * LANES), cur),
        ],
        out_specs=pl.BlockSpec((rows, WIDTH_A), cur),
        out_shape=jax.ShapeDtypeStruct((batch * seq, WIDTH_A), F32),
        compiler_params=pltpu.CompilerParams(
            dimension_semantics=("arbitrary", "arbitrary"), vmem_limit_bytes=VMEM_LIMIT),
        name="attn_swa_gqa",
    )(sinks, qa, ka, ka, va, va)


def _attn_b_kernel(q_ref, k_ref, v_ref, *rest, seq, n_casts):
    cast_in, rest = rest[:n_casts], rest[n_casts:]
    o_ref, cast_out = rest[0], rest[1:1 + n_casts]
    x4_ref, t_sc, m_sc, l_sc, acc_sc = rest[1 + n_casts:]
    _cast_blocks(cast_in, cast_out)
    (_, d1), (_, d2), (_, d3) = DILATED
    assert d1 == 1 and d3 == d2 * d2 and all(w // dil == BLOCK for w, dil in DILATED)
    n_iter = seq // BLOCK
    cls = seq // d2
    nblk2 = cls // BLOCK
    assert seq // d3 == BLOCK and n_iter == d2 * nblk2 == d3

    lane = lax.broadcasted_iota(jnp.int32, (1, LANES), 1)
    lo = lane < HEAD_DIM
    qi = lax.broadcasted_iota(jnp.int32, (BLOCK, BLOCK), 0)
    kj = lax.broadcasted_iota(jnp.int32, (BLOCK, BLOCK), 1)
    bias_cur = jnp.where(kj <= qi, 0.0, NEG_BIG).astype(F32)
    bias_prev = jnp.where(kj >= qi, 0.0, NEG_BIG).astype(F32)
    zero = jnp.zeros((BLOCK, LANES), BF16)
    ones = jnp.ones((2 * BLOCK, LANES), BF16)

    for a, ref in enumerate((q_ref, k_ref, v_ref)):
        for r in range(d2):
            x4_ref[a, r * cls:(r + 1) * cls, :] = ref[0, pl.ds(r, cls, stride=d2), :]

    natural = lambda a, rows: (q_ref, k_ref, v_ref)[a][0, rows, :]
    by_class = lambda a, rows: x4_ref[a, rows, :]

    def block_rows(cfg, idx):
        static = isinstance(idx, int)
        block = lambda start: pl.ds(start if static else pl.multiple_of(start, BLOCK), BLOCK)
        biggest = max if static else jnp.maximum
        start = idx * BLOCK
        own = block(start)
        if cfg == 0:
            return natural, own, block(biggest(start - BLOCK, 0)), idx > 0
        if cfg == 1:
            n = idx % nblk2
            return by_class, own, block(biggest(start - BLOCK, start - n * BLOCK)), n > 0
        return by_class, pl.ds((idx % d2) * cls + idx // d2, BLOCK, stride=d2), None, None

    def scores(cfg, idx, slot):
        load, own, prev, has_prev = block_rows(cfg, idx)
        qb = load(0, own).astype(BF16)
        lhs = jnp.concatenate([jnp.where(lo, qb, zero), jnp.where(lo, zero, qb)], axis=0)
        if prev is None:
            keys, bias = load(1, own), bias_cur
        else:
            keys = jnp.concatenate([load(1, prev), load(1, own)], axis=0)
            off = jnp.where(has_prev, 0.0, NEG_BIG).astype(F32)
            bias = jnp.concatenate([bias_prev + off, bias_cur], axis=1)
        nk = keys.shape[0]
        s = lax.dot_general(lhs, keys.astype(BF16), (((1,), (1,)), ((), ())),
                            preferred_element_type=F32)
        s0 = s[:BLOCK, :] + bias
        s1 = s[BLOCK:, :] + bias
        m0 = jnp.max(s0, axis=-1, keepdims=True)
        m1 = jnp.max(s1, axis=-1, keepdims=True)
        t_sc[cfg, slot, :BLOCK, :nk] = s0 - m0
        t_sc[cfg, slot, BLOCK:, :nk] = s1 - m1
        m_sc[cfg, own, :] = jnp.where(lo, m0, m1)

    def values(cfg, idx, slot):
        load, own, prev, _ = block_rows(cfg, idx)
        vals = load(2, own) if prev is None else jnp.concatenate([load(2, prev), load(2, own)], axis=0)
        nk = vals.shape[0]
        p = jnp.exp(t_sc[cfg, slot, :, :nk]).astype(BF16)
        v3 = jnp.concatenate([vals.astype(BF16), ones[:nk, :]], axis=1)
        pv = jnp.dot(p, v3, preferred_element_type=F32)
        acc_sc[cfg, own, :] = jnp.where(lo, pv[:BLOCK, :LANES], pv[BLOCK:, :LANES])
        l_sc[cfg, own, :] = jnp.where(lo, pv[:BLOCK, LANES:], pv[BLOCK:, LANES:])

    n_cfg = len(DILATED)
    for cfg in range(n_cfg):
        scores(cfg, 0, 0)

    def body(k, carry):
        for half in range(2):
            b = 2 * k + half
            for cfg in range(n_cfg):
                values(cfg, b, half)
                scores(cfg, b + 1, 1 - half)
        return carry

    assert n_iter % 2 == 0
    lax.fori_loop(0, n_iter // 2 - 1, body, 0, unroll=UNROLL_B)
    for cfg in range(n_cfg):
        values(cfg, n_iter - 2, 0)
        scores(cfg, n_iter - 1, 1)
    for cfg in range(n_cfg):
        values(cfg, n_iter - 1, 1)

    def merge(i, carry):
        start = i * BLOCK
        xr = pl.ds(pl.multiple_of(start, BLOCK), BLOCK)
        nat = pl.ds(i // nblk2 + (i % nblk2) * (BLOCK * d2), BLOCK, stride=d2)
        sel = (nat, xr, xr)
        ms = [m_sc[c, sel[c], :] for c in range(len(DILATED))]
        m_all = functools.reduce(jnp.maximum, ms)
        ws = [jnp.exp(m - m_all) for m in ms]
        l_all = sum(w * l_sc[c, sel[c], :] for c, w in enumerate(ws))
        acc_all = sum(w * acc_sc[c, sel[c], :] for c, w in enumerate(ws))
        o_ref[0, nat, :] = acc_all * (1.0 / l_all)
        return carry

    lax.fori_loop(0, n_iter, merge, 0, unroll=True)


def _attn_b_call(qb, kb, vb, *, batch, seq, cast_weights=()):
    spec = pl.BlockSpec((1, seq, LANES), lambda b, hp: (b, 0, hp))
    shape3 = (batch, seq, WIDTH_B)
    pairs = WIDTH_B // LANES
    cast_specs = _cast_specs(cast_weights, batch * pairs, lambda b, hp: b * pairs + hp)
    return pl.pallas_call(
        functools.partial(_attn_b_kernel, seq=seq, n_casts=len(cast_weights)),
        grid=(batch, pairs),
        in_specs=[spec, spec, spec] + cast_specs,
        out_specs=[spec] + cast_specs,
        out_shape=[jax.ShapeDtypeStruct(shape3, F32)] + _cast_shapes(cast_weights),
        scratch_shapes=[pltpu.VMEM((3, seq, LANES), F32),
                        pltpu.VMEM((len(DILATED), 2, 2 * BLOCK, 2 * BLOCK), F32)]
        + [pltpu.VMEM((len(DILATED), seq, LANES), F32)] * 3,
        compiler_params=pltpu.CompilerParams(
            dimension_semantics=("arbitrary", "arbitrary"), vmem_limit_bytes=VMEM_LIMIT),
        name="attn_dilated",
    )(qb.reshape(shape3), kb.reshape(shape3), vb.reshape(shape3), *cast_weights)


def _outproj_kernel(oa_ref, ob_ref, x_ref, mod_ref, ga_ref, gb_ref, w_ref, b_ref, *rest,
                    sub, n_casts):
    o_ref = rest[n_casts]
    _cast_blocks(rest[:n_casts], rest[n_casts + 1:])
    ya = _rms(oa_ref[...], ga_ref[...]).astype(BF16)
    yb = _rms(ob_ref[...], gb_ref[...]).astype(BF16)
    y = (jnp.dot(ya, w_ref[0:WIDTH_A, :], preferred_element_type=F32)
         + jnp.dot(yb, w_ref[WIDTH_A:WIDTH_A + WIDTH_B, :], preferred_element_type=F32)
         + b_ref[...])
    gmod = mod_ref[0, pl.ds(3 * sub + 2, 1), :]
    o_ref[...] = x_ref[...] + gmod * y


def _outproj_call(out_a, out_b, x, mod, g_a, g_b, w_out, b_out, *, sub, seq, cast_weights=()):
    t, d = x.shape
    tm = TM_PROJ
    tiles_per_batch = seq // tm
    row = lambda i: (i, 0)
    const = lambda i: (0, 0)
    return pl.pallas_call(
        functools.partial(_outproj_kernel, sub=sub, n_casts=len(cast_weights)),
        grid=(t // tm,),
        in_specs=[
            pl.BlockSpec((tm, WIDTH_A), row),
            pl.BlockSpec((tm, WIDTH_B), row),
            pl.BlockSpec((tm, d), row),
            pl.BlockSpec((1, N_SUB * N_MOD, d), lambda i: (i // tiles_per_batch, 0, 0)),
            pl.BlockSpec((1, WIDTH_A), const),
            pl.BlockSpec((1, WIDTH_B), const),
            pl.BlockSpec((WIDTH_A + WIDTH_B, d), const),
            pl.BlockSpec((1, d), const),
        ] + _cast_specs(cast_weights, t // tm),
        out_specs=[pl.BlockSpec((tm, d), row)] + _cast_specs(cast_weights, t // tm),
        out_shape=[jax.ShapeDtypeStruct((t, d), F32)] + _cast_shapes(cast_weights),
        compiler_params=pltpu.CompilerParams(
            dimension_semantics=("arbitrary",), vmem_limit_bytes=VMEM_LIMIT),
        name="outproj_residual",
    )(out_a, out_b, x, mod, g_a.reshape(1, WIDTH_A), g_b.reshape(1, WIDTH_B), w_out,
      b_out.reshape(1, d), *cast_weights)


def kernel(x, c, positions, w_ada, b_ada, g_ffn1, w_ffn1_in, w_ffn1_out, g_mix, w_in, b_in, sinks,
           g_out_a, g_out_b, w_out, b_out, g_ffn2, w_ffn2_in, w_ffn2_out, g_final):
    batch, seq, d = x.shape
    depth = w_ada.shape[0]
    t = batch * seq
    xt = x.reshape(t, d)
    pos = positions.reshape(t, 1)
    lane_dim = jnp.arange(LANES, dtype=jnp.int32) % (ROT_DIM // 2)
    invf = (ROPE_THETA ** (-(2.0 * lane_dim.astype(F32)) / ROT_DIM)).reshape(1, LANES)
    for layer in range(depth):
        mod = _ada_call(c, w_ada[layer], b_ada[layer]).reshape(batch, N_SUB * N_MOD, d)
        last = layer == depth - 1
        xt, w_in_bf = _ffn_call(
            xt, mod, g_ffn1[layer], w_ffn1_in[layer].astype(BF16), w_ffn1_out[layer].astype(BF16),
            g_final, sub=0, seq=seq, final_norm=False, cast_weights=(w_in[layer],))
        qa, ka, va, qb, kb, vb, w_out_bf, w_down2_bf = _inproj_call(
            xt, mod, g_mix[layer], w_in_bf, b_in[layer], pos, invf, sub=1, seq=seq,
            cast_weights=(w_out[layer], w_ffn2_out[layer]))
        out_a = _attn_a_call(sinks[layer], qa, ka, va, batch=batch, seq=seq)
        out_b, w_gu2_bf = _attn_b_call(qb, kb, vb, batch=batch, seq=seq,
                                       cast_weights=(w_ffn2_in[layer],))
        xt, = _outproj_call(out_a, out_b.reshape(t, WIDTH_B), xt, mod, g_out_a[layer],
                            g_out_b[layer], w_out_bf, b_out[layer], sub=1, seq=seq)
        xt, = _ffn_call(xt, mod, g_ffn2[layer], w_gu2_bf, w_down2_bf, g_final, sub=2, seq=seq,
                        final_norm=last)
    if depth == 0:
        raise ValueError("depth must be >= 1")
    return xt.reshape(batch, seq, d)
```

```python
import functools

import jax
import jax.numpy as jnp
from jax import lax
from jax.experimental import pallas as pl
from jax.experimental.pallas import tpu as pltpu

F32 = jnp.float32
BF16 = jnp.bfloat16

D_MODEL = 2048
HEAD_DIM = 64
N_HEADS_A = 16
N_KV_A = 2
GROUP_A = N_HEADS_A // N_KV_A
N_HEADS_B = 16
WIDTH_A = N_HEADS_A * HEAD_DIM
WIDTH_KV_A = N_KV_A * HEAD_DIM
WIDTH_B = N_HEADS_B * HEAD_DIM
IN_WIDTH = WIDTH_A + 2 * WIDTH_KV_A + 3 * WIDTH_B
WINDOW_A = 128
DILATED = ((128, 1), (512, 4), (2048, 16))
BLOCK = 128
ROPE_THETA = 500000.0
ROT_DIM = HEAD_DIM // 4
D_FF = 5632
FFN_RES = 0.5
N_SUB = 3
N_MOD = 3
EPS = 1e-5

LANES = 128
MXU_COLS = 256
NEG_BIG = -1e30
VMEM_LIMIT = 56 * 1024 * 1024

TM_FFN = 512
TF_FFN = 512
TM_PROJ = 512
TN_ADA = 1024
GROUP_CHUNKS = 2
BF16_ROWS = 16
QBLOCKS_A = 2
UNROLL_B = 7


def _silu(x):
    return x * (1.0 / (1.0 + jnp.exp(-x)))


def _rms(x, g):
    ms = jnp.mean(x * x, axis=-1, keepdims=True)
    return (x * lax.rsqrt(ms + EPS)) * g


def _modulated_norm(x, g_ref, mod_ref, sub):
    shift = mod_ref[0, pl.ds(3 * sub + 0, 1), :]
    scale = mod_ref[0, pl.ds(3 * sub + 1, 1), :]
    return (_rms(x, g_ref[...]) * (1.0 + scale) + shift).astype(BF16)


def _ada_kernel(c_ref, w_ref, b_ref, o_ref):
    cond = _silu(c_ref[...]).astype(BF16)
    o_ref[...] = jnp.dot(cond, w_ref[...].astype(BF16), preferred_element_type=F32) + b_ref[...]


def _ada_call(c, w_ada, b_ada):
    b, d = c.shape
    n = w_ada.shape[1]
    return pl.pallas_call(
        _ada_kernel,
        grid=(n // TN_ADA,),
        in_specs=[
            pl.BlockSpec((b, d), lambda j: (0, 0)),
            pl.BlockSpec((d, TN_ADA), lambda j: (0, j)),
            pl.BlockSpec((1, TN_ADA), lambda j: (0, j)),
        ],
        out_specs=pl.BlockSpec((b, TN_ADA), lambda j: (0, j)),
        out_shape=jax.ShapeDtypeStruct((b, n), F32),
        compiler_params=pltpu.CompilerParams(
            dimension_semantics=("arbitrary",), vmem_limit_bytes=VMEM_LIMIT),
        name="adaln_mod",
    )(c, w_ada, b_ada.reshape(1, n))


def _ffn_kernel(x_hbm, mod_ref, modn_ref, g_ref, wgu_hbm, wd_hbm, gfin_ref, *rest,
                sub, n_tiles, n_chunks, final_norm, n_casts):
    cast_in, rest = rest[:n_casts], rest[n_casts:]
    o_hbm, cast_out = rest[0], rest[1:1 + n_casts]
    (xbuf, h_ref, hn_ref, acc_ref, obuf, wg_buf, wu_buf, wd_buf,
     xsem, wsem, osem) = rest[1 + n_casts:]
    i = pl.program_id(0)
    tm = h_ref.shape[0]
    tf = wd_buf.shape[1] // GROUP_CHUNKS
    dff = n_chunks * tf
    cur = i % 2
    nxt = 1 - cur
    n_groups = -(-n_chunks // GROUP_CHUNKS)
    norm_groups = min(n_groups - 1, 4)
    norm_rows = tm // norm_groups
    assert n_groups % 2 == 0 and norm_rows * norm_groups == tm and norm_rows % BF16_ROWS == 0

    def x_copy(tile, slot):
        start = tile * tm if isinstance(tile, int) else pl.multiple_of(tile * tm, tm)
        return pltpu.make_async_copy(x_hbm.at[pl.ds(start, tm), :], xbuf.at[slot], xsem.at[slot])

    def out_copy(tile):
        return pltpu.make_async_copy(obuf, o_hbm.at[pl.ds(pl.multiple_of(tile * tm, tm), tm), :],
                                     osem.at[0])

    def w_copies(group):
        slot = group % 2
        col = group * GROUP_CHUNKS * tf
        width = min(GROUP_CHUNKS, n_chunks - group * GROUP_CHUNKS) * tf
        return (
            pltpu.make_async_copy(wgu_hbm.at[:, pl.ds(col, width)],
                                  wg_buf.at[slot, :, pl.ds(0, width)], wsem.at[slot]),
            pltpu.make_async_copy(wgu_hbm.at[:, pl.ds(dff + col, width)],
                                  wu_buf.at[slot, :, pl.ds(0, width)], wsem.at[2 + slot]),
            pltpu.make_async_copy(wd_hbm.at[pl.ds(col, width), :],
                                  wd_buf.at[slot, pl.ds(0, width), :], wsem.at[4 + slot]),
        )

    @pl.when(i == 0)
    def _():
        x_copy(0, 0).start()
        for c in w_copies(0):
            c.start()
        x_copy(0, 0).wait()
        h_ref[...] = _modulated_norm(xbuf[0], g_ref, mod_ref, sub)

    @pl.when(i > 0)
    def _():
        h_ref[...] = hn_ref[...]

    next_tile = jnp.minimum(i + 1, n_tiles - 1)
    x_copy(next_tile, nxt).start()
    _cast_blocks(cast_in, cast_out)

    for group in range(n_groups):
        slot = group % 2
        for c in w_copies(group):
            c.wait()
        for c in w_copies((group + 1) % n_groups):
            c.start()
        if group == n_groups - norm_groups:
            x_copy(next_tile, nxt).wait()
        for k in range(min(GROUP_CHUNKS, n_chunks - group * GROUP_CHUNKS)):
            cols = slice(k * tf, (k + 1) * tf)
            h = h_ref[...]
            gate = jnp.dot(h, wg_buf[slot, :, cols], preferred_element_type=F32)
            up = jnp.dot(h, wu_buf[slot, :, cols], preferred_element_type=F32)
            act = (_silu(gate) * up).astype(BF16)
            down = jnp.dot(act, wd_buf[slot, cols, :], preferred_element_type=F32)
            if group == 0 and k == 0:
                acc_ref[...] = down
            else:
                acc_ref[...] += down
        if group >= n_groups - norm_groups:
            rows = pl.ds((group - (n_groups - norm_groups)) * norm_rows, norm_rows)
            hn_ref[rows, :] = _modulated_norm(xbuf[nxt, rows, :], g_ref, modn_ref, sub)

    @pl.when(i == n_tiles - 1)
    def _():
        for c in w_copies(0):
            c.wait()

    @pl.when(i > 0)
    def _():
        out_copy(i - 1).wait()

    gmod = mod_ref[0, pl.ds(3 * sub + 2, 1), :]
    out = xbuf[cur] + (FFN_RES * gmod) * acc_ref[...]
    if final_norm:
        out = _rms(out, gfin_ref[...])
    obuf[...] = out
    out_copy(i).start()

    @pl.when(i == n_tiles - 1)
    def _():
        out_copy(i).wait()


def _cast_specs(weights, n_steps, step=lambda i: i):
    specs = []
    for w in weights:
        rows, cols = w.shape
        assert rows % (n_steps * BF16_ROWS) == 0
        specs.append(pl.BlockSpec((rows // n_steps, cols), lambda *ids: (step(*ids), 0)))
    return specs


def _cast_shapes(weights):
    return [jax.ShapeDtypeStruct(w.shape, BF16) for w in weights]


def _cast_blocks(cast_in, cast_out):
    for src, dst in zip(cast_in, cast_out):
        dst[...] = src[...].astype(BF16)


def _ffn_call(x, mod, g, w_gu, w_down, g_final, *, sub, seq, final_norm, cast_weights=()):
    t, d = x.shape
    dff = w_down.shape[0]
    tm, tf = TM_FFN, TF_FFN
    n_tiles, n_chunks = t // tm, dff // tf
    tiles_per_batch = seq // tm
    group = GROUP_CHUNKS * tf
    nxt = lambda i: jnp.minimum(i + 1, n_tiles - 1)
    const = lambda i: (0, 0)
    kern = functools.partial(_ffn_kernel, sub=sub, n_tiles=n_tiles, n_chunks=n_chunks,
                             final_norm=final_norm, n_casts=len(cast_weights))
    return pl.pallas_call(
        kern,
        grid=(n_tiles,),
        in_specs=[
            pl.BlockSpec(memory_space=pl.ANY),
            pl.BlockSpec((1, N_SUB * N_MOD, d), lambda i: (i // tiles_per_batch, 0, 0)),
            pl.BlockSpec((1, N_SUB * N_MOD, d), lambda i: (nxt(i) // tiles_per_batch, 0, 0)),
            pl.BlockSpec((1, d), const),
            pl.BlockSpec(memory_space=pl.ANY),
            pl.BlockSpec(memory_space=pl.ANY),
            pl.BlockSpec((1, d), const),
        ] + _cast_specs(cast_weights, n_tiles),
        out_specs=[pl.BlockSpec(memory_space=pl.ANY)] + _cast_specs(cast_weights, n_tiles),
        out_shape=[jax.ShapeDtypeStruct((t, d), F32)] + _cast_shapes(cast_weights),
        scratch_shapes=[
            pltpu.VMEM((2, tm, d), F32),
            pltpu.VMEM((tm, d), BF16),
            pltpu.VMEM((tm, d), BF16),
            pltpu.VMEM((tm, d), F32),
            pltpu.VMEM((tm, d), F32),
            pltpu.VMEM((2, d, group), BF16),
            pltpu.VMEM((2, d, group), BF16),
            pltpu.VMEM((2, group, d), BF16),
            pltpu.SemaphoreType.DMA((2,)),
            pltpu.SemaphoreType.DMA((6,)),
            pltpu.SemaphoreType.DMA((1,)),
        ],
        compiler_params=pltpu.CompilerParams(
            dimension_semantics=("arbitrary",), vmem_limit_bytes=VMEM_LIMIT),
        name="ffn_sub%d" % sub,
    )(x, mod, mod, g.reshape(1, d), w_gu, w_down, g_final.reshape(1, d), *cast_weights)


def _rope_tables(pos_ref, invf_ref):
    lane = lax.broadcasted_iota(jnp.int32, (1, LANES), 1)
    d = lane % HEAD_DIM
    ang = pos_ref[...].astype(F32) * invf_ref[...]
    c = jnp.cos(ang)
    s = jnp.sin(ang)
    half = ROT_DIM // 2
    coef_self = jnp.where(d < ROT_DIM, c, 1.0)
    coef_next = jnp.where(d < half, -s, 0.0)
    coef_prev = jnp.where((d >= half) & (d < ROT_DIM), s, 0.0)
    return coef_self, coef_next, coef_prev


def _rope_slab(z, tables):
    coef_self, coef_next, coef_prev = tables
    half = ROT_DIM // 2
    nxt = pltpu.roll(z, LANES - half, axis=1)
    prv = pltpu.roll(z, half, axis=1)
    return z * coef_self + nxt * coef_next + prv * coef_prev


def _inproj_kernel(x_ref, mod_ref, g_ref, w_ref, b_ref, pos_ref, invf_ref, *rest, sub, n_casts):
    cast_in, rest = rest[:n_casts], rest[n_casts:]
    qa_ref, ka_ref, va_ref, qb_ref, kb_ref, vb_ref = rest[:6]
    _cast_blocks(cast_in, rest[6:])
    h = _modulated_norm(x_ref[...], g_ref, mod_ref, sub)
    tables = _rope_tables(pos_ref, invf_ref)
    lane = lax.broadcasted_iota(jnp.int32, (1, LANES), 1)
    lo = lane < HEAD_DIM
    qscale = HEAD_DIM ** -0.5

    def proj_slabs(col, width):
        for c0 in range(0, width, MXU_COLS):
            z = (jnp.dot(h, w_ref[:, col + c0:col + c0 + MXU_COLS], preferred_element_type=F32)
                 + b_ref[:, col + c0:col + c0 + MXU_COLS])
            for half in range(MXU_COLS // LANES):
                yield c0 // LANES + half, z[:, half * LANES:(half + 1) * LANES]

    def dup_heads(slab):
        swapped = pltpu.roll(slab, HEAD_DIM, axis=1)
        return jnp.where(lo, slab, swapped), jnp.where(lo, swapped, slab)

    col = 0
    for k, z in proj_slabs(col, WIDTH_A):
        qa_ref[:, k * LANES:(k + 1) * LANES] = (_rope_slab(z, tables) * qscale).astype(BF16)
    col += WIDTH_A
    for k, z in proj_slabs(col, 2 * WIDTH_KV_A):
        dst = ka_ref if k == 0 else va_ref
        h0, h1 = dup_heads(_rope_slab(z, tables) if k == 0 else z)
        dst[:, 0:LANES] = h0.astype(BF16)
        dst[:, LANES:2 * LANES] = h1.astype(BF16)
    col += 2 * WIDTH_KV_A
    for k, z in proj_slabs(col, WIDTH_B):
        qb_ref[:, k * LANES:(k + 1) * LANES] = _rope_slab(z, tables) * qscale
    col += WIDTH_B
    for k, z in proj_slabs(col, WIDTH_B):
        kb_ref[:, k * LANES:(k + 1) * LANES] = _rope_slab(z, tables)
    col += WIDTH_B
    for k, z in proj_slabs(col, WIDTH_B):
        vb_ref[:, k * LANES:(k + 1) * LANES] = z


def _inproj_call(x, mod, g, w_in, b_in, pos, invf, *, sub, seq, cast_weights=()):
    t, d = x.shape
    tm = TM_PROJ
    tiles_per_batch = seq // tm
    row = lambda i: (i, 0)
    const = lambda i: (0, 0)
    return pl.pallas_call(
        functools.partial(_inproj_kernel, sub=sub, n_casts=len(cast_weights)),
        grid=(t // tm,),
        in_specs=[
            pl.BlockSpec((tm, d), row),
            pl.BlockSpec((1, N_SUB * N_MOD, d), lambda i: (i // tiles_per_batch, 0, 0)),
            pl.BlockSpec((1, d), const),
            pl.BlockSpec((d, IN_WIDTH), const, pipeline_mode=pl.Buffered(1)),
            pl.BlockSpec((1, IN_WIDTH), const),
            pl.BlockSpec((tm, 1), row),
            pl.BlockSpec((1, LANES), const),
        ] + _cast_specs(cast_weights, t // tm),
        out_specs=[
            pl.BlockSpec((tm, WIDTH_A), row),
            pl.BlockSpec((tm, 2 * LANES), row),
            pl.BlockSpec((tm, 2 * LANES), row),
            pl.BlockSpec((tm, WIDTH_B), row),
            pl.BlockSpec((tm, WIDTH_B), row),
            pl.BlockSpec((tm, WIDTH_B), row),
        ] + _cast_specs(cast_weights, t // tm),
        out_shape=[
            jax.ShapeDtypeStruct((t, WIDTH_A), BF16),
            jax.ShapeDtypeStruct((t, 2 * LANES), BF16),
            jax.ShapeDtypeStruct((t, 2 * LANES), BF16),
            jax.ShapeDtypeStruct((t, WIDTH_B), F32),
            jax.ShapeDtypeStruct((t, WIDTH_B), F32),
            jax.ShapeDtypeStruct((t, WIDTH_B), F32),
        ] + _cast_shapes(cast_weights),
        compiler_params=pltpu.CompilerParams(
            dimension_semantics=("arbitrary",), vmem_limit_bytes=VMEM_LIMIT),
        name="inproj_rope",
    )(x, mod, g.reshape(1, d), w_in, b_in.reshape(1, IN_WIDTH), pos, invf, *cast_weights)


def _attn_a_kernel(sink_ref, q_ref, kp_ref, kc_ref, vp_ref, vc_ref, o_ref):
    i = pl.program_id(1)
    lane = lax.broadcasted_iota(jnp.int32, (1, LANES), 1)
    lo = lane < HEAD_DIM
    qi = lax.broadcasted_iota(jnp.int32, (BLOCK, 2 * BLOCK), 0)
    kj = lax.broadcasted_iota(jnp.int32, (BLOCK, 2 * BLOCK), 1)
    dist = qi + BLOCK - kj
    band = (dist >= 0) & (dist <= WINDOW_A - 1)
    bias_inner = jnp.where(band, 0.0, NEG_BIG).astype(F32)
    bias_first = jnp.where(band & ((kj >= BLOCK) | (i > 0)), 0.0, NEG_BIG).astype(F32)
    zero = jnp.zeros((BLOCK, LANES), BF16)
    ones = jnp.ones((2 * BLOCK, LANES), BF16)
    pairs = GROUP_A // 2
    for u in range(QBLOCKS_A):
        r0 = u * BLOCK
        bias = bias_first if u == 0 else bias_inner
        for hk in range(N_KV_A):
            cols = slice(hk * LANES, (hk + 1) * LANES)
            k_prev = kp_ref[:, cols] if u == 0 else kc_ref[r0 - BLOCK:r0, cols]
            v_prev = vp_ref[:, cols] if u == 0 else vc_ref[r0 - BLOCK:r0, cols]
            kk = jnp.concatenate([k_prev, kc_ref[r0:r0 + BLOCK, cols]], axis=0)
            vv = jnp.concatenate([v_prev, vc_ref[r0:r0 + BLOCK, cols]], axis=0)
            rows = []
            for jp in range(pairs):
                c0 = (hk * pairs + jp) * LANES
                qp = q_ref[r0:r0 + BLOCK, c0:c0 + LANES]
                rows.append(jnp.where(lo, qp, zero))
                rows.append(jnp.where(lo, zero, qp))
            lhs = jnp.concatenate(rows, axis=0)
            s_all = lax.dot_general(lhs, kk, (((1,), (1,)), ((), ())),
                                    preferred_element_type=F32)
            ps, ms = [], []
            for g in range(GROUP_A):
                sink = sink_ref[hk * GROUP_A + g]
                s = s_all[g * BLOCK:(g + 1) * BLOCK, :] + bias
                m = jnp.maximum(jnp.max(s, axis=-1, keepdims=True), sink)
                ps.append(jnp.exp(s - m).astype(BF16))
                ms.append(m)
            v3 = jnp.concatenate([vv, ones], axis=1)
            pv = jnp.dot(jnp.concatenate(ps, axis=0), v3, preferred_element_type=F32)
            outs = []
            for g in range(GROUP_A):
                sink = sink_ref[hk * GROUP_A + g]
                blk = pv[g * BLOCK:(g + 1) * BLOCK, :]
                l = blk[:, LANES:] + jnp.exp(sink - ms[g])
                outs.append(blk[:, :LANES] * (1.0 / l))
            for jp in range(pairs):
                c0 = (hk * pairs + jp) * LANES
                o_ref[r0:r0 + BLOCK, c0:c0 + LANES] = jnp.where(lo, outs[2 * jp], outs[2 * jp + 1])


def _attn_a_call(sinks, qa, ka, va, *, batch, seq):
    nb = seq // BLOCK
    rows = QBLOCKS_A * BLOCK
    steps = nb // QBLOCKS_A
    cur = lambda b, i: (b * steps + i, 0)
    prev = lambda b, i: (b * nb + jnp.maximum(QBLOCKS_A * i - 1, 0), 0)
    return pl.pallas_call(
        _attn_a_kernel,
        grid=(batch, steps),
        in_specs=[
            pl.BlockSpec(memory_space=pltpu.SMEM),
            pl.BlockSpec((rows, WIDTH_A), cur),
            pl.BlockSpec((BLOCK, 2 * LANES), prev),
            pl.BlockSpec((rows, 2 * LANES), cur),
            pl.BlockSpec((BLOCK, 2 * LANES), prev),
            pl.BlockSpec((rows, 2 * LANES), cur),
        ],
        out_specs=pl.BlockSpec((rows, WIDTH_A), cur),
        out_shape=jax.ShapeDtypeStruct((batch * seq, WIDTH_A), F32),
        compiler_params=pltpu.CompilerParams(
            dimension_semantics=("arbitrary", "arbitrary"), vmem_limit_bytes=VMEM_LIMIT),
        name="attn_swa_gqa",
    )(sinks, qa, ka, ka, va, va)


def _attn_b_kernel(q_ref, k_ref, v_ref, *rest, seq, n_casts):
    cast_in, rest = rest[:n_casts], rest[n_casts:]
    o_ref, cast_out = rest[0], rest[1:1 + n_casts]
    x4_ref, t_sc, m_sc, l_sc, acc_sc = rest[1 + n_casts:]
    _cast_blocks(cast_in, cast_out)
    (_, d1), (_, d2), (_, d3) = DILATED
    assert d1 == 1 and d3 == d2 * d2 and all(w // dil == BLOCK for w, dil in DILATED)
    n_iter = seq // BLOCK
    cls = seq // d2
    nblk2 = cls // BLOCK
    assert seq // d3 == BLOCK and n_iter == d2 * nblk2 == d3

    lane = lax.broadcasted_iota(jnp.int32, (1, LANES), 1)
    lo = lane < HEAD_DIM
    qi = lax.broadcasted_iota(jnp.int32, (BLOCK, BLOCK), 0)
    kj = lax.broadcasted_iota(jnp.int32, (BLOCK, BLOCK), 1)
    bias_cur = jnp.where(kj <= qi, 0.0, NEG_BIG).astype(F32)
    bias_prev = jnp.where(kj >= qi, 0.0, NEG_BIG).astype(F32)
    zero = jnp.zeros((BLOCK, LANES), BF16)
    ones = jnp.ones((2 * BLOCK, LANES), BF16)

    for a, ref in enumerate((q_ref, k_ref, v_ref)):
        for r in range(d2):
            x4_ref[a, r * cls:(r + 1) * cls, :] = ref[0, pl.ds(r, cls, stride=d2), :]

    natural = lambda a, rows: (q_ref, k_ref, v_ref)[a][0, rows, :]
    by_class = lambda a, rows: x4_ref[a, rows, :]

    def block_rows(cfg, idx):
        static = isinstance(idx, int)
        block = lambda start: pl.ds(start if static else pl.multiple_of(start, BLOCK), BLOCK)
        biggest = max if static else jnp.maximum
        start = idx * BLOCK
        own = block(start)
        if cfg == 0:
            return natural, own, block(biggest(start - BLOCK, 0)), idx > 0
        if cfg == 1:
            n = idx % nblk2
            return by_class, own, block(biggest(start - BLOCK, start - n * BLOCK)), n > 0
        return by_class, pl.ds((idx % d2) * cls + idx // d2, BLOCK, stride=d2), None, None

    def scores(cfg, idx, slot):
        load, own, prev, has_prev = block_rows(cfg, idx)
        qb = load(0, own).astype(BF16)
        lhs = jnp.concatenate([jnp.where(lo, qb, zero), jnp.where(lo, zero, qb)], axis=0)
        if prev is None:
            keys, bias = load(1, own), bias_cur
        else:
            keys = jnp.concatenate([load(1, prev), load(1, own)], axis=0)
            off = jnp.where(has_prev, 0.0, NEG_BIG).astype(F32)
            bias = jnp.concatenate([bias_prev + off, bias_cur], axis=1)
        nk = keys.shape[0]
        s = lax.dot_general(lhs, keys.astype(BF16), (((1,), (1,)), ((), ())),
                            preferred_element_type=F32)
        s0 = s[:BLOCK, :] + bias
        s1 = s[BLOCK:, :] + bias
        m0 = jnp.max(s0, axis=-1, keepdims=True)
        m1 = jnp.max(s1, axis=-1, keepdims=True)
        t_sc[cfg, slot, :BLOCK, :nk] = s0 - m0
        t_sc[cfg, slot, BLOCK:, :nk] = s1 - m1
        m_sc[cfg, own, :] = jnp.where(lo, m0, m1)

    def values(cfg, idx, slot):
        load, own, prev, _ = block_rows(cfg, idx)
        vals = load(2, own) if prev is None else jnp.concatenate([load(2, prev), load(2, own)], axis=0)
        nk = vals.shape[0]
        p = jnp.exp(t_sc[cfg, slot, :, :nk]).astype(BF16)
        v3 = jnp.concatenate([vals.astype(BF16), ones[:nk, :]], axis=1)
        pv = jnp.dot(p, v3, preferred_element_type=F32)
        acc_sc[cfg, own, :] = jnp.where(lo, pv[:BLOCK, :LANES], pv[BLOCK:, :LANES])
        l_sc[cfg, own, :] = jnp.where(lo, pv[:BLOCK, LANES:], pv[BLOCK:, LANES:])

    n_cfg = len(DILATED)
    for cfg in range(n_cfg):
        scores(cfg, 0, 0)

    def body(k, carry):
        for half in range(2):
            b = 2 * k + half
            for cfg in range(n_cfg):
                values(cfg, b, half)
                scores(cfg, b + 1, 1 - half)
        return carry

    assert n_iter % 2 == 0
    lax.fori_loop(0, n_iter // 2 - 1, body, 0, unroll=UNROLL_B)
    for cfg in range(n_cfg):
        values(cfg, n_iter - 2, 0)
        scores(cfg, n_iter - 1, 1)
    for cfg in range(n_cfg):
        values(cfg, n_iter - 1, 1)

    def merge(i, carry):
        start = i * BLOCK
        xr = pl.ds(pl.multiple_of(start, BLOCK), BLOCK)
        nat = pl.ds(i // nblk2 + (i % nblk2) * (BLOCK * d2), BLOCK, stride=d2)
        sel = (nat, xr, xr)
        ms = [m_sc[c, sel[c], :] for c in range(len(DILATED))]
        m_all = functools.reduce(jnp.maximum, ms)
        ws = [jnp.exp(m - m_all) for m in ms]
        l_all = sum(w * l_sc[c, sel[c], :] for c, w in enumerate(ws))
        acc_all = sum(w * acc_sc[c, sel[c], :] for c, w in enumerate(ws))
        o_ref[0, nat, :] = acc_all * (1.0 / l_all)
        return carry

    lax.fori_loop(0, n_iter, merge, 0, unroll=True)


def _attn_b_call(qb, kb, vb, *, batch, seq, cast_weights=()):
    spec = pl.BlockSpec((1, seq, LANES), lambda b, hp: (b, 0, hp))
    shape3 = (batch, seq, WIDTH_B)
    pairs = WIDTH_B // LANES
    cast_specs = _cast_specs(cast_weights, batch * pairs, lambda b, hp: b * pairs + hp)
    return pl.pallas_call(
        functools.partial(_attn_b_kernel, seq=seq, n_casts=len(cast_weights)),
        grid=(batch, pairs),
        in_specs=[spec, spec, spec] + cast_specs,
        out_specs=[spec] + cast_specs,
        out_shape=[jax.ShapeDtypeStruct(shape3, F32)] + _cast_shapes(cast_weights),
        scratch_shapes=[pltpu.VMEM((3, seq, LANES), F32),
                        pltpu.VMEM((len(DILATED), 2, 2 * BLOCK, 2 * BLOCK), F32)]
        + [pltpu.VMEM((len(DILATED), seq, LANES), F32)] * 3,
        compiler_params=pltpu.CompilerParams(
            dimension_semantics=("arbitrary", "arbitrary"), vmem_limit_bytes=VMEM_LIMIT),
        name="attn_dilated",
    )(qb.reshape(shape3), kb.reshape(shape3), vb.reshape(shape3), *cast_weights)


def _outproj_kernel(oa_ref, ob_ref, x_ref, mod_ref, ga_ref, gb_ref, w_ref, b_ref, *rest,
                    sub, n_casts):
    o_ref = rest[n_casts]
    _cast_blocks(rest[:n_casts], rest[n_casts + 1:])
    ya = _rms(oa_ref[...], ga_ref[...]).astype(BF16)
    yb = _rms(ob_ref[...], gb_ref[...]).astype(BF16)
    y = (jnp.dot(ya, w_ref[0:WIDTH_A, :], preferred_element_type=F32)
         + jnp.dot(yb, w_ref[WIDTH_A:WIDTH_A + WIDTH_B, :], preferred_element_type=F32)
         + b_ref[...])
    gmod = mod_ref[0, pl.ds(3 * sub + 2, 1), :]
    o_ref[...] = x_ref[...] + gmod * y


def _outproj_call(out_a, out_b, x, mod, g_a, g_b, w_out, b_out, *, sub, seq, cast_weights=()):
    t, d = x.shape
    tm = TM_PROJ
    tiles_per_batch = seq // tm
    row = lambda i: (i, 0)
    const = lambda i: (0, 0)
    return pl.pallas_call(
        functools.partial(_outproj_kernel, sub=sub, n_casts=len(cast_weights)),
        grid=(t // tm,),
        in_specs=[
            pl.BlockSpec((tm, WIDTH_A), row),
            pl.BlockSpec((tm, WIDTH_B), row),
            pl.BlockSpec((tm, d), row),
            pl.BlockSpec((1, N_SUB * N_MOD, d), lambda i: (i // tiles_per_batch, 0, 0)),
            pl.BlockSpec((1, WIDTH_A), const),
            pl.BlockSpec((1, WIDTH_B), const),
            pl.BlockSpec((WIDTH_A + WIDTH_B, d), const),
            pl.BlockSpec((1, d), const),
        ] + _cast_specs(cast_weights, t // tm),
        out_specs=[pl.BlockSpec((tm, d), row)] + _cast_specs(cast_weights, t // tm),
        out_shape=[jax.ShapeDtypeStruct((t, d), F32)] + _cast_shapes(cast_weights),
        compiler_params=pltpu.CompilerParams(
            dimension_semantics=("arbitrary",), vmem_limit_bytes=VMEM_LIMIT),
        name="outproj_residual",
    )(out_a, out_b, x, mod, g_a.reshape(1, WIDTH_A), g_b.reshape(1, WIDTH_B), w_out,
      b_out.reshape(1, d), *cast_weights)


def kernel(x, c, positions, w_ada, b_ada, g_ffn1, w_ffn1_in, w_ffn1_out, g_mix, w_in, b_in, sinks,
           g_out_a, g_out_b, w_out, b_out, g_ffn2, w_ffn2_in, w_ffn2_out, g_final):
    batch, seq, d = x.shape
    depth = w_ada.shape[0]
    t = batch * seq
    xt = x.reshape(t, d)
    pos = positions.reshape(t, 1)
    lane_dim = jnp.arange(LANES, dtype=jnp.int32) % (ROT_DIM // 2)
    invf = (ROPE_THETA ** (-(2.0 * lane_dim.astype(F32)) / ROT_DIM)).reshape(1, LANES)
    for layer in range(depth):
        mod = _ada_call(c, w_ada[layer], b_ada[layer]).reshape(batch, N_SUB * N_MOD, d)
        last = layer == depth - 1
        xt, w_in_bf = _ffn_call(
            xt, mod, g_ffn1[layer], w_ffn1_in[layer].astype(BF16), w_ffn1_out[layer].astype(BF16),
            g_final, sub=0, seq=seq, final_norm=False, cast_weights=(w_in[layer],))
        qa, ka, va, qb, kb, vb, w_out_bf, w_down2_bf = _inproj_call(
            xt, mod, g_mix[layer], w_in_bf, b_in[layer], pos, invf, sub=1, seq=seq,
            cast_weights=(w_out[layer], w_ffn2_out[layer]))
        out_a = _attn_a_call(sinks[layer], qa, ka, va, batch=batch, seq=seq)
        out_b, w_gu2_bf = _attn_b_call(qb, kb, vb, batch=batch, seq=seq,
                                       cast_weights=(w_ffn2_in[layer],))
        xt, = _outproj_call(out_a, out_b.reshape(t, WIDTH_B), xt, mod, g_out_a[layer],
                            g_out_b[layer], w_out_bf, b_out[layer], sub=1, seq=seq)
        xt, = _ffn_call(xt, mod, g_ffn2[layer], w_gu2_bf, w_down2_bf, g_final, sub=2, seq=seq,
                        final_norm=last)
    if depth == 0:
        raise ValueError("depth must be >= 1")
    return xt.reshape(batch, seq, d)
```

```python
import functools

import jax
import jax.numpy as jnp
from jax import lax
from jax.experimental import pallas as pl
from jax.experimental.pallas import tpu as pltpu

F32 = jnp.float32
BF16 = jnp.bfloat16

D_MODEL = 2048
HEAD_DIM = 64
N_HEADS_A = 16
N_KV_A = 2
GROUP_A = N_HEADS_A // N_KV_A
N_HEADS_B = 16
WIDTH_A = N_HEADS_A * HEAD_DIM
WIDTH_KV_A = N_KV_A * HEAD_DIM
WIDTH_B = N_HEADS_B * HEAD_DIM
IN_WIDTH = WIDTH_A + 2 * WIDTH_KV_A + 3 * WIDTH_B
WINDOW_A = 128
DILATED = ((128, 1), (512, 4), (2048, 16))
BLOCK = 128
ROPE_THETA = 500000.0
ROT_DIM = HEAD_DIM // 4
D_FF = 5632
FFN_RES = 0.5
N_SUB = 3
N_MOD = 3
EPS = 1e-5

LANES = 128
MXU_COLS = 256
NEG_BIG = -1e30
VMEM_LIMIT = 56 * 1024 * 1024

TM_FFN = 512
TF_FFN = 512
TM_PROJ = 512
TN_ADA = 1024
GROUP_CHUNKS = 2
BF16_ROWS = 16
QBLOCKS_A = 4
UNROLL_B = 7


def _silu(x):
    return x * (1.0 / (1.0 + jnp.exp(-x)))


def _rms(x, g):
    ms = jnp.mean(x * x, axis=-1, keepdims=True)
    return (x * lax.rsqrt(ms + EPS)) * g


def _modulated_norm(x, g_ref, mod_ref, sub):
    shift = mod_ref[0, pl.ds(3 * sub + 0, 1), :]
    scale = mod_ref[0, pl.ds(3 * sub + 1, 1), :]
    return (_rms(x, g_ref[...]) * (1.0 + scale) + shift).astype(BF16)


def _ada_kernel(c_ref, w_ref, b_ref, o_ref):
    cond = _silu(c_ref[...]).astype(BF16)
    o_ref[...] = jnp.dot(cond, w_ref[...].astype(BF16), preferred_element_type=F32) + b_ref[...]


def _ada_call(c, w_ada, b_ada):
    b, d = c.shape
    n = w_ada.shape[1]
    return pl.pallas_call(
        _ada_kernel,
        grid=(n // TN_ADA,),
        in_specs=[
            pl.BlockSpec((b, d), lambda j: (0, 0)),
            pl.BlockSpec((d, TN_ADA), lambda j: (0, j)),
            pl.BlockSpec((1, TN_ADA), lambda j: (0, j)),
        ],
        out_specs=pl.BlockSpec((b, TN_ADA), lambda j: (0, j)),
        out_shape=jax.ShapeDtypeStruct((b, n), F32),
        compiler_params=pltpu.CompilerParams(
            dimension_semantics=("arbitrary",), vmem_limit_bytes=VMEM_LIMIT),
        name="adaln_mod",
    )(c, w_ada, b_ada.reshape(1, n))


def _ffn_kernel(x_hbm, mod_ref, modn_ref, g_ref, wgu_hbm, wd_hbm, gfin_ref, *rest,
                sub, n_tiles, n_chunks, final_norm, n_casts):
    cast_in, rest = rest[:n_casts], rest[n_casts:]
    o_hbm, cast_out = rest[0], rest[1:1 + n_casts]
    (xbuf, h_ref, hn_ref, acc_ref, obuf, wg_buf, wu_buf, wd_buf,
     xsem, wsem, osem) = rest[1 + n_casts:]
    i = pl.program_id(0)
    tm = h_ref.shape[0]
    tf = wd_buf.shape[1] // GROUP_CHUNKS
    dff = n_chunks * tf
    cur = i % 2
    nxt = 1 - cur
    n_groups = -(-n_chunks // GROUP_CHUNKS)
    norm_groups = min(n_groups - 1, 4)
    norm_rows = tm // norm_groups
    assert n_groups % 2 == 0 and norm_rows * norm_groups == tm and norm_rows % BF16_ROWS == 0

    def x_copy(tile, slot):
        start = tile * tm if isinstance(tile, int) else pl.multiple_of(tile * tm, tm)
        return pltpu.make_async_copy(x_hbm.at[pl.ds(start, tm), :], xbuf.at[slot], xsem.at[slot])

    def out_copy(tile):
        return pltpu.make_async_copy(obuf, o_hbm.at[pl.ds(pl.multiple_of(tile * tm, tm), tm), :],
                                     osem.at[0])

    def w_copies(group):
        slot = group % 2
        col = group * GROUP_CHUNKS * tf
        width = min(GROUP_CHUNKS, n_chunks - group * GROUP_CHUNKS) * tf
        return (
            pltpu.make_async_copy(wgu_hbm.at[:, pl.ds(col, width)],
                                  wg_buf.at[slot, :, pl.ds(0, width)], wsem.at[slot]),
            pltpu.make_async_copy(wgu_hbm.at[:, pl.ds(dff + col, width)],
                                  wu_buf.at[slot, :, pl.ds(0, width)], wsem.at[2 + slot]),
            pltpu.make_async_copy(wd_hbm.at[pl.ds(col, width), :],
                                  wd_buf.at[slot, pl.ds(0, width), :], wsem.at[4 + slot]),
        )

    @pl.when(i == 0)
    def _():
        x_copy(0, 0).start()
        for c in w_copies(0):
            c.start()
        x_copy(0, 0).wait()
        h_ref[...] = _modulated_norm(xbuf[0], g_ref, mod_ref, sub)

    @pl.when(i > 0)
    def _():
        h_ref[...] = hn_ref[...]

    next_tile = jnp.minimum(i + 1, n_tiles - 1)
    x_copy(next_tile, nxt).start()
    _cast_blocks(cast_in, cast_out)

    for group in range(n_groups):
        slot = group % 2
        for c in w_copies(group):
            c.wait()
        for c in w_copies((group + 1) % n_groups):
            c.start()
        if group == n_groups - norm_groups:
            x_copy(next_tile, nxt).wait()
        for k in range(min(GROUP_CHUNKS, n_chunks - group * GROUP_CHUNKS)):
            cols = slice(k * tf, (k + 1) * tf)
            h = h_ref[...]
            gate = jnp.dot(h, wg_buf[slot, :, cols], preferred_element_type=F32)
            up = jnp.dot(h, wu_buf[slot, :, cols], preferred_element_type=F32)
            act = (_silu(gate) * up).astype(BF16)
            down = jnp.dot(act, wd_buf[slot, cols, :], preferred_element_type=F32)
            if group == 0 and k == 0:
                acc_ref[...] = down
            else:
                acc_ref[...] += down
        if group >= n_groups - norm_groups:
            rows = pl.ds((group - (n_groups - norm_groups)) * norm_rows, norm_rows)
            hn_ref[rows, :] = _modulated_norm(xbuf[nxt, rows, :], g_ref, modn_ref, sub)

    @pl.when(i == n_tiles - 1)
    def _():
        for c in w_copies(0):
            c.wait()

    @pl.when(i > 0)
    def _():
        out_copy(i - 1).wait()

    gmod = mod_ref[0, pl.ds(3 * sub + 2, 1), :]
    out = xbuf[cur] + (FFN_RES * gmod) * acc_ref[...]
    if final_norm:
        out = _rms(out, gfin_ref[...])
    obuf[...] = out
    out_copy(i).start()

    @pl.when(i == n_tiles - 1)
    def _():
        out_copy(i).wait()


def _cast_specs(weights, n_steps, step=lambda i: i):
    specs = []
    for w in weights:
        rows, cols = w.shape
        assert rows % (n_steps * BF16_ROWS) == 0
        specs.append(pl.BlockSpec((rows // n_steps, cols), lambda *ids: (step(*ids), 0)))
    return specs


def _cast_shapes(weights):
    return [jax.ShapeDtypeStruct(w.shape, BF16) for w in weights]


def _cast_blocks(cast_in, cast_out):
    for src, dst in zip(cast_in, cast_out):
        dst[...] = src[...].astype(BF16)


def _ffn_call(x, mod, g, w_gu, w_down, g_final, *, sub, seq, final_norm, cast_weights=()):
    t, d = x.shape
    dff = w_down.shape[0]
    tm, tf = TM_FFN, TF_FFN
    n_tiles, n_chunks = t // tm, dff // tf
    tiles_per_batch = seq // tm
    group = GROUP_CHUNKS * tf
    nxt = lambda i: jnp.minimum(i + 1, n_tiles - 1)
    const = lambda i: (0, 0)
    kern = functools.partial(_ffn_kernel, sub=sub, n_tiles=n_tiles, n_chunks=n_chunks,
                             final_norm=final_norm, n_casts=len(cast_weights))
    return pl.pallas_call(
        kern,
        grid=(n_tiles,),
        in_specs=[
            pl.BlockSpec(memory_space=pl.ANY),
            pl.BlockSpec((1, N_SUB * N_MOD, d), lambda i: (i // tiles_per_batch, 0, 0)),
            pl.BlockSpec((1, N_SUB * N_MOD, d), lambda i: (nxt(i) // tiles_per_batch, 0, 0)),
            pl.BlockSpec((1, d), const),
            pl.BlockSpec(memory_space=pl.ANY),
            pl.BlockSpec(memory_space=pl.ANY),
            pl.BlockSpec((1, d), const),
        ] + _cast_specs(cast_weights, n_tiles),
        out_specs=[pl.BlockSpec(memory_space=pl.ANY)] + _cast_specs(cast_weights, n_tiles),
        out_shape=[jax.ShapeDtypeStruct((t, d), F32)] + _cast_shapes(cast_weights),
        scratch_shapes=[
            pltpu.VMEM((2, tm, d), F32),
            pltpu.VMEM((tm, d), BF16),
            pltpu.VMEM((tm, d), BF16),
            pltpu.VMEM((tm, d), F32),
            pltpu.VMEM((tm, d), F32),
            pltpu.VMEM((2, d, group), BF16),
            pltpu.VMEM((2, d, group), BF16),
            pltpu.VMEM((2, group, d), BF16),
            pltpu.SemaphoreType.DMA((2,)),
            pltpu.SemaphoreType.DMA((6,)),
            pltpu.SemaphoreType.DMA((1,)),
        ],
        compiler_params=pltpu.CompilerParams(
            dimension_semantics=("arbitrary",), vmem_limit_bytes=VMEM_LIMIT),
        name="ffn_sub%d" % sub,
    )(x, mod, mod, g.reshape(1, d), w_gu, w_down, g_final.reshape(1, d), *cast_weights)


def _rope_tables(pos_ref, invf_ref):
    lane = lax.broadcasted_iota(jnp.int32, (1, LANES), 1)
    d = lane % HEAD_DIM
    ang = pos_ref[...].astype(F32) * invf_ref[...]
    c = jnp.cos(ang)
    s = jnp.sin(ang)
    half = ROT_DIM // 2
    coef_self = jnp.where(d < ROT_DIM, c, 1.0)
    coef_next = jnp.where(d < half, -s, 0.0)
    coef_prev = jnp.where((d >= half) & (d < ROT_DIM), s, 0.0)
    return coef_self, coef_next, coef_prev


def _rope_slab(z, tables):
    coef_self, coef_next, coef_prev = tables
    half = ROT_DIM // 2
    nxt = pltpu.roll(z, LANES - half, axis=1)
    prv = pltpu.roll(z, half, axis=1)
    return z * coef_self + nxt * coef_next + prv * coef_prev


def _inproj_kernel(x_ref, mod_ref, g_ref, w_ref, b_ref, pos_ref, invf_ref, *rest, sub, n_casts):
    cast_in, rest = rest[:n_casts], rest[n_casts:]
    qa_ref, ka_ref, va_ref, qb_ref, kb_ref, vb_ref = rest[:6]
    _cast_blocks(cast_in, rest[6:])
    h = _modulated_norm(x_ref[...], g_ref, mod_ref, sub)
    tables = _rope_tables(pos_ref, invf_ref)
    lane = lax.broadcasted_iota(jnp.int32, (1, LANES), 1)
    lo = lane < HEAD_DIM
    qscale = HEAD_DIM ** -0.5

    def proj_slabs(col, width):
        for c0 in range(0, width, MXU_COLS):
            z = (jnp.dot(h, w_ref[:, col + c0:col + c0 + MXU_COLS], preferred_element_type=F32)
                 + b_ref[:, col + c0:col + c0 + MXU_COLS])
            for half in range(MXU_COLS // LANES):
                yield c0 // LANES + half, z[:, half * LANES:(half + 1) * LANES]

    def dup_heads(slab):
        swapped = pltpu.roll(slab, HEAD_DIM, axis=1)
        return jnp.where(lo, slab, swapped), jnp.where(lo, swapped, slab)

    col = 0
    for k, z in proj_slabs(col, WIDTH_A):
        qa_ref[:, k * LANES:(k + 1) * LANES] = (_rope_slab(z, tables) * qscale).astype(BF16)
    col += WIDTH_A
    for k, z in proj_slabs(col, 2 * WIDTH_KV_A):
        dst = ka_ref if k == 0 else va_ref
        h0, h1 = dup_heads(_rope_slab(z, tables) if k == 0 else z)
        dst[:, 0:LANES] = h0.astype(BF16)
        dst[:, LANES:2 * LANES] = h1.astype(BF16)
    col += 2 * WIDTH_KV_A
    for k, z in proj_slabs(col, WIDTH_B):
        qb_ref[:, k * LANES:(k + 1) * LANES] = _rope_slab(z, tables) * qscale
    col += WIDTH_B
    for k, z in proj_slabs(col, WIDTH_B):
        kb_ref[:, k * LANES:(k + 1) * LANES] = _rope_slab(z, tables)
    col += WIDTH_B
    for k, z in proj_slabs(col, WIDTH_B):
        vb_ref[:, k * LANES:(k + 1) * LANES] = z


def _inproj_call(x, mod, g, w_in, b_in, pos, invf, *, sub, seq, cast_weights=()):
    t, d = x.shape
    tm = TM_PROJ
    tiles_per_batch = seq // tm
    row = lambda i: (i, 0)
    const = lambda i: (0, 0)
    return pl.pallas_call(
        functools.partial(_inproj_kernel, sub=sub, n_casts=len(cast_weights)),
        grid=(t // tm,),
        in_specs=[
            pl.BlockSpec((tm, d), row),
            pl.BlockSpec((1, N_SUB * N_MOD, d), lambda i: (i // tiles_per_batch, 0, 0)),
            pl.BlockSpec((1, d), const),
            pl.BlockSpec((d, IN_WIDTH), const, pipeline_mode=pl.Buffered(1)),
            pl.BlockSpec((1, IN_WIDTH), const),
            pl.BlockSpec((tm, 1), row),
            pl.BlockSpec((1, LANES), const),
        ] + _cast_specs(cast_weights, t // tm),
        out_specs=[
            pl.BlockSpec((tm, WIDTH_A), row),
            pl.BlockSpec((tm, 2 * LANES), row),
            pl.BlockSpec((tm, 2 * LANES), row),
            pl.BlockSpec((tm, WIDTH_B), row),
            pl.BlockSpec((tm, WIDTH_B), row),
            pl.BlockSpec((tm, WIDTH_B), row),
        ] + _cast_specs(cast_weights, t // tm),
        out_shape=[
            jax.ShapeDtypeStruct((t, WIDTH_A), BF16),
            jax.ShapeDtypeStruct((t, 2 * LANES), BF16),
            jax.ShapeDtypeStruct((t, 2 * LANES), BF16),
            jax.ShapeDtypeStruct((t, WIDTH_B), F32),
            jax.ShapeDtypeStruct((t, WIDTH_B), F32),
            jax.ShapeDtypeStruct((t, WIDTH_B), F32),
        ] + _cast_shapes(cast_weights),
        compiler_params=pltpu.CompilerParams(
            dimension_semantics=("arbitrary",), vmem_limit_bytes=VMEM_LIMIT),
        name="inproj_rope",
    )(x, mod, g.reshape(1, d), w_in, b_in.reshape(1, IN_WIDTH), pos, invf, *cast_weights)


def _attn_a_kernel(sink_ref, q_ref, kp_ref, kc_ref, vp_ref, vc_ref, o_ref, t_sc, es_sc):
    i = pl.program_id(1)
    lane = lax.broadcasted_iota(jnp.int32, (1, LANES), 1)
    lo = lane < HEAD_DIM
    qi = lax.broadcasted_iota(jnp.int32, (BLOCK, 2 * BLOCK), 0)
    kj = lax.broadcasted_iota(jnp.int32, (BLOCK, 2 * BLOCK), 1)
    dist = qi + BLOCK - kj
    band = (dist >= 0) & (dist <= WINDOW_A - 1)
    bias_inner = jnp.where(band, 0.0, NEG_BIG).astype(F32)
    bias_first = jnp.where(band & ((kj >= BLOCK) | (i > 0)), 0.0, NEG_BIG).astype(F32)
    zero = jnp.zeros((BLOCK, LANES), BF16)
    ones = jnp.ones((2 * BLOCK, LANES), BF16)
    pairs = GROUP_A // 2
    units = [(u, hk) for u in range(QBLOCKS_A) for hk in range(N_KV_A)]

    def keys_values(ref_prev, ref_cur, u, hk):
        r0 = u * BLOCK
        cols = slice(hk * LANES, (hk + 1) * LANES)
        prev = ref_prev[:, cols] if u == 0 else ref_cur[r0 - BLOCK:r0, cols]
        return jnp.concatenate([prev, ref_cur[r0:r0 + BLOCK, cols]], axis=0)

    def scores(n):
        u, hk = units[n]
        r0, slot = u * BLOCK, n % 2
        rows = []
        for jp in range(pairs):
            c0 = (hk * pairs + jp) * LANES
            qp = q_ref[r0:r0 + BLOCK, c0:c0 + LANES]
            rows.append(jnp.where(lo, qp, zero))
            rows.append(jnp.where(lo, zero, qp))
        lhs = jnp.concatenate(rows, axis=0)
        s_all = lax.dot_general(lhs, keys_values(kp_ref, kc_ref, u, hk), (((1,), (1,)), ((), ())),
                                preferred_element_type=F32)
        bias = bias_first if u == 0 else bias_inner
        for g in range(GROUP_A):
            sink = sink_ref[hk * GROUP_A + g]
            s = s_all[g * BLOCK:(g + 1) * BLOCK, :] + bias
            m = jnp.maximum(jnp.max(s, axis=-1, keepdims=True), sink)
            t_sc[slot, g * BLOCK:(g + 1) * BLOCK, :] = s - m
            es_sc[slot, g] = jnp.broadcast_to(jnp.exp(sink - m), (BLOCK, LANES))

    def values(n):
        u, hk = units[n]
        r0, slot = u * BLOCK, n % 2
        v3 = jnp.concatenate([keys_values(vp_ref, vc_ref, u, hk), ones], axis=1)
        pv = jnp.dot(jnp.exp(t_sc[slot]).astype(BF16), v3, preferred_element_type=F32)
        outs = []
        for g in range(GROUP_A):
            blk = pv[g * BLOCK:(g + 1) * BLOCK, :]
            outs.append(blk[:, :LANES] * (1.0 / (blk[:, LANES:] + es_sc[slot, g])))
        for jp in range(pairs):
            c0 = (hk * pairs + jp) * LANES
            o_ref[r0:r0 + BLOCK, c0:c0 + LANES] = jnp.where(lo, outs[2 * jp], outs[2 * jp + 1])

    scores(0)
    for n in range(len(units)):
        values(n)
        if n + 1 < len(units):
            scores(n + 1)


def _attn_a_call(sinks, qa, ka, va, *, batch, seq):
    nb = seq // BLOCK
    rows = QBLOCKS_A * BLOCK
    steps = nb // QBLOCKS_A
    cur = lambda b, i: (b * steps + i, 0)
    prev = lambda b, i: (b * nb + jnp.maximum(QBLOCKS_A * i - 1, 0), 0)
    return pl.pallas_call(
        _attn_a_kernel,
        grid=(batch, steps),
        in_specs=[
            pl.BlockSpec(memory_space=pltpu.SMEM),
            pl.BlockSpec((rows, WIDTH_A), cur),
            pl.BlockSpec((BLOCK, 2 * LANES), prev),
            pl.BlockSpec((rows, 2 * LANES), cur),
            pl.BlockSpec((BLOCK, 2 * LANES), prev),
            pl.BlockSpec((rows, 2 * LANES), cur),
        ],
        out_specs=pl.BlockSpec((rows, WIDTH_A), cur),
        out_shape=jax.ShapeDtypeStruct((batch * seq, WIDTH_A), F32),
        scratch_shapes=[pltpu.VMEM((2, GROUP_A * BLOCK, 2 * BLOCK), F32),
                        pltpu.VMEM((2, GROUP_A, BLOCK, LANES), F32)],
        compiler_params=pltpu.CompilerParams(
            dimension_semantics=("arbitrary", "arbitrary"), vmem_limit_bytes=VMEM_LIMIT),
        name="attn_swa_gqa",
    )(sinks, qa, ka, ka, va, va)


def _attn_b_kernel(q_ref, k_ref, v_ref, *rest, seq, n_casts):
    cast_in, rest = rest[:n_casts], rest[n_casts:]
    o_ref, cast_out = rest[0], rest[1:1 + n_casts]
    x4_ref, t_sc, m_sc, l_sc, acc_sc = rest[1 + n_casts:]
    _cast_blocks(cast_in, cast_out)
    (_, d1), (_, d2), (_, d3) = DILATED
    assert d1 == 1 and d3 == d2 * d2 and all(w // dil == BLOCK for w, dil in DILATED)
    n_iter = seq // BLOCK
    cls = seq // d2
    nblk2 = cls // BLOCK
    assert seq // d3 == BLOCK and n_iter == d2 * nblk2 == d3

    lane = lax.broadcasted_iota(jnp.int32, (1, LANES), 1)
    lo = lane < HEAD_DIM
    qi = lax.broadcasted_iota(jnp.int32, (BLOCK, BLOCK), 0)
    kj = lax.broadcasted_iota(jnp.int32, (BLOCK, BLOCK), 1)
    bias_cur = jnp.where(kj <= qi, 0.0, NEG_BIG).astype(F32)
    bias_prev = jnp.where(kj >= qi, 0.0, NEG_BIG).astype(F32)
    zero = jnp.zeros((BLOCK, LANES), BF16)
    ones = jnp.ones((2 * BLOCK, LANES), BF16)

    for a, ref in enumerate((q_ref, k_ref, v_ref)):
        for r in range(d2):
            x4_ref[a, r * cls:(r + 1) * cls, :] = ref[0, pl.ds(r, cls, stride=d2), :]

    natural = lambda a, rows: (q_ref, k_ref, v_ref)[a][0, rows, :]
    by_class = lambda a, rows: x4_ref[a, rows, :]

    def block_rows(cfg, idx):
        static = isinstance(idx, int)
        block = lambda start: pl.ds(start if static else pl.multiple_of(start, BLOCK), BLOCK)
        biggest = max if static else jnp.maximum
        start = idx * BLOCK
        own = block(start)
        if cfg == 0:
            return natural, own, block(biggest(start - BLOCK, 0)), idx > 0
        if cfg == 1:
            n = idx % nblk2
            return by_class, own, block(biggest(start - BLOCK, start - n * BLOCK)), n > 0
        return by_class, pl.ds((idx % d2) * cls + idx // d2, BLOCK, stride=d2), None, None

    def scores(cfg, idx, slot):
        load, own, prev, has_prev = block_rows(cfg, idx)
        qb = load(0, own).astype(BF16)
        lhs = jnp.concatenate([jnp.where(lo, qb, zero), jnp.where(lo, zero, qb)], axis=0)
        if prev is None:
            keys, bias = load(1, own), bias_cur
        else:
            keys = jnp.concatenate([load(1, prev), load(1, own)], axis=0)
            off = jnp.where(has_prev, 0.0, NEG_BIG).astype(F32)
            bias = jnp.concatenate([bias_prev + off, bias_cur], axis=1)
        nk = keys.shape[0]
        s = lax.dot_general(lhs, keys.astype(BF16), (((1,), (1,)), ((), ())),
                            preferred_element_type=F32)
        s0 = s[:BLOCK, :] + bias
        s1 = s[BLOCK:, :] + bias
        m0 = jnp.max(s0, axis=-1, keepdims=True)
        m1 = jnp.max(s1, axis=-1, keepdims=True)
        t_sc[cfg, slot, :BLOCK, :nk] = s0 - m0
        t_sc[cfg, slot, BLOCK:, :nk] = s1 - m1
        m_sc[cfg, own, :] = jnp.where(lo, m0, m1)

    def values(cfg, idx, slot):
        load, own, prev, _ = block_rows(cfg, idx)
        vals = load(2, own) if prev is None else jnp.concatenate([load(2, prev), load(2, own)], axis=0)
        nk = vals.shape[0]
        p = jnp.exp(t_sc[cfg, slot, :, :nk]).astype(BF16)
        v3 = jnp.concatenate([vals.astype(BF16), ones[:nk, :]], axis=1)
        pv = jnp.dot(p, v3, preferred_element_type=F32)
        acc_sc[cfg, own, :] = jnp.where(lo, pv[:BLOCK, :LANES], pv[BLOCK:, :LANES])
        l_sc[cfg, own, :] = jnp.where(lo, pv[:BLOCK, LANES:], pv[BLOCK:, LANES:])

    n_cfg = len(DILATED)
    for cfg in range(n_cfg):
        scores(cfg, 0, 0)

    def body(k, carry):
        for half in range(2):
            b = 2 * k + half
            for cfg in range(n_cfg):
                values(cfg, b, half)
                scores(cfg, b + 1, 1 - half)
        return carry

    assert n_iter % 2 == 0
    lax.fori_loop(0, n_iter // 2 - 1, body, 0, unroll=UNROLL_B)
    for cfg in range(n_cfg):
        values(cfg, n_iter - 2, 0)
        scores(cfg, n_iter - 1, 1)
    for cfg in range(n_cfg):
        values(cfg, n_iter - 1, 1)

    def merge(i, carry):
        start = i * BLOCK
        xr = pl.ds(pl.multiple_of(start, BLOCK), BLOCK)
        nat = pl.ds(i // nblk2 + (i % nblk2) * (BLOCK * d2), BLOCK, stride=d2)
        sel = (nat, xr, xr)
        ms = [m_sc[c, sel[c], :] for c in range(len(DILATED))]
        m_all = functools.reduce(jnp.maximum, ms)
        ws = [jnp.exp(m - m_all) for m in ms]
        l_all = sum(w * l_sc[c, sel[c], :] for c, w in enumerate(ws))
        acc_all = sum(w * acc_sc[c, sel[c], :] for c, w in enumerate(ws))
        o_ref[0, nat, :] = acc_all * (1.0 / l_all)
        return carry

    lax.fori_loop(0, n_iter, merge, 0, unroll=True)


def _attn_b_call(qb, kb, vb, *, batch, seq, cast_weights=()):
    spec = pl.BlockSpec((1, seq, LANES), lambda b, hp: (b, 0, hp))
    shape3 = (batch, seq, WIDTH_B)
    pairs = WIDTH_B // LANES
    cast_specs = _cast_specs(cast_weights, batch * pairs, lambda b, hp: b * pairs + hp)
    return pl.pallas_call(
        functools.partial(_attn_b_kernel, seq=seq, n_casts=len(cast_weights)),
        grid=(batch, pairs),
        in_specs=[spec, spec, spec] + cast_specs,
        out_specs=[spec] + cast_specs,
        out_shape=[jax.ShapeDtypeStruct(shape3, F32)] + _cast_shapes(cast_weights),
        scratch_shapes=[pltpu.VMEM((3, seq, LANES), F32),
                        pltpu.VMEM((len(DILATED), 2, 2 * BLOCK, 2 * BLOCK), F32)]
        + [pltpu.VMEM((len(DILATED), seq, LANES), F32)] * 3,
        compiler_params=pltpu.CompilerParams(
            dimension_semantics=("arbitrary", "arbitrary"), vmem_limit_bytes=VMEM_LIMIT),
        name="attn_dilated",
    )(qb.reshape(shape3), kb.reshape(shape3), vb.reshape(shape3), *cast_weights)


def _outproj_kernel(oa_ref, ob_ref, x_ref, mod_ref, ga_ref, gb_ref, w_ref, b_ref, *rest,
                    sub, n_casts):
    o_ref = rest[n_casts]
    _cast_blocks(rest[:n_casts], rest[n_casts + 1:])
    ya = _rms(oa_ref[...], ga_ref[...]).astype(BF16)
    yb = _rms(ob_ref[...], gb_ref[...]).astype(BF16)
    y = (jnp.dot(ya, w_ref[0:WIDTH_A, :], preferred_element_type=F32)
         + jnp.dot(yb, w_ref[WIDTH_A:WIDTH_A + WIDTH_B, :], preferred_element_type=F32)
         + b_ref[...])
    gmod = mod_ref[0, pl.ds(3 * sub + 2, 1), :]
    o_ref[...] = x_ref[...] + gmod * y


def _outproj_call(out_a, out_b, x, mod, g_a, g_b, w_out, b_out, *, sub, seq, cast_weights=()):
    t, d = x.shape
    tm = TM_PROJ
    tiles_per_batch = seq // tm
    row = lambda i: (i, 0)
    const = lambda i: (0, 0)
    return pl.pallas_call(
        functools.partial(_outproj_kernel, sub=sub, n_casts=len(cast_weights)),
        grid=(t // tm,),
        in_specs=[
            pl.BlockSpec((tm, WIDTH_A), row),
            pl.BlockSpec((tm, WIDTH_B), row),
            pl.BlockSpec((tm, d), row),
            pl.BlockSpec((1, N_SUB * N_MOD, d), lambda i: (i // tiles_per_batch, 0, 0)),
            pl.BlockSpec((1, WIDTH_A), const),
            pl.BlockSpec((1, WIDTH_B), const),
            pl.BlockSpec((WIDTH_A + WIDTH_B, d), const),
            pl.BlockSpec((1, d), const),
        ] + _cast_specs(cast_weights, t // tm),
        out_specs=[pl.BlockSpec((tm, d), row)] + _cast_specs(cast_weights, t // tm),
        out_shape=[jax.ShapeDtypeStruct((t, d), F32)] + _cast_shapes(cast_weights),
        compiler_params=pltpu.CompilerParams(
            dimension_semantics=("arbitrary",), vmem_limit_bytes=VMEM_LIMIT),
        name="outproj_residual",
    )(out_a, out_b, x, mod, g_a.reshape(1, WIDTH_A), g_b.reshape(1, WIDTH_B), w_out,
      b_out.reshape(1, d), *cast_weights)


def kernel(x, c, positions, w_ada, b_ada, g_ffn1, w_ffn1_in, w_ffn1_out, g_mix, w_in, b_in, sinks,
           g_out_a, g_out_b, w_out, b_out, g_ffn2, w_ffn2_in, w_ffn2_out, g_final):
    batch, seq, d = x.shape
    depth = w_ada.shape[0]
    t = batch * seq
    xt = x.reshape(t, d)
    pos = positions.reshape(t, 1)
    lane_dim = jnp.arange(LANES, dtype=jnp.int32) % (ROT_DIM // 2)
    invf = (ROPE_THETA ** (-(2.0 * lane_dim.astype(F32)) / ROT_DIM)).reshape(1, LANES)
    for layer in range(depth):
        mod = _ada_call(c, w_ada[layer], b_ada[layer]).reshape(batch, N_SUB * N_MOD, d)
        last = layer == depth - 1
        xt, w_in_bf = _ffn_call(
            xt, mod, g_ffn1[layer], w_ffn1_in[layer].astype(BF16), w_ffn1_out[layer].astype(BF16),
            g_final, sub=0, seq=seq, final_norm=False, cast_weights=(w_in[layer],))
        qa, ka, va, qb, kb, vb, w_out_bf, w_down2_bf = _inproj_call(
            xt, mod, g_mix[layer], w_in_bf, b_in[layer], pos, invf, sub=1, seq=seq,
            cast_weights=(w_out[layer], w_ffn2_out[layer]))
        out_a = _attn_a_call(sinks[layer], qa, ka, va, batch=batch, seq=seq)
        out_b, w_gu2_bf = _attn_b_call(qb, kb, vb, batch=batch, seq=seq,
                                       cast_weights=(w_ffn2_in[layer],))
        xt, = _outproj_call(out_a, out_b.reshape(t, WIDTH_B), xt, mod, g_out_a[layer],
                            g_out_b[layer], w_out_bf, b_out[layer], sub=1, seq=seq)
        xt, = _ffn_call(xt, mod, g_ffn2[layer], w_gu2_bf, w_down2_bf, g_final, sub=2, seq=seq,
                        final_norm=last)
    if depth == 0:
        raise ValueError("depth must be >= 1")
    return xt.reshape(batch, seq, d)
```

```python
import functools

import jax
import jax.numpy as jnp
from jax import lax
from jax.experimental import pallas as pl
from jax.experimental.pallas import tpu as pltpu

F32 = jnp.float32
BF16 = jnp.bfloat16

D_MODEL = 2048
HEAD_DIM = 64
N_HEADS_A = 16
N_KV_A = 2
GROUP_A = N_HEADS_A // N_KV_A
N_HEADS_B = 16
WIDTH_A = N_HEADS_A * HEAD_DIM
WIDTH_KV_A = N_KV_A * HEAD_DIM
WIDTH_B = N_HEADS_B * HEAD_DIM
IN_WIDTH = WIDTH_A + 2 * WIDTH_KV_A + 3 * WIDTH_B
WINDOW_A = 128
DILATED = ((128, 1), (512, 4), (2048, 16))
BLOCK = 128
ROPE_THETA = 500000.0
ROT_DIM = HEAD_DIM // 4
D_FF = 5632
FFN_RES = 0.5
N_SUB = 3
N_MOD = 3
EPS = 1e-5

LANES = 128
MXU_COLS = 256
NEG_BIG = -1e30
VMEM_LIMIT = 56 * 1024 * 1024

TM_FFN = 512
TF_FFN = 512
TM_PROJ = 512
TN_ADA = 1024
GROUP_CHUNKS = 2
BF16_ROWS = 16
QBLOCKS_A = 4
UNROLL_B = 7


def _silu(x):
    return x * (1.0 / (1.0 + jnp.exp(-x)))


def _rms(x, g):
    ms = jnp.mean(x * x, axis=-1, keepdims=True)
    return (x * lax.rsqrt(ms + EPS)) * g


def _modulated_norm(x, g_ref, mod_ref, sub):
    shift = mod_ref[0, pl.ds(3 * sub + 0, 1), :]
    scale = mod_ref[0, pl.ds(3 * sub + 1, 1), :]
    return (_rms(x, g_ref[...]) * (1.0 + scale) + shift).astype(BF16)


def _ada_kernel(c_ref, w_ref, b_ref, o_ref):
    cond = _silu(c_ref[...]).astype(BF16)
    o_ref[...] = jnp.dot(cond, w_ref[...].astype(BF16), preferred_element_type=F32) + b_ref[...]


def _ada_call(c, w_ada, b_ada):
    b, d = c.shape
    n = w_ada.shape[1]
    return pl.pallas_call(
        _ada_kernel,
        grid=(n // TN_ADA,),
        in_specs=[
            pl.BlockSpec((b, d), lambda j: (0, 0)),
            pl.BlockSpec((d, TN_ADA), lambda j: (0, j)),
            pl.BlockSpec((1, TN_ADA), lambda j: (0, j)),
        ],
        out_specs=pl.BlockSpec((b, TN_ADA), lambda j: (0, j)),
        out_shape=jax.ShapeDtypeStruct((b, n), F32),
        compiler_params=pltpu.CompilerParams(
            dimension_semantics=("arbitrary",), vmem_limit_bytes=VMEM_LIMIT),
        name="adaln_mod",
    )(c, w_ada, b_ada.reshape(1, n))


def _ffn_kernel(x_hbm, mod_ref, modn_ref, g_ref, wgu_hbm, wd_hbm, gfin_ref, *rest,
                sub, n_tiles, n_chunks, final_norm, n_casts):
    cast_in, rest = rest[:n_casts], rest[n_casts:]
    o_hbm, cast_out = rest[0], rest[1:1 + n_casts]
    (xbuf, h_ref, hn_ref, acc_ref, obuf, wg_buf, wu_buf, wd_buf,
     xsem, wsem, osem) = rest[1 + n_casts:]
    i = pl.program_id(0)
    tm = h_ref.shape[0]
    tf = wd_buf.shape[1] // GROUP_CHUNKS
    dff = n_chunks * tf
    cur = i % 2
    nxt = 1 - cur
    n_groups = -(-n_chunks // GROUP_CHUNKS)
    norm_groups = min(n_groups - 1, 4)
    norm_rows = tm // norm_groups
    assert n_groups % 2 == 0 and norm_rows * norm_groups == tm and norm_rows % BF16_ROWS == 0

    def x_copy(tile, slot):
        start = tile * tm if isinstance(tile, int) else pl.multiple_of(tile * tm, tm)
        return pltpu.make_async_copy(x_hbm.at[pl.ds(start, tm), :], xbuf.at[slot], xsem.at[slot])

    def out_copy(tile):
        return pltpu.make_async_copy(obuf, o_hbm.at[pl.ds(pl.multiple_of(tile * tm, tm), tm), :],
                                     osem.at[0])

    def w_copies(group):
        slot = group % 2
        col = group * GROUP_CHUNKS * tf
        width = min(GROUP_CHUNKS, n_chunks - group * GROUP_CHUNKS) * tf
        return (
            pltpu.make_async_copy(wgu_hbm.at[:, pl.ds(col, width)],
                                  wg_buf.at[slot, :, pl.ds(0, width)], wsem.at[slot]),
            pltpu.make_async_copy(wgu_hbm.at[:, pl.ds(dff + col, width)],
                                  wu_buf.at[slot, :, pl.ds(0, width)], wsem.at[2 + slot]),
            pltpu.make_async_copy(wd_hbm.at[pl.ds(col, width), :],
                                  wd_buf.at[slot, pl.ds(0, width), :], wsem.at[4 + slot]),
        )

    @pl.when(i == 0)
    def _():
        x_copy(0, 0).start()
        for c in w_copies(0):
            c.start()
        x_copy(0, 0).wait()
        h_ref[...] = _modulated_norm(xbuf[0], g_ref, mod_ref, sub)

    @pl.when(i > 0)
    def _():
        h_ref[...] = hn_ref[...]

    next_tile = jnp.minimum(i + 1, n_tiles - 1)
    x_copy(next_tile, nxt).start()
    _cast_blocks(cast_in, cast_out)

    for group in range(n_groups):
        slot = group % 2
        for c in w_copies(group):
            c.wait()
        for c in w_copies((group + 1) % n_groups):
            c.start()
        if group == n_groups - norm_groups:
            x_copy(next_tile, nxt).wait()
        for k in range(min(GROUP_CHUNKS, n_chunks - group * GROUP_CHUNKS)):
            cols = slice(k * tf, (k + 1) * tf)
            h = h_ref[...]
            gate = jnp.dot(h, wg_buf[slot, :, cols], preferred_element_type=F32)
            up = jnp.dot(h, wu_buf[slot, :, cols], preferred_element_type=F32)
            act = (_silu(gate) * up).astype(BF16)
            down = jnp.dot(act, wd_buf[slot, cols, :], preferred_element_type=F32)
            if group == 0 and k == 0:
                acc_ref[...] = down
            else:
                acc_ref[...] += down
        if group >= n_groups - norm_groups:
            rows = pl.ds((group - (n_groups - norm_groups)) * norm_rows, norm_rows)
            hn_ref[rows, :] = _modulated_norm(xbuf[nxt, rows, :], g_ref, modn_ref, sub)

    @pl.when(i == n_tiles - 1)
    def _():
        for c in w_copies(0):
            c.wait()

    @pl.when(i > 0)
    def _():
        out_copy(i - 1).wait()

    gmod = mod_ref[0, pl.ds(3 * sub + 2, 1), :]
    out = xbuf[cur] + (FFN_RES * gmod) * acc_ref[...]
    if final_norm:
        out = _rms(out, gfin_ref[...])
    obuf[...] = out
    out_copy(i).start()

    @pl.when(i == n_tiles - 1)
    def _():
        out_copy(i).wait()


def _cast_specs(weights, n_steps, step=lambda i: i):
    specs = []
    for w in weights:
        rows, cols = w.shape
        assert rows % (n_steps * BF16_ROWS) == 0
        specs.append(pl.BlockSpec((rows // n_steps, cols), lambda *ids: (step(*ids), 0)))
    return specs


def _cast_shapes(weights):
    return [jax.ShapeDtypeStruct(w.shape, BF16) for w in weights]


def _cast_blocks(cast_in, cast_out):
    for src, dst in zip(cast_in, cast_out):
        dst[...] = src[...].astype(BF16)


def _ffn_call(x, mod, g, w_gu, w_down, g_final, *, sub, seq, final_norm, cast_weights=()):
    t, d = x.shape
    dff = w_down.shape[0]
    tm, tf = TM_FFN, TF_FFN
    n_tiles, n_chunks = t // tm, dff // tf
    tiles_per_batch = seq // tm
    group = GROUP_CHUNKS * tf
    nxt = lambda i: jnp.minimum(i + 1, n_tiles - 1)
    const = lambda i: (0, 0)
    kern = functools.partial(_ffn_kernel, sub=sub, n_tiles=n_tiles, n_chunks=n_chunks,
                             final_norm=final_norm, n_casts=len(cast_weights))
    return pl.pallas_call(
        kern,
        grid=(n_tiles,),
        in_specs=[
            pl.BlockSpec(memory_space=pl.ANY),
            pl.BlockSpec((1, N_SUB * N_MOD, d), lambda i: (i // tiles_per_batch, 0, 0)),
            pl.BlockSpec((1, N_SUB * N_MOD, d), lambda i: (nxt(i) // tiles_per_batch, 0, 0)),
            pl.BlockSpec((1, d), const),
            pl.BlockSpec(memory_space=pl.ANY),
            pl.BlockSpec(memory_space=pl.ANY),
            pl.BlockSpec((1, d), const),
        ] + _cast_specs(cast_weights, n_tiles),
        out_specs=[pl.BlockSpec(memory_space=pl.ANY)] + _cast_specs(cast_weights, n_tiles),
        out_shape=[jax.ShapeDtypeStruct((t, d), F32)] + _cast_shapes(cast_weights),
        scratch_shapes=[
            pltpu.VMEM((2, tm, d), F32),
            pltpu.VMEM((tm, d), BF16),
            pltpu.VMEM((tm, d), BF16),
            pltpu.VMEM((tm, d), F32),
            pltpu.VMEM((tm, d), F32),
            pltpu.VMEM((2, d, group), BF16),
            pltpu.VMEM((2, d, group), BF16),
            pltpu.VMEM((2, group, d), BF16),
            pltpu.SemaphoreType.DMA((2,)),
            pltpu.SemaphoreType.DMA((6,)),
            pltpu.SemaphoreType.DMA((1,)),
        ],
        compiler_params=pltpu.CompilerParams(
            dimension_semantics=("arbitrary",), vmem_limit_bytes=VMEM_LIMIT),
        name="ffn_sub%d" % sub,
    )(x, mod, mod, g.reshape(1, d), w_gu, w_down, g_final.reshape(1, d), *cast_weights)


def _packed_positions(positions):
    per_row = LANES // ROT_DIM
    return jnp.repeat(positions.reshape(-1, per_row), ROT_DIM, axis=1)


def _rope_tables(pos_ref, invf_ref, unpack_ref):
    per_row = LANES // ROT_DIM
    lane = lax.broadcasted_iota(jnp.int32, (1, LANES), 1)
    d = lane % HEAD_DIM
    ang = pos_ref[...].astype(F32) * invf_ref[...]
    n_packed = ang.shape[0]
    for t, packed in enumerate((jnp.cos(ang), jnp.sin(ang))):
        for u in range(per_row):
            shifted = packed if u == 0 else pltpu.roll(packed, LANES - ROT_DIM * u, 1)
            unpack_ref[t, pl.ds(u, n_packed, stride=per_row), :] = shifted

    def both_heads(a):
        a = jnp.where(lane < ROT_DIM, a, 0.0)
        return a + pltpu.roll(a, HEAD_DIM, 1)

    c, s = both_heads(unpack_ref[0]), both_heads(unpack_ref[1])
    half = ROT_DIM // 2
    coef_self = jnp.where(d < ROT_DIM, c, 1.0)
    coef_next = jnp.where(d < half, -s, 0.0)
    coef_prev = jnp.where((d >= half) & (d < ROT_DIM), s, 0.0)
    return coef_self, coef_next, coef_prev


def _rope_slab(z, tables):
    coef_self, coef_next, coef_prev = tables
    half = ROT_DIM // 2
    nxt = pltpu.roll(z, LANES - half, axis=1)
    prv = pltpu.roll(z, half, axis=1)
    return z * coef_self + nxt * coef_next + prv * coef_prev


def _inproj_kernel(x_ref, mod_ref, g_ref, w_ref, b_ref, pos_ref, invf_ref, *rest, sub, n_casts):
    cast_in, rest = rest[:n_casts], rest[n_casts:]
    qa_ref, ka_ref, va_ref, qb_ref, kb_ref, vb_ref = rest[:6]
    _cast_blocks(cast_in, rest[6:6 + n_casts])
    h = _modulated_norm(x_ref[...], g_ref, mod_ref, sub)
    tables = _rope_tables(pos_ref, invf_ref, rest[6 + n_casts])
    lane = lax.broadcasted_iota(jnp.int32, (1, LANES), 1)
    lo = lane < HEAD_DIM
    qscale = HEAD_DIM ** -0.5

    def proj_slabs(col, width):
        for c0 in range(0, width, MXU_COLS):
            z = (jnp.dot(h, w_ref[:, col + c0:col + c0 + MXU_COLS], preferred_element_type=F32)
                 + b_ref[:, col + c0:col + c0 + MXU_COLS])
            for half in range(MXU_COLS // LANES):
                yield c0 // LANES + half, z[:, half * LANES:(half + 1) * LANES]

    def dup_heads(slab):
        swapped = pltpu.roll(slab, HEAD_DIM, axis=1)
        return jnp.where(lo, slab, swapped), jnp.where(lo, swapped, slab)

    col = 0
    for k, z in proj_slabs(col, WIDTH_A):
        qa_ref[:, k * LANES:(k + 1) * LANES] = (_rope_slab(z, tables) * qscale).astype(BF16)
    col += WIDTH_A
    for k, z in proj_slabs(col, 2 * WIDTH_KV_A):
        dst = ka_ref if k == 0 else va_ref
        h0, h1 = dup_heads(_rope_slab(z, tables) if k == 0 else z)
        dst[:, 0:LANES] = h0.astype(BF16)
        dst[:, LANES:2 * LANES] = h1.astype(BF16)
    col += 2 * WIDTH_KV_A
    for k, z in proj_slabs(col, WIDTH_B):
        qb_ref[:, k * LANES:(k + 1) * LANES] = _rope_slab(z, tables) * qscale
    col += WIDTH_B
    for k, z in proj_slabs(col, WIDTH_B):
        kb_ref[:, k * LANES:(k + 1) * LANES] = _rope_slab(z, tables)
    col += WIDTH_B
    for k, z in proj_slabs(col, WIDTH_B):
        vb_ref[:, k * LANES:(k + 1) * LANES] = z


def _inproj_call(x, mod, g, w_in, b_in, pos, invf, *, sub, seq, cast_weights=()):
    t, d = x.shape
    tm = TM_PROJ
    tiles_per_batch = seq // tm
    row = lambda i: (i, 0)
    const = lambda i: (0, 0)
    return pl.pallas_call(
        functools.partial(_inproj_kernel, sub=sub, n_casts=len(cast_weights)),
        grid=(t // tm,),
        in_specs=[
            pl.BlockSpec((tm, d), row),
            pl.BlockSpec((1, N_SUB * N_MOD, d), lambda i: (i // tiles_per_batch, 0, 0)),
            pl.BlockSpec((1, d), const),
            pl.BlockSpec((d, IN_WIDTH), const, pipeline_mode=pl.Buffered(1)),
            pl.BlockSpec((1, IN_WIDTH), const),
            pl.BlockSpec((tm * ROT_DIM // LANES, LANES), row),
            pl.BlockSpec((1, LANES), const),
        ] + _cast_specs(cast_weights, t // tm),
        out_specs=[
            pl.BlockSpec((tm, WIDTH_A), row),
            pl.BlockSpec((tm, 2 * LANES), row),
            pl.BlockSpec((tm, 2 * LANES), row),
            pl.BlockSpec((tm, WIDTH_B), row),
            pl.BlockSpec((tm, WIDTH_B), row),
            pl.BlockSpec((tm, WIDTH_B), row),
        ] + _cast_specs(cast_weights, t // tm),
        out_shape=[
            jax.ShapeDtypeStruct((t, WIDTH_A), BF16),
            jax.ShapeDtypeStruct((t, 2 * LANES), BF16),
            jax.ShapeDtypeStruct((t, 2 * LANES), BF16),
            jax.ShapeDtypeStruct((t, WIDTH_B), F32),
            jax.ShapeDtypeStruct((t, WIDTH_B), F32),
            jax.ShapeDtypeStruct((t, WIDTH_B), F32),
        ] + _cast_shapes(cast_weights),
        scratch_shapes=[pltpu.VMEM((2, tm, LANES), F32)],
        compiler_params=pltpu.CompilerParams(
            dimension_semantics=("arbitrary",), vmem_limit_bytes=VMEM_LIMIT),
        name="inproj_rope",
    )(x, mod, g.reshape(1, d), w_in, b_in.reshape(1, IN_WIDTH), pos, invf, *cast_weights)


def _attn_a_kernel(sink_ref, q_ref, kp_ref, kc_ref, vp_ref, vc_ref, o_ref, t_sc, es_sc):
    i = pl.program_id(1)
    lane = lax.broadcasted_iota(jnp.int32, (1, LANES), 1)
    lo = lane < HEAD_DIM
    qi = lax.broadcasted_iota(jnp.int32, (BLOCK, 2 * BLOCK), 0)
    kj = lax.broadcasted_iota(jnp.int32, (BLOCK, 2 * BLOCK), 1)
    dist = qi + BLOCK - kj
    band = (dist >= 0) & (dist <= WINDOW_A - 1)
    bias_inner = jnp.where(band, 0.0, NEG_BIG).astype(F32)
    bias_first = jnp.where(band & ((kj >= BLOCK) | (i > 0)), 0.0, NEG_BIG).astype(F32)
    zero = jnp.zeros((BLOCK, LANES), BF16)
    ones = jnp.ones((2 * BLOCK, LANES), BF16)
    pairs = GROUP_A // 2
    units = [(u, hk) for u in range(QBLOCKS_A) for hk in range(N_KV_A)]

    def keys_values(ref_prev, ref_cur, u, hk):
        r0 = u * BLOCK
        cols = slice(hk * LANES, (hk + 1) * LANES)
        prev = ref_prev[:, cols] if u == 0 else ref_cur[r0 - BLOCK:r0, cols]
        return jnp.concatenate([prev, ref_cur[r0:r0 + BLOCK, cols]], axis=0)

    def scores(n):
        u, hk = units[n]
        r0, slot = u * BLOCK, n % 2
        rows = []
        for jp in range(pairs):
            c0 = (hk * pairs + jp) * LANES
            qp = q_ref[r0:r0 + BLOCK, c0:c0 + LANES]
            rows.append(jnp.where(lo, qp, zero))
            rows.append(jnp.where(lo, zero, qp))
        lhs = jnp.concatenate(rows, axis=0)
        s_all = lax.dot_general(lhs, keys_values(kp_ref, kc_ref, u, hk), (((1,), (1,)), ((), ())),
                                preferred_element_type=F32)
        bias = bias_first if u == 0 else bias_inner
        for g in range(GROUP_A):
            sink = sink_ref[hk * GROUP_A + g]
            s = s_all[g * BLOCK:(g + 1) * BLOCK, :] + bias
            m = jnp.maximum(jnp.max(s, axis=-1, keepdims=True), sink)
            t_sc[slot, g * BLOCK:(g + 1) * BLOCK, :] = s - m
            es_sc[slot, g] = jnp.broadcast_to(jnp.exp(sink - m), (BLOCK, LANES))

    def values(n):
        u, hk = units[n]
        r0, slot = u * BLOCK, n % 2
        v3 = jnp.concatenate([keys_values(vp_ref, vc_ref, u, hk), ones], axis=1)
        pv = jnp.dot(jnp.exp(t_sc[slot]).astype(BF16), v3, preferred_element_type=F32)
        outs = []
        for g in range(GROUP_A):
            blk = pv[g * BLOCK:(g + 1) * BLOCK, :]
            outs.append(blk[:, :LANES] * (1.0 / (blk[:, LANES:] + es_sc[slot, g])))
        for jp in range(pairs):
            c0 = (hk * pairs + jp) * LANES
            o_ref[r0:r0 + BLOCK, c0:c0 + LANES] = jnp.where(lo, outs[2 * jp], outs[2 * jp + 1])

    scores(0)
    for n in range(len(units)):
        values(n)
        if n + 1 < len(units):
            scores(n + 1)


def _attn_a_call(sinks, qa, ka, va, *, batch, seq):
    nb = seq // BLOCK
    rows = QBLOCKS_A * BLOCK
    steps = nb // QBLOCKS_A
    cur = lambda b, i: (b * steps + i, 0)
    prev = lambda b, i: (b * nb + jnp.maximum(QBLOCKS_A * i - 1, 0), 0)
    return pl.pallas_call(
        _attn_a_kernel,
        grid=(batch, steps),
        in_specs=[
            pl.BlockSpec(memory_space=pltpu.SMEM),
            pl.BlockSpec((rows, WIDTH_A), cur),
            pl.BlockSpec((BLOCK, 2 * LANES), prev),
            pl.BlockSpec((rows, 2 * LANES), cur),
            pl.BlockSpec((BLOCK, 2 * LANES), prev),
            pl.BlockSpec((rows, 2 * LANES), cur),
        ],
        out_specs=pl.BlockSpec((rows, WIDTH_A), cur),
        out_shape=jax.ShapeDtypeStruct((batch * seq, WIDTH_A), F32),
        scratch_shapes=[pltpu.VMEM((2, GROUP_A * BLOCK, 2 * BLOCK), F32),
                        pltpu.VMEM((2, GROUP_A, BLOCK, LANES), F32)],
        compiler_params=pltpu.CompilerParams(
            dimension_semantics=("arbitrary", "arbitrary"), vmem_limit_bytes=VMEM_LIMIT),
        name="attn_swa_gqa",
    )(sinks, qa, ka, ka, va, va)


def _attn_b_kernel(q_ref, k_ref, v_ref, *rest, seq, n_casts):
    cast_in, rest = rest[:n_casts], rest[n_casts:]
    o_ref, cast_out = rest[0], rest[1:1 + n_casts]
    x4_ref, t_sc, m_sc, l_sc, acc_sc = rest[1 + n_casts:]
    _cast_blocks(cast_in, cast_out)
    (_, d1), (_, d2), (_, d3) = DILATED
    assert d1 == 1 and d3 == d2 * d2 and all(w // dil == BLOCK for w, dil in DILATED)
    n_iter = seq // BLOCK
    cls = seq // d2
    nblk2 = cls // BLOCK
    assert seq // d3 == BLOCK and n_iter == d2 * nblk2 == d3

    lane = lax.broadcasted_iota(jnp.int32, (1, LANES), 1)
    lo = lane < HEAD_DIM
    qi = lax.broadcasted_iota(jnp.int32, (BLOCK, BLOCK), 0)
    kj = lax.broadcasted_iota(jnp.int32, (BLOCK, BLOCK), 1)
    bias_cur = jnp.where(kj <= qi, 0.0, NEG_BIG).astype(F32)
    bias_prev = jnp.where(kj >= qi, 0.0, NEG_BIG).astype(F32)
    zero = jnp.zeros((BLOCK, LANES), BF16)
    ones = jnp.ones((2 * BLOCK, LANES), BF16)

    for a, ref in enumerate((q_ref, k_ref, v_ref)):
        for r in range(d2):
            x4_ref[a, r * cls:(r + 1) * cls, :] = ref[0, pl.ds(r, cls, stride=d2), :]

    natural = lambda a, rows: (q_ref, k_ref, v_ref)[a][0, rows, :]
    by_class = lambda a, rows: x4_ref[a, rows, :]

    def block_rows(cfg, idx):
        static = isinstance(idx, int)
        block = lambda start: pl.ds(start if static else pl.multiple_of(start, BLOCK), BLOCK)
        biggest = max if static else jnp.maximum
        start = idx * BLOCK
        own = block(start)
        if cfg == 0:
            return natural, own, block(biggest(start - BLOCK, 0)), idx > 0
        if cfg == 1:
            n = idx % nblk2
            return by_class, own, block(biggest(start - BLOCK, start - n * BLOCK)), n > 0
        return by_class, pl.ds((idx % d2) * cls + idx // d2, BLOCK, stride=d2), None, None

    def scores(cfg, idx, slot):
        load, own, prev, has_prev = block_rows(cfg, idx)
        qb = load(0, own).astype(BF16)
        lhs = jnp.concatenate([jnp.where(lo, qb, zero), jnp.where(lo, zero, qb)], axis=0)
        if prev is None:
            keys, bias = load(1, own), bias_cur
        else:
            keys = jnp.concatenate([load(1, prev), load(1, own)], axis=0)
            off = jnp.where(has_prev, 0.0, NEG_BIG).astype(F32)
            bias = jnp.concatenate([bias_prev + off, bias_cur], axis=1)
        nk = keys.shape[0]
        s = lax.dot_general(lhs, keys.astype(BF16), (((1,), (1,)), ((), ())),
                            preferred_element_type=F32)
        s0 = s[:BLOCK, :] + bias
        s1 = s[BLOCK:, :] + bias
        m0 = jnp.max(s0, axis=-1, keepdims=True)
        m1 = jnp.max(s1, axis=-1, keepdims=True)
        t_sc[cfg, slot, :BLOCK, :nk] = s0 - m0
        t_sc[cfg, slot, BLOCK:, :nk] = s1 - m1
        m_sc[cfg, own, :] = jnp.where(lo, m0, m1)

    def values(cfg, idx, slot):
        load, own, prev, _ = block_rows(cfg, idx)
        vals = load(2, own) if prev is None else jnp.concatenate([load(2, prev), load(2, own)], axis=0)
        nk = vals.shape[0]
        p = jnp.exp(t_sc[cfg, slot, :, :nk]).astype(BF16)
        v3 = jnp.concatenate([vals.astype(BF16), ones[:nk, :]], axis=1)
        pv = jnp.dot(p, v3, preferred_element_type=F32)
        acc_sc[cfg, own, :] = jnp.where(lo, pv[:BLOCK, :LANES], pv[BLOCK:, :LANES])
        l_sc[cfg, own, :] = jnp.where(lo, pv[:BLOCK, LANES:], pv[BLOCK:, LANES:])

    n_cfg = len(DILATED)
    for cfg in range(n_cfg):
        scores(cfg, 0, 0)

    def body(k, carry):
        for half in range(2):
            b = 2 * k + half
            for cfg in range(n_cfg):
                values(cfg, b, half)
                scores(cfg, b + 1, 1 - half)
        return carry

    assert n_iter % 2 == 0
    lax.fori_loop(0, n_iter // 2 - 1, body, 0, unroll=UNROLL_B)
    for cfg in range(n_cfg):
        values(cfg, n_iter - 2, 0)
        scores(cfg, n_iter - 1, 1)
    for cfg in range(n_cfg):
        values(cfg, n_iter - 1, 1)

    def merge(i, carry):
        start = i * BLOCK
        xr = pl.ds(pl.multiple_of(start, BLOCK), BLOCK)
        nat = pl.ds(i // nblk2 + (i % nblk2) * (BLOCK * d2), BLOCK, stride=d2)
        sel = (nat, xr, xr)
        ms = [m_sc[c, sel[c], :] for c in range(len(DILATED))]
        m_all = functools.reduce(jnp.maximum, ms)
        ws = [jnp.exp(m - m_all) for m in ms]
        l_all = sum(w * l_sc[c, sel[c], :] for c, w in enumerate(ws))
        acc_all = sum(w * acc_sc[c, sel[c], :] for c, w in enumerate(ws))
        o_ref[0, nat, :] = acc_all * (1.0 / l_all)
        return carry

    lax.fori_loop(0, n_iter, merge, 0, unroll=True)


def _attn_b_call(qb, kb, vb, *, batch, seq, cast_weights=()):
    spec = pl.BlockSpec((1, seq, LANES), lambda b, hp: (b, 0, hp))
    shape3 = (batch, seq, WIDTH_B)
    pairs = WIDTH_B // LANES
    cast_specs = _cast_specs(cast_weights, batch * pairs, lambda b, hp: b * pairs + hp)
    return pl.pallas_call(
        functools.partial(_attn_b_kernel, seq=seq, n_casts=len(cast_weights)),
        grid=(batch, pairs),
        in_specs=[spec, spec, spec] + cast_specs,
        out_specs=[spec] + cast_specs,
        out_shape=[jax.ShapeDtypeStruct(shape3, F32)] + _cast_shapes(cast_weights),
        scratch_shapes=[pltpu.VMEM((3, seq, LANES), F32),
                        pltpu.VMEM((len(DILATED), 2, 2 * BLOCK, 2 * BLOCK), F32)]
        + [pltpu.VMEM((len(DILATED), seq, LANES), F32)] * 3,
        compiler_params=pltpu.CompilerParams(
            dimension_semantics=("arbitrary", "arbitrary"), vmem_limit_bytes=VMEM_LIMIT),
        name="attn_dilated",
    )(qb.reshape(shape3), kb.reshape(shape3), vb.reshape(shape3), *cast_weights)


def _outproj_kernel(oa_ref, ob_ref, x_ref, mod_ref, ga_ref, gb_ref, w_ref, b_ref, *rest,
                    sub, n_casts):
    o_ref = rest[n_casts]
    _cast_blocks(rest[:n_casts], rest[n_casts + 1:])
    ya = _rms(oa_ref[...], ga_ref[...]).astype(BF16)
    yb = _rms(ob_ref[...], gb_ref[...]).astype(BF16)
    y = (jnp.dot(ya, w_ref[0:WIDTH_A, :], preferred_element_type=F32)
         + jnp.dot(yb, w_ref[WIDTH_A:WIDTH_A + WIDTH_B, :], preferred_element_type=F32)
         + b_ref[...])
    gmod = mod_ref[0, pl.ds(3 * sub + 2, 1), :]
    o_ref[...] = x_ref[...] + gmod * y


def _outproj_call(out_a, out_b, x, mod, g_a, g_b, w_out, b_out, *, sub, seq, cast_weights=()):
    t, d = x.shape
    tm = TM_PROJ
    tiles_per_batch = seq // tm
    row = lambda i: (i, 0)
    const = lambda i: (0, 0)
    return pl.pallas_call(
        functools.partial(_outproj_kernel, sub=sub, n_casts=len(cast_weights)),
        grid=(t // tm,),
        in_specs=[
            pl.BlockSpec((tm, WIDTH_A), row),
            pl.BlockSpec((tm, WIDTH_B), row),
            pl.BlockSpec((tm, d), row),
            pl.BlockSpec((1, N_SUB * N_MOD, d), lambda i: (i // tiles_per_batch, 0, 0)),
            pl.BlockSpec((1, WIDTH_A), const),
            pl.BlockSpec((1, WIDTH_B), const),
            pl.BlockSpec((WIDTH_A + WIDTH_B, d), const),
            pl.BlockSpec((1, d), const),
        ] + _cast_specs(cast_weights, t // tm),
        out_specs=[pl.BlockSpec((tm, d), row)] + _cast_specs(cast_weights, t // tm),
        out_shape=[jax.ShapeDtypeStruct((t, d), F32)] + _cast_shapes(cast_weights),
        compiler_params=pltpu.CompilerParams(
            dimension_semantics=("arbitrary",), vmem_limit_bytes=VMEM_LIMIT),
        name="outproj_residual",
    )(out_a, out_b, x, mod, g_a.reshape(1, WIDTH_A), g_b.reshape(1, WIDTH_B), w_out,
      b_out.reshape(1, d), *cast_weights)


def kernel(x, c, positions, w_ada, b_ada, g_ffn1, w_ffn1_in, w_ffn1_out, g_mix, w_in, b_in, sinks,
           g_out_a, g_out_b, w_out, b_out, g_ffn2, w_ffn2_in, w_ffn2_out, g_final):
    batch, seq, d = x.shape
    depth = w_ada.shape[0]
    t = batch * seq
    xt = x.reshape(t, d)
    pos = _packed_positions(positions.reshape(t))
    lane_dim = jnp.arange(LANES, dtype=jnp.int32) % (ROT_DIM // 2)
    invf = (ROPE_THETA ** (-(2.0 * lane_dim.astype(F32)) / ROT_DIM)).reshape(1, LANES)
    for layer in range(depth):
        mod = _ada_call(c, w_ada[layer], b_ada[layer]).reshape(batch, N_SUB * N_MOD, d)
        last = layer == depth - 1
        xt, w_in_bf = _ffn_call(
            xt, mod, g_ffn1[layer], w_ffn1_in[layer].astype(BF16), w_ffn1_out[layer].astype(BF16),
            g_final, sub=0, seq=seq, final_norm=False, cast_weights=(w_in[layer],))
        qa, ka, va, qb, kb, vb, w_out_bf, w_down2_bf = _inproj_call(
            xt, mod, g_mix[layer], w_in_bf, b_in[layer], pos, invf, sub=1, seq=seq,
            cast_weights=(w_out[layer], w_ffn2_out[layer]))
        out_a = _attn_a_call(sinks[layer], qa, ka, va, batch=batch, seq=seq)
        out_b, w_gu2_bf = _attn_b_call(qb, kb, vb, batch=batch, seq=seq,
                                       cast_weights=(w_ffn2_in[layer],))
        xt, = _outproj_call(out_a, out_b.reshape(t, WIDTH_B), xt, mod, g_out_a[layer],
                            g_out_b[layer], w_out_bf, b_out[layer], sub=1, seq=seq)
        xt, = _ffn_call(xt, mod, g_ffn2[layer], w_gu2_bf, w_down2_bf, g_final, sub=2, seq=seq,
                        final_norm=last)
    if depth == 0:
        raise ValueError("depth must be >= 1")
    return xt.reshape(batch, seq, d)
```

```python
import functools

import jax
import jax.numpy as jnp
from jax import lax
from jax.experimental import pallas as pl
from jax.experimental.pallas import tpu as pltpu

F32 = jnp.float32
BF16 = jnp.bfloat16

D_MODEL = 2048
HEAD_DIM = 64
N_HEADS_A = 16
N_KV_A = 2
GROUP_A = N_HEADS_A // N_KV_A
N_HEADS_B = 16
WIDTH_A = N_HEADS_A * HEAD_DIM
WIDTH_KV_A = N_KV_A * HEAD_DIM
WIDTH_B = N_HEADS_B * HEAD_DIM
IN_WIDTH = WIDTH_A + 2 * WIDTH_KV_A + 3 * WIDTH_B
WINDOW_A = 128
DILATED = ((128, 1), (512, 4), (2048, 16))
BLOCK = 128
ROPE_THETA = 500000.0
ROT_DIM = HEAD_DIM // 4
D_FF = 5632
FFN_RES = 0.5
N_SUB = 3
N_MOD = 3
EPS = 1e-5

LANES = 128
MXU_COLS = 256
NEG_BIG = -1e30
VMEM_LIMIT = 56 * 1024 * 1024

TM_FFN = 512
TF_FFN = 512
TM_PROJ = 512
TN_ADA = 2048
GROUP_CHUNKS = 2
BF16_ROWS = 16
QBLOCKS_A = 8
UNROLL_B = 7


def _silu(x):
    return x * (1.0 / (1.0 + jnp.exp(-x)))


def _rms(x, g):
    ms = jnp.mean(x * x, axis=-1, keepdims=True)
    return (x * lax.rsqrt(ms + EPS)) * g


def _modulated_norm(x, g_ref, mod_ref, sub):
    shift = mod_ref[0, pl.ds(3 * sub + 0, 1), :]
    scale = mod_ref[0, pl.ds(3 * sub + 1, 1), :]
    return (_rms(x, g_ref[...]) * (1.0 + scale) + shift).astype(BF16)


def _ada_kernel(c_ref, w_ref, b_ref, o_ref):
    cond = _silu(c_ref[...]).astype(BF16)
    o_ref[...] = jnp.dot(cond, w_ref[...].astype(BF16), preferred_element_type=F32) + b_ref[...]


def _ada_call(c, w_ada, b_ada):
    b, d = c.shape
    n = w_ada.shape[1]
    return pl.pallas_call(
        _ada_kernel,
        grid=(n // TN_ADA,),
        in_specs=[
            pl.BlockSpec((b, d), lambda j: (0, 0)),
            pl.BlockSpec((d, TN_ADA), lambda j: (0, j)),
            pl.BlockSpec((1, TN_ADA), lambda j: (0, j)),
        ],
        out_specs=pl.BlockSpec((b, TN_ADA), lambda j: (0, j)),
        out_shape=jax.ShapeDtypeStruct((b, n), F32),
        compiler_params=pltpu.CompilerParams(
            dimension_semantics=("arbitrary",), vmem_limit_bytes=VMEM_LIMIT),
        name="adaln_mod",
    )(c, w_ada, b_ada.reshape(1, n))


def _ffn_kernel(x_hbm, mod_ref, modn_ref, g_ref, wgu_hbm, wd_hbm, gfin_ref, *rest,
                sub, n_tiles, n_chunks, final_norm, n_casts):
    cast_in, rest = rest[:n_casts], rest[n_casts:]
    o_hbm, cast_out = rest[0], rest[1:1 + n_casts]
    (xbuf, h_ref, hn_ref, acc_ref, obuf, wg_buf, wu_buf, wd_buf,
     xsem, wsem, osem) = rest[1 + n_casts:]
    i = pl.program_id(0)
    tm = h_ref.shape[0]
    tf = wd_buf.shape[1] // GROUP_CHUNKS
    dff = n_chunks * tf
    cur = i % 2
    nxt = 1 - cur
    n_groups = -(-n_chunks // GROUP_CHUNKS)
    norm_groups = min(n_groups - 1, 4)
    norm_rows = tm // norm_groups
    assert n_groups % 2 == 0 and norm_rows * norm_groups == tm and norm_rows % BF16_ROWS == 0

    def x_copy(tile, slot):
        start = tile * tm if isinstance(tile, int) else pl.multiple_of(tile * tm, tm)
        return pltpu.make_async_copy(x_hbm.at[pl.ds(start, tm), :], xbuf.at[slot], xsem.at[slot])

    def out_copy(tile):
        return pltpu.make_async_copy(obuf, o_hbm.at[pl.ds(pl.multiple_of(tile * tm, tm), tm), :],
                                     osem.at[0])

    def w_copies(group):
        slot = group % 2
        col = group * GROUP_CHUNKS * tf
        width = min(GROUP_CHUNKS, n_chunks - group * GROUP_CHUNKS) * tf
        return (
            pltpu.make_async_copy(wgu_hbm.at[:, pl.ds(col, width)],
                                  wg_buf.at[slot, :, pl.ds(0, width)], wsem.at[slot]),
            pltpu.make_async_copy(wgu_hbm.at[:, pl.ds(dff + col, width)],
                                  wu_buf.at[slot, :, pl.ds(0, width)], wsem.at[2 + slot]),
            pltpu.make_async_copy(wd_hbm.at[pl.ds(col, width), :],
                                  wd_buf.at[slot, pl.ds(0, width), :], wsem.at[4 + slot]),
        )

    @pl.when(i == 0)
    def _():
        x_copy(0, 0).start()
        for c in w_copies(0):
            c.start()
        x_copy(0, 0).wait()
        h_ref[...] = _modulated_norm(xbuf[0], g_ref, mod_ref, sub)

    @pl.when(i > 0)
    def _():
        h_ref[...] = hn_ref[...]

    next_tile = jnp.minimum(i + 1, n_tiles - 1)
    x_copy(next_tile, nxt).start()
    _cast_blocks(cast_in, cast_out)

    for group in range(n_groups):
        slot = group % 2
        for c in w_copies(group):
            c.wait()
        for c in w_copies((group + 1) % n_groups):
            c.start()
        if group == n_groups - norm_groups:
            x_copy(next_tile, nxt).wait()
        for k in range(min(GROUP_CHUNKS, n_chunks - group * GROUP_CHUNKS)):
            cols = slice(k * tf, (k + 1) * tf)
            h = h_ref[...]
            gate = jnp.dot(h, wg_buf[slot, :, cols], preferred_element_type=F32)
            up = jnp.dot(h, wu_buf[slot, :, cols], preferred_element_type=F32)
            act = (_silu(gate) * up).astype(BF16)
            down = jnp.dot(act, wd_buf[slot, cols, :], preferred_element_type=F32)
            if group == 0 and k == 0:
                acc_ref[...] = down
            else:
                acc_ref[...] += down
        if group >= n_groups - norm_groups:
            rows = pl.ds((group - (n_groups - norm_groups)) * norm_rows, norm_rows)
            hn_ref[rows, :] = _modulated_norm(xbuf[nxt, rows, :], g_ref, modn_ref, sub)

    @pl.when(i == n_tiles - 1)
    def _():
        for c in w_copies(0):
            c.wait()

    @pl.when(i > 0)
    def _():
        out_copy(i - 1).wait()

    gmod = mod_ref[0, pl.ds(3 * sub + 2, 1), :]
    out = xbuf[cur] + (FFN_RES * gmod) * acc_ref[...]
    if final_norm:
        out = _rms(out, gfin_ref[...])
    obuf[...] = out
    out_copy(i).start()

    @pl.when(i == n_tiles - 1)
    def _():
        out_copy(i).wait()


def _cast_specs(weights, n_steps, step=lambda i: i):
    specs = []
    for w in weights:
        rows, cols = w.shape
        assert rows % (n_steps * BF16_ROWS) == 0
        specs.append(pl.BlockSpec((rows // n_steps, cols), lambda *ids: (step(*ids), 0)))
    return specs


def _cast_shapes(weights):
    return [jax.ShapeDtypeStruct(w.shape, BF16) for w in weights]


def _cast_blocks(cast_in, cast_out):
    for src, dst in zip(cast_in, cast_out):
        dst[...] = src[...].astype(BF16)


def _ffn_call(x, mod, g, w_gu, w_down, g_final, *, sub, seq, final_norm, cast_weights=()):
    t, d = x.shape
    dff = w_down.shape[0]
    tm, tf = TM_FFN, TF_FFN
    n_tiles, n_chunks = t // tm, dff // tf
    tiles_per_batch = seq // tm
    group = GROUP_CHUNKS * tf
    nxt = lambda i: jnp.minimum(i + 1, n_tiles - 1)
    const = lambda i: (0, 0)
    kern = functools.partial(_ffn_kernel, sub=sub, n_tiles=n_tiles, n_chunks=n_chunks,
                             final_norm=final_norm, n_casts=len(cast_weights))
    return pl.pallas_call(
        kern,
        grid=(n_tiles,),
        in_specs=[
            pl.BlockSpec(memory_space=pl.ANY),
            pl.BlockSpec((1, N_SUB * N_MOD, d), lambda i: (i // tiles_per_batch, 0, 0)),
            pl.BlockSpec((1, N_SUB * N_MOD, d), lambda i: (nxt(i) // tiles_per_batch, 0, 0)),
            pl.BlockSpec((1, d), const),
            pl.BlockSpec(memory_space=pl.ANY),
            pl.BlockSpec(memory_space=pl.ANY),
            pl.BlockSpec((1, d), const),
        ] + _cast_specs(cast_weights, n_tiles),
        out_specs=[pl.BlockSpec(memory_space=pl.ANY)] + _cast_specs(cast_weights, n_tiles),
        out_shape=[jax.ShapeDtypeStruct((t, d), F32)] + _cast_shapes(cast_weights),
        scratch_shapes=[
            pltpu.VMEM((2, tm, d), F32),
            pltpu.VMEM((tm, d), BF16),
            pltpu.VMEM((tm, d), BF16),
            pltpu.VMEM((tm, d), F32),
            pltpu.VMEM((tm, d), F32),
            pltpu.VMEM((2, d, group), BF16),
            pltpu.VMEM((2, d, group), BF16),
            pltpu.VMEM((2, group, d), BF16),
            pltpu.SemaphoreType.DMA((2,)),
            pltpu.SemaphoreType.DMA((6,)),
            pltpu.SemaphoreType.DMA((1,)),
        ],
        compiler_params=pltpu.CompilerParams(
            dimension_semantics=("arbitrary",), vmem_limit_bytes=VMEM_LIMIT),
        name="ffn_sub%d" % sub,
    )(x, mod, mod, g.reshape(1, d), w_gu, w_down, g_final.reshape(1, d), *cast_weights)


def _packed_positions(positions):
    per_row = LANES // ROT_DIM
    return jnp.repeat(positions.reshape(-1, per_row), ROT_DIM, axis=1)


def _rope_tables(pos_ref, invf_ref, unpack_ref):
    per_row = LANES // ROT_DIM
    lane = lax.broadcasted_iota(jnp.int32, (1, LANES), 1)
    d = lane % HEAD_DIM
    ang = pos_ref[...].astype(F32) * invf_ref[...]
    n_packed = ang.shape[0]
    for t, packed in enumerate((jnp.cos(ang), jnp.sin(ang))):
        for u in range(per_row):
            shifted = packed if u == 0 else pltpu.roll(packed, LANES - ROT_DIM * u, 1)
            unpack_ref[t, pl.ds(u, n_packed, stride=per_row), :] = shifted

    def both_heads(a):
        a = jnp.where(lane < ROT_DIM, a, 0.0)
        return a + pltpu.roll(a, HEAD_DIM, 1)

    c, s = both_heads(unpack_ref[0]), both_heads(unpack_ref[1])
    half = ROT_DIM // 2
    coef_self = jnp.where(d < ROT_DIM, c, 1.0)
    coef_next = jnp.where(d < half, -s, 0.0)
    coef_prev = jnp.where((d >= half) & (d < ROT_DIM), s, 0.0)
    return coef_self, coef_next, coef_prev


def _rope_slab(z, tables):
    coef_self, coef_next, coef_prev = tables
    half = ROT_DIM // 2
    nxt = pltpu.roll(z, LANES - half, axis=1)
    prv = pltpu.roll(z, half, axis=1)
    return z * coef_self + nxt * coef_next + prv * coef_prev


def _inproj_kernel(x_ref, mod_ref, g_ref, w_ref, b_ref, pos_ref, invf_ref, *rest, sub, n_casts):
    cast_in, rest = rest[:n_casts], rest[n_casts:]
    qa_ref, ka_ref, va_ref, qb_ref, kb_ref, vb_ref = rest[:6]
    _cast_blocks(cast_in, rest[6:6 + n_casts])
    h = _modulated_norm(x_ref[...], g_ref, mod_ref, sub)
    tables = _rope_tables(pos_ref, invf_ref, rest[6 + n_casts])
    lane = lax.broadcasted_iota(jnp.int32, (1, LANES), 1)
    lo = lane < HEAD_DIM
    qscale = HEAD_DIM ** -0.5

    def proj_slabs(col, width):
        for c0 in range(0, width, MXU_COLS):
            z = (jnp.dot(h, w_ref[:, col + c0:col + c0 + MXU_COLS], preferred_element_type=F32)
                 + b_ref[:, col + c0:col + c0 + MXU_COLS])
            for half in range(MXU_COLS // LANES):
                yield c0 // LANES + half, z[:, half * LANES:(half + 1) * LANES]

    def dup_heads(slab):
        swapped = pltpu.roll(slab, HEAD_DIM, axis=1)
        return jnp.where(lo, slab, swapped), jnp.where(lo, swapped, slab)

    col = 0
    for k, z in proj_slabs(col, WIDTH_A):
        qa_ref[:, k * LANES:(k + 1) * LANES] = (_rope_slab(z, tables) * qscale).astype(BF16)
    col += WIDTH_A
    for k, z in proj_slabs(col, 2 * WIDTH_KV_A):
        dst = ka_ref if k == 0 else va_ref
        h0, h1 = dup_heads(_rope_slab(z, tables) if k == 0 else z)
        dst[:, 0:LANES] = h0.astype(BF16)
        dst[:, LANES:2 * LANES] = h1.astype(BF16)
    col += 2 * WIDTH_KV_A
    for k, z in proj_slabs(col, WIDTH_B):
        qb_ref[:, k * LANES:(k + 1) * LANES] = _rope_slab(z, tables) * qscale
    col += WIDTH_B
    for k, z in proj_slabs(col, WIDTH_B):
        kb_ref[:, k * LANES:(k + 1) * LANES] = _rope_slab(z, tables)
    col += WIDTH_B
    for k, z in proj_slabs(col, WIDTH_B):
        vb_ref[:, k * LANES:(k + 1) * LANES] = z


def _inproj_call(x, mod, g, w_in, b_in, pos, invf, *, sub, seq, cast_weights=()):
    t, d = x.shape
    tm = TM_PROJ
    tiles_per_batch = seq // tm
    row = lambda i: (i, 0)
    const = lambda i: (0, 0)
    return pl.pallas_call(
        functools.partial(_inproj_kernel, sub=sub, n_casts=len(cast_weights)),
        grid=(t // tm,),
        in_specs=[
            pl.BlockSpec((tm, d), row),
            pl.BlockSpec((1, N_SUB * N_MOD, d), lambda i: (i // tiles_per_batch, 0, 0)),
            pl.BlockSpec((1, d), const),
            pl.BlockSpec((d, IN_WIDTH), const, pipeline_mode=pl.Buffered(1)),
            pl.BlockSpec((1, IN_WIDTH), const),
            pl.BlockSpec((tm * ROT_DIM // LANES, LANES), row),
            pl.BlockSpec((1, LANES), const),
        ] + _cast_specs(cast_weights, t // tm),
        out_specs=[
            pl.BlockSpec((tm, WIDTH_A), row),
            pl.BlockSpec((tm, 2 * LANES), row),
            pl.BlockSpec((tm, 2 * LANES), row),
            pl.BlockSpec((tm, WIDTH_B), row),
            pl.BlockSpec((tm, WIDTH_B), row),
            pl.BlockSpec((tm, WIDTH_B), row),
        ] + _cast_specs(cast_weights, t // tm),
        out_shape=[
            jax.ShapeDtypeStruct((t, WIDTH_A), BF16),
            jax.ShapeDtypeStruct((t, 2 * LANES), BF16),
            jax.ShapeDtypeStruct((t, 2 * LANES), BF16),
            jax.ShapeDtypeStruct((t, WIDTH_B), F32),
            jax.ShapeDtypeStruct((t, WIDTH_B), F32),
            jax.ShapeDtypeStruct((t, WIDTH_B), F32),
        ] + _cast_shapes(cast_weights),
        scratch_shapes=[pltpu.VMEM((2, tm, LANES), F32)],
        compiler_params=pltpu.CompilerParams(
            dimension_semantics=("arbitrary",), vmem_limit_bytes=VMEM_LIMIT),
        name="inproj_rope",
    )(x, mod, g.reshape(1, d), w_in, b_in.reshape(1, IN_WIDTH), pos, invf, *cast_weights)


def _attn_a_kernel(sink_ref, q_ref, kp_ref, kc_ref, vp_ref, vc_ref, o_ref, t_sc, es_sc):
    i = pl.program_id(1)
    lane = lax.broadcasted_iota(jnp.int32, (1, LANES), 1)
    lo = lane < HEAD_DIM
    qi = lax.broadcasted_iota(jnp.int32, (BLOCK, 2 * BLOCK), 0)
    kj = lax.broadcasted_iota(jnp.int32, (BLOCK, 2 * BLOCK), 1)
    dist = qi + BLOCK - kj
    band = (dist >= 0) & (dist <= WINDOW_A - 1)
    bias_inner = jnp.where(band, 0.0, NEG_BIG).astype(F32)
    bias_first = jnp.where(band & ((kj >= BLOCK) | (i > 0)), 0.0, NEG_BIG).astype(F32)
    zero = jnp.zeros((BLOCK, LANES), BF16)
    ones = jnp.ones((2 * BLOCK, LANES), BF16)
    pairs = GROUP_A // 2
    units = [(u, hk) for u in range(QBLOCKS_A) for hk in range(N_KV_A)]

    def keys_values(ref_prev, ref_cur, u, hk):
        r0 = u * BLOCK
        cols = slice(hk * LANES, (hk + 1) * LANES)
        prev = ref_prev[:, cols] if u == 0 else ref_cur[r0 - BLOCK:r0, cols]
        return jnp.concatenate([prev, ref_cur[r0:r0 + BLOCK, cols]], axis=0)

    def scores(n):
        u, hk = units[n]
        r0, slot = u * BLOCK, n % 2
        rows = []
        for jp in range(pairs):
            c0 = (hk * pairs + jp) * LANES
            qp = q_ref[r0:r0 + BLOCK, c0:c0 + LANES]
            rows.append(jnp.where(lo, qp, zero))
            rows.append(jnp.where(lo, zero, qp))
        lhs = jnp.concatenate(rows, axis=0)
        s_all = lax.dot_general(lhs, keys_values(kp_ref, kc_ref, u, hk), (((1,), (1,)), ((), ())),
                                preferred_element_type=F32)
        bias = bias_first if u == 0 else bias_inner
        for g in range(GROUP_A):
            sink = sink_ref[hk * GROUP_A + g]
            s = s_all[g * BLOCK:(g + 1) * BLOCK, :] + bias
            m = jnp.maximum(jnp.max(s, axis=-1, keepdims=True), sink)
            t_sc[slot, g * BLOCK:(g + 1) * BLOCK, :] = s - m
            es_sc[slot, g] = jnp.broadcast_to(jnp.exp(sink - m), (BLOCK, LANES))

    def values(n):
        u, hk = units[n]
        r0, slot = u * BLOCK, n % 2
        v3 = jnp.concatenate([keys_values(vp_ref, vc_ref, u, hk), ones], axis=1)
        pv = jnp.dot(jnp.exp(t_sc[slot]).astype(BF16), v3, preferred_element_type=F32)
        outs = []
        for g in range(GROUP_A):
            blk = pv[g * BLOCK:(g + 1) * BLOCK, :]
            outs.append(blk[:, :LANES] * (1.0 / (blk[:, LANES:] + es_sc[slot, g])))
        for jp in range(pairs):
            c0 = (hk * pairs + jp) * LANES
            o_ref[r0:r0 + BLOCK, c0:c0 + LANES] = jnp.where(lo, outs[2 * jp], outs[2 * jp + 1])

    scores(0)
    for n in range(len(units)):
        values(n)
        if n + 1 < len(units):
            scores(n + 1)


def _attn_a_call(sinks, qa, ka, va, *, batch, seq):
    nb = seq // BLOCK
    rows = QBLOCKS_A * BLOCK
    steps = nb // QBLOCKS_A
    cur = lambda b, i: (b * steps + i, 0)
    prev = lambda b, i: (b * nb + jnp.maximum(QBLOCKS_A * i - 1, 0), 0)
    return pl.pallas_call(
        _attn_a_kernel,
        grid=(batch, steps),
        in_specs=[
            pl.BlockSpec(memory_space=pltpu.SMEM),
            pl.BlockSpec((rows, WIDTH_A), cur),
            pl.BlockSpec((BLOCK, 2 * LANES), prev),
            pl.BlockSpec((rows, 2 * LANES), cur),
            pl.BlockSpec((BLOCK, 2 * LANES), prev),
            pl.BlockSpec((rows, 2 * LANES), cur),
        ],
        out_specs=pl.BlockSpec((rows, WIDTH_A), cur),
        out_shape=jax.ShapeDtypeStruct((batch * seq, WIDTH_A), F32),
        scratch_shapes=[pltpu.VMEM((2, GROUP_A * BLOCK, 2 * BLOCK), F32),
                        pltpu.VMEM((2, GROUP_A, BLOCK, LANES), F32)],
        compiler_params=pltpu.CompilerParams(
            dimension_semantics=("arbitrary", "arbitrary"), vmem_limit_bytes=VMEM_LIMIT),
        name="attn_swa_gqa",
    )(sinks, qa, ka, ka, va, va)


def _attn_b_kernel(q_ref, k_ref, v_ref, *rest, seq, n_casts):
    cast_in, rest = rest[:n_casts], rest[n_casts:]
    o_ref, cast_out = rest[0], rest[1:1 + n_casts]
    x4_ref, t_sc, m_sc, l_sc, acc_sc = rest[1 + n_casts:]
    _cast_blocks(cast_in, cast_out)
    (_, d1), (_, d2), (_, d3) = DILATED
    assert d1 == 1 and d3 == d2 * d2 and all(w // dil == BLOCK for w, dil in DILATED)
    n_iter = seq // BLOCK
    cls = seq // d2
    nblk2 = cls // BLOCK
    assert seq // d3 == BLOCK and n_iter == d2 * nblk2 == d3

    lane = lax.broadcasted_iota(jnp.int32, (1, LANES), 1)
    lo = lane < HEAD_DIM
    qi = lax.broadcasted_iota(jnp.int32, (BLOCK, BLOCK), 0)
    kj = lax.broadcasted_iota(jnp.int32, (BLOCK, BLOCK), 1)
    bias_cur = jnp.where(kj <= qi, 0.0, NEG_BIG).astype(F32)
    bias_prev = jnp.where(kj >= qi, 0.0, NEG_BIG).astype(F32)
    zero = jnp.zeros((BLOCK, LANES), BF16)
    ones = jnp.ones((2 * BLOCK, LANES), BF16)

    for a, ref in enumerate((q_ref, k_ref, v_ref)):
        for r in range(d2):
            x4_ref[a, r * cls:(r + 1) * cls, :] = ref[0, pl.ds(r, cls, stride=d2), :]

    natural = lambda a, rows: (q_ref, k_ref, v_ref)[a][0, rows, :]
    by_class = lambda a, rows: x4_ref[a, rows, :]

    def block_rows(cfg, idx):
        static = isinstance(idx, int)
        block = lambda start: pl.ds(start if static else pl.multiple_of(start, BLOCK), BLOCK)
        biggest = max if static else jnp.maximum
        start = idx * BLOCK
        own = block(start)
        if cfg == 0:
            return natural, own, block(biggest(start - BLOCK, 0)), idx > 0
        if cfg == 1:
            n = idx % nblk2
            return by_class, own, block(biggest(start - BLOCK, start - n * BLOCK)), n > 0
        return by_class, pl.ds((idx % d2) * cls + idx // d2, BLOCK, stride=d2), None, None

    def scores(cfg, idx, slot):
        load, own, prev, has_prev = block_rows(cfg, idx)
        qb = load(0, own).astype(BF16)
        lhs = jnp.concatenate([jnp.where(lo, qb, zero), jnp.where(lo, zero, qb)], axis=0)
        if prev is None:
            keys, bias = load(1, own), bias_cur
        else:
            keys = jnp.concatenate([load(1, prev), load(1, own)], axis=0)
            off = jnp.where(has_prev, 0.0, NEG_BIG).astype(F32)
            bias = jnp.concatenate([bias_prev + off, bias_cur], axis=1)
        nk = keys.shape[0]
        s = lax.dot_general(lhs, keys.astype(BF16), (((1,), (1,)), ((), ())),
                            preferred_element_type=F32)
        s0 = s[:BLOCK, :] + bias
        s1 = s[BLOCK:, :] + bias
        m0 = jnp.max(s0, axis=-1, keepdims=True)
        m1 = jnp.max(s1, axis=-1, keepdims=True)
        t_sc[cfg, slot, :BLOCK, :nk] = s0 - m0
        t_sc[cfg, slot, BLOCK:, :nk] = s1 - m1
        m_sc[cfg, own, :] = jnp.where(lo, m0, m1)

    def values(cfg, idx, slot):
        load, own, prev, _ = block_rows(cfg, idx)
        vals = load(2, own) if prev is None else jnp.concatenate([load(2, prev), load(2, own)], axis=0)
        nk = vals.shape[0]
        p = jnp.exp(t_sc[cfg, slot, :, :nk]).astype(BF16)
        v3 = jnp.concatenate([vals.astype(BF16), ones[:nk, :]], axis=1)
        pv = jnp.dot(p, v3, preferred_element_type=F32)
        acc_sc[cfg, own, :] = jnp.where(lo, pv[:BLOCK, :LANES], pv[BLOCK:, :LANES])
        l_sc[cfg, own, :] = jnp.where(lo, pv[:BLOCK, LANES:], pv[BLOCK:, LANES:])

    n_cfg = len(DILATED)
    for cfg in range(n_cfg):
        scores(cfg, 0, 0)

    def body(k, carry):
        for half in range(2):
            b = 2 * k + half
            for cfg in range(n_cfg):
                values(cfg, b, half)
                scores(cfg, b + 1, 1 - half)
        return carry

    assert n_iter % 2 == 0
    lax.fori_loop(0, n_iter // 2 - 1, body, 0, unroll=UNROLL_B)
    for cfg in range(n_cfg):
        values(cfg, n_iter - 2, 0)
        scores(cfg, n_iter - 1, 1)
    for cfg in range(n_cfg):
        values(cfg, n_iter - 1, 1)

    def merge(i, carry):
        start = i * BLOCK
        xr = pl.ds(pl.multiple_of(start, BLOCK), BLOCK)
        nat = pl.ds(i // nblk2 + (i % nblk2) * (BLOCK * d2), BLOCK, stride=d2)
        sel = (nat, xr, xr)
        ms = [m_sc[c, sel[c], :] for c in range(len(DILATED))]
        m_all = functools.reduce(jnp.maximum, ms)
        ws = [jnp.exp(m - m_all) for m in ms]
        l_all = sum(w * l_sc[c, sel[c], :] for c, w in enumerate(ws))
        acc_all = sum(w * acc_sc[c, sel[c], :] for c, w in enumerate(ws))
        o_ref[0, nat, :] = acc_all * (1.0 / l_all)
        return carry

    lax.fori_loop(0, n_iter, merge, 0, unroll=True)


def _attn_b_call(qb, kb, vb, *, batch, seq, cast_weights=()):
    spec = pl.BlockSpec((1, seq, LANES), lambda b, hp: (b, 0, hp))
    shape3 = (batch, seq, WIDTH_B)
    pairs = WIDTH_B // LANES
    cast_specs = _cast_specs(cast_weights, batch * pairs, lambda b, hp: b * pairs + hp)
    return pl.pallas_call(
        functools.partial(_attn_b_kernel, seq=seq, n_casts=len(cast_weights)),
        grid=(batch, pairs),
        in_specs=[spec, spec, spec] + cast_specs,
        out_specs=[spec] + cast_specs,
        out_shape=[jax.ShapeDtypeStruct(shape3, F32)] + _cast_shapes(cast_weights),
        scratch_shapes=[pltpu.VMEM((3, seq, LANES), F32),
                        pltpu.VMEM((len(DILATED), 2, 2 * BLOCK, 2 * BLOCK), F32)]
        + [pltpu.VMEM((len(DILATED), seq, LANES), F32)] * 3,
        compiler_params=pltpu.CompilerParams(
            dimension_semantics=("arbitrary", "arbitrary"), vmem_limit_bytes=VMEM_LIMIT),
        name="attn_dilated",
    )(qb.reshape(shape3), kb.reshape(shape3), vb.reshape(shape3), *cast_weights)


def _outproj_kernel(oa_ref, ob_ref, x_ref, mod_ref, ga_ref, gb_ref, w_ref, b_ref, *rest,
                    sub, n_casts):
    o_ref = rest[n_casts]
    _cast_blocks(rest[:n_casts], rest[n_casts + 1:])
    ya = _rms(oa_ref[...], ga_ref[...]).astype(BF16)
    yb = _rms(ob_ref[...], gb_ref[...]).astype(BF16)
    y = (jnp.dot(ya, w_ref[0:WIDTH_A, :], preferred_element_type=F32)
         + jnp.dot(yb, w_ref[WIDTH_A:WIDTH_A + WIDTH_B, :], preferred_element_type=F32)
         + b_ref[...])
    gmod = mod_ref[0, pl.ds(3 * sub + 2, 1), :]
    o_ref[...] = x_ref[...] + gmod * y


def _outproj_call(out_a, out_b, x, mod, g_a, g_b, w_out, b_out, *, sub, seq, cast_weights=()):
    t, d = x.shape
    tm = TM_PROJ
    tiles_per_batch = seq // tm
    row = lambda i: (i, 0)
    const = lambda i: (0, 0)
    return pl.pallas_call(
        functools.partial(_outproj_kernel, sub=sub, n_casts=len(cast_weights)),
        grid=(t // tm,),
        in_specs=[
            pl.BlockSpec((tm, WIDTH_A), row),
            pl.BlockSpec((tm, WIDTH_B), row),
            pl.BlockSpec((tm, d), row),
            pl.BlockSpec((1, N_SUB * N_MOD, d), lambda i: (i // tiles_per_batch, 0, 0)),
            pl.BlockSpec((1, WIDTH_A), const),
            pl.BlockSpec((1, WIDTH_B), const),
            pl.BlockSpec((WIDTH_A + WIDTH_B, d), const),
            pl.BlockSpec((1, d), const),
        ] + _cast_specs(cast_weights, t // tm),
        out_specs=[pl.BlockSpec((tm, d), row)] + _cast_specs(cast_weights, t // tm),
        out_shape=[jax.ShapeDtypeStruct((t, d), F32)] + _cast_shapes(cast_weights),
        compiler_params=pltpu.CompilerParams(
            dimension_semantics=("arbitrary",), vmem_limit_bytes=VMEM_LIMIT),
        name="outproj_residual",
    )(out_a, out_b, x, mod, g_a.reshape(1, WIDTH_A), g_b.reshape(1, WIDTH_B), w_out,
      b_out.reshape(1, d), *cast_weights)


def kernel(x, c, positions, w_ada, b_ada, g_ffn1, w_ffn1_in, w_ffn1_out, g_mix, w_in, b_in, sinks,
           g_out_a, g_out_b, w_out, b_out, g_ffn2, w_ffn2_in, w_ffn2_out, g_final):
    batch, seq, d = x.shape
    depth = w_ada.shape[0]
    t = batch * seq
    xt = x.reshape(t, d)
    pos = _packed_positions(positions.reshape(t))
    lane_dim = jnp.arange(LANES, dtype=jnp.int32) % (ROT_DIM // 2)
    invf = (ROPE_THETA ** (-(2.0 * lane_dim.astype(F32)) / ROT_DIM)).reshape(1, LANES)
    for layer in range(depth):
        mod = _ada_call(c, w_ada[layer], b_ada[layer]).reshape(batch, N_SUB * N_MOD, d)
        last = layer == depth - 1
        xt, w_in_bf = _ffn_call(
            xt, mod, g_ffn1[layer], w_ffn1_in[layer].astype(BF16), w_ffn1_out[layer].astype(BF16),
            g_final, sub=0, seq=seq, final_norm=False, cast_weights=(w_in[layer],))
        qa, ka, va, qb, kb, vb, w_out_bf, w_down2_bf = _inproj_call(
            xt, mod, g_mix[layer], w_in_bf, b_in[layer], pos, invf, sub=1, seq=seq,
            cast_weights=(w_out[layer], w_ffn2_out[layer]))
        out_a = _attn_a_call(sinks[layer], qa, ka, va, batch=batch, seq=seq)
        out_b, w_gu2_bf = _attn_b_call(qb, kb, vb, batch=batch, seq=seq,
                                       cast_weights=(w_ffn2_in[layer],))
        xt, = _outproj_call(out_a, out_b.reshape(t, WIDTH_B), xt, mod, g_out_a[layer],
                            g_out_b[layer], w_out_bf, b_out[layer], sub=1, seq=seq)
        xt, = _ffn_call(xt, mod, g_ffn2[layer], w_gu2_bf, w_down2_bf, g_final, sub=2, seq=seq,
                        final_norm=last)
    if depth == 0:
        raise ValueError("depth must be >= 1")
    return xt.reshape(batch, seq, d)
```

```python
import functools

import jax
import jax.numpy as jnp
from jax import lax
from jax.experimental import pallas as pl
from jax.experimental.pallas import tpu as pltpu

F32 = jnp.float32
BF16 = jnp.bfloat16

D_MODEL = 2048
HEAD_DIM = 64
N_HEADS_A = 16
N_KV_A = 2
GROUP_A = N_HEADS_A // N_KV_A
N_HEADS_B = 16
WIDTH_A = N_HEADS_A * HEAD_DIM
WIDTH_KV_A = N_KV_A * HEAD_DIM
WIDTH_B = N_HEADS_B * HEAD_DIM
IN_WIDTH = WIDTH_A + 2 * WIDTH_KV_A + 3 * WIDTH_B
WINDOW_A = 128
DILATED = ((128, 1), (512, 4), (2048, 16))
BLOCK = 128
ROPE_THETA = 500000.0
ROT_DIM = HEAD_DIM // 4
D_FF = 5632
FFN_RES = 0.5
N_SUB = 3
N_MOD = 3
EPS = 1e-5

LANES = 128
MXU_COLS = 256
NEG_BIG = -1e30
LOG2_E = 1.4426950408889634
VMEM_LIMIT = 56 * 1024 * 1024

TM_FFN = 512
TF_FFN = 512
TM_PROJ = 512
TN_ADA = 1024
GROUP_CHUNKS = 2
BF16_ROWS = 16
QBLOCKS_A = 8
UNROLL_B = 7


def _silu(x):
    return x * (1.0 / (1.0 + jnp.exp(-x)))


def _rms(x, g):
    ms = jnp.mean(x * x, axis=-1, keepdims=True)
    return (x * lax.rsqrt(ms + EPS)) * g


def _modulated_norm(x, g_ref, mod_ref, sub):
    shift = mod_ref[0, pl.ds(3 * sub + 0, 1), :]
    scale = mod_ref[0, pl.ds(3 * sub + 1, 1), :]
    return (_rms(x, g_ref[...]) * (1.0 + scale) + shift).astype(BF16)


def _ada_kernel(c_ref, w_ref, b_ref, o_ref):
    cond = _silu(c_ref[...]).astype(BF16)
    o_ref[...] = jnp.dot(cond, w_ref[...].astype(BF16), preferred_element_type=F32) + b_ref[...]


def _ada_call(c, w_ada, b_ada):
    b, d = c.shape
    n = w_ada.shape[1]
    return pl.pallas_call(
        _ada_kernel,
        grid=(n // TN_ADA,),
        in_specs=[
            pl.BlockSpec((b, d), lambda j: (0, 0)),
            pl.BlockSpec((d, TN_ADA), lambda j: (0, j)),
            pl.BlockSpec((1, TN_ADA), lambda j: (0, j)),
        ],
        out_specs=pl.BlockSpec((b, TN_ADA), lambda j: (0, j)),
        out_shape=jax.ShapeDtypeStruct((b, n), F32),
        compiler_params=pltpu.CompilerParams(
            dimension_semantics=("arbitrary",), vmem_limit_bytes=VMEM_LIMIT),
        name="adaln_mod",
    )(c, w_ada, b_ada.reshape(1, n))


def _ffn_kernel(x_hbm, mod_ref, modn_ref, g_ref, wgu_hbm, wd_hbm, gfin_ref, *rest,
                sub, n_tiles, n_chunks, final_norm, n_casts):
    cast_in, rest = rest[:n_casts], rest[n_casts:]
    o_hbm, cast_out = rest[0], rest[1:1 + n_casts]
    (xbuf, h_ref, hn_ref, acc_ref, obuf, wg_buf, wu_buf, wd_buf,
     xsem, wsem, osem) = rest[1 + n_casts:]
    i = pl.program_id(0)
    tm = h_ref.shape[0]
    tf = wd_buf.shape[1] // GROUP_CHUNKS
    dff = n_chunks * tf
    cur = i % 2
    nxt = 1 - cur
    n_groups = -(-n_chunks // GROUP_CHUNKS)
    norm_groups = min(n_groups - 1, 4)
    norm_rows = tm // norm_groups
    assert n_groups % 2 == 0 and norm_rows * norm_groups == tm and norm_rows % BF16_ROWS == 0

    def x_copy(tile, slot):
        start = tile * tm if isinstance(tile, int) else pl.multiple_of(tile * tm, tm)
        return pltpu.make_async_copy(x_hbm.at[pl.ds(start, tm), :], xbuf.at[slot], xsem.at[slot])

    def out_copy(tile):
        return pltpu.make_async_copy(obuf, o_hbm.at[pl.ds(pl.multiple_of(tile * tm, tm), tm), :],
                                     osem.at[0])

    def w_copies(group):
        slot = group % 2
        col = group * GROUP_CHUNKS * tf
        width = min(GROUP_CHUNKS, n_chunks - group * GROUP_CHUNKS) * tf
        return (
            pltpu.make_async_copy(wgu_hbm.at[:, pl.ds(col, width)],
                                  wg_buf.at[slot, :, pl.ds(0, width)], wsem.at[slot]),
            pltpu.make_async_copy(wgu_hbm.at[:, pl.ds(dff + col, width)],
                                  wu_buf.at[slot, :, pl.ds(0, width)], wsem.at[2 + slot]),
            pltpu.make_async_copy(wd_hbm.at[pl.ds(col, width), :],
                                  wd_buf.at[slot, pl.ds(0, width), :], wsem.at[4 + slot]),
        )

    @pl.when(i == 0)
    def _():
        x_copy(0, 0).start()
        for c in w_copies(0):
            c.start()
        x_copy(0, 0).wait()
        h_ref[...] = _modulated_norm(xbuf[0], g_ref, mod_ref, sub)

    @pl.when(i > 0)
    def _():
        h_ref[...] = hn_ref[...]

    next_tile = jnp.minimum(i + 1, n_tiles - 1)
    x_copy(next_tile, nxt).start()
    _cast_blocks(cast_in, cast_out)

    for group in range(n_groups):
        slot = group % 2
        for c in w_copies(group):
            c.wait()
        for c in w_copies((group + 1) % n_groups):
            c.start()
        if group == n_groups - norm_groups:
            x_copy(next_tile, nxt).wait()
        for k in range(min(GROUP_CHUNKS, n_chunks - group * GROUP_CHUNKS)):
            cols = slice(k * tf, (k + 1) * tf)
            h = h_ref[...]
            gate = jnp.dot(h, wg_buf[slot, :, cols], preferred_element_type=F32)
            up = jnp.dot(h, wu_buf[slot, :, cols], preferred_element_type=F32)
            act = (_silu(gate) * up).astype(BF16)
            down = jnp.dot(act, wd_buf[slot, cols, :], preferred_element_type=F32)
            if group == 0 and k == 0:
                acc_ref[...] = down
            else:
                acc_ref[...] += down
        if group >= n_groups - norm_groups:
            rows = pl.ds((group - (n_groups - norm_groups)) * norm_rows, norm_rows)
            hn_ref[rows, :] = _modulated_norm(xbuf[nxt, rows, :], g_ref, modn_ref, sub)

    @pl.when(i == n_tiles - 1)
    def _():
        for c in w_copies(0):
            c.wait()

    @pl.when(i > 0)
    def _():
        out_copy(i - 1).wait()

    gmod = mod_ref[0, pl.ds(3 * sub + 2, 1), :]
    out = xbuf[cur] + (FFN_RES * gmod) * acc_ref[...]
    if final_norm:
        out = _rms(out, gfin_ref[...])
    obuf[...] = out
    out_copy(i).start()

    @pl.when(i == n_tiles - 1)
    def _():
        out_copy(i).wait()


def _cast_specs(weights, n_steps, step=lambda i: i):
    specs = []
    for w in weights:
        rows, cols = w.shape
        assert rows % (n_steps * BF16_ROWS) == 0
        specs.append(pl.BlockSpec((rows // n_steps, cols), lambda *ids: (step(*ids), 0)))
    return specs


def _cast_shapes(weights):
    return [jax.ShapeDtypeStruct(w.shape, BF16) for w in weights]


def _cast_blocks(cast_in, cast_out):
    for src, dst in zip(cast_in, cast_out):
        dst[...] = src[...].astype(BF16)


def _ffn_call(x, mod, g, w_gu, w_down, g_final, *, sub, seq, final_norm, cast_weights=()):
    t, d = x.shape
    dff = w_down.shape[0]
    tm, tf = TM_FFN, TF_FFN
    n_tiles, n_chunks = t // tm, dff // tf
    tiles_per_batch = seq // tm
    group = GROUP_CHUNKS * tf
    nxt = lambda i: jnp.minimum(i + 1, n_tiles - 1)
    const = lambda i: (0, 0)
    kern = functools.partial(_ffn_kernel, sub=sub, n_tiles=n_tiles, n_chunks=n_chunks,
                             final_norm=final_norm, n_casts=len(cast_weights))
    return pl.pallas_call(
        kern,
        grid=(n_tiles,),
        in_specs=[
            pl.BlockSpec(memory_space=pl.ANY),
            pl.BlockSpec((1, N_SUB * N_MOD, d), lambda i: (i // tiles_per_batch, 0, 0)),
            pl.BlockSpec((1, N_SUB * N_MOD, d), lambda i: (nxt(i) // tiles_per_batch, 0, 0)),
            pl.BlockSpec((1, d), const),
            pl.BlockSpec(memory_space=pl.ANY),
            pl.BlockSpec(memory_space=pl.ANY),
            pl.BlockSpec((1, d), const),
        ] + _cast_specs(cast_weights, n_tiles),
        out_specs=[pl.BlockSpec(memory_space=pl.ANY)] + _cast_specs(cast_weights, n_tiles),
        out_shape=[jax.ShapeDtypeStruct((t, d), F32)] + _cast_shapes(cast_weights),
        scratch_shapes=[
            pltpu.VMEM((2, tm, d), F32),
            pltpu.VMEM((tm, d), BF16),
            pltpu.VMEM((tm, d), BF16),
            pltpu.VMEM((tm, d), F32),
            pltpu.VMEM((tm, d), F32),
            pltpu.VMEM((2, d, group), BF16),
            pltpu.VMEM((2, d, group), BF16),
            pltpu.VMEM((2, group, d), BF16),
            pltpu.SemaphoreType.DMA((2,)),
            pltpu.SemaphoreType.DMA((6,)),
            pltpu.SemaphoreType.DMA((1,)),
        ],
        compiler_params=pltpu.CompilerParams(
            dimension_semantics=("arbitrary",), vmem_limit_bytes=VMEM_LIMIT),
        name="ffn_sub%d" % sub,
    )(x, mod, mod, g.reshape(1, d), w_gu, w_down, g_final.reshape(1, d), *cast_weights)


def _packed_positions(positions):
    per_row = LANES // ROT_DIM
    return jnp.repeat(positions.reshape(-1, per_row), ROT_DIM, axis=1)


def _rope_tables(pos_ref, invf_ref, unpack_ref):
    per_row = LANES // ROT_DIM
    lane = lax.broadcasted_iota(jnp.int32, (1, LANES), 1)
    d = lane % HEAD_DIM
    ang = pos_ref[...].astype(F32) * invf_ref[...]
    n_packed = ang.shape[0]
    for t, packed in enumerate((jnp.cos(ang), jnp.sin(ang))):
        for u in range(per_row):
            shifted = packed if u == 0 else pltpu.roll(packed, LANES - ROT_DIM * u, 1)
            unpack_ref[t, pl.ds(u, n_packed, stride=per_row), :] = shifted

    def both_heads(a):
        a = jnp.where(lane < ROT_DIM, a, 0.0)
        return a + pltpu.roll(a, HEAD_DIM, 1)

    c, s = both_heads(unpack_ref[0]), both_heads(unpack_ref[1])
    half = ROT_DIM // 2
    coef_self = jnp.where(d < ROT_DIM, c, 1.0)
    coef_next = jnp.where(d < half, -s, 0.0)
    coef_prev = jnp.where((d >= half) & (d < ROT_DIM), s, 0.0)
    return coef_self, coef_next, coef_prev


def _rope_slab(z, tables):
    coef_self, coef_next, coef_prev = tables
    half = ROT_DIM // 2
    nxt = pltpu.roll(z, LANES - half, axis=1)
    prv = pltpu.roll(z, half, axis=1)
    return z * coef_self + nxt * coef_next + prv * coef_prev


def _inproj_kernel(x_ref, mod_ref, g_ref, w_ref, b_ref, pos_ref, invf_ref, *rest, sub, n_casts):
    cast_in, rest = rest[:n_casts], rest[n_casts:]
    qa_ref, ka_ref, va_ref, qb_ref, kb_ref, vb_ref = rest[:6]
    _cast_blocks(cast_in, rest[6:6 + n_casts])
    h = _modulated_norm(x_ref[...], g_ref, mod_ref, sub)
    tables = _rope_tables(pos_ref, invf_ref, rest[6 + n_casts])
    lane = lax.broadcasted_iota(jnp.int32, (1, LANES), 1)
    lo = lane < HEAD_DIM
    qscale = HEAD_DIM ** -0.5 * LOG2_E

    def proj_slabs(col, width):
        for c0 in range(0, width, MXU_COLS):
            z = (jnp.dot(h, w_ref[:, col + c0:col + c0 + MXU_COLS], preferred_element_type=F32)
                 + b_ref[:, col + c0:col + c0 + MXU_COLS])
            for half in range(MXU_COLS // LANES):
                yield c0 // LANES + half, z[:, half * LANES:(half + 1) * LANES]

    def dup_heads(slab):
        swapped = pltpu.roll(slab, HEAD_DIM, axis=1)
        return jnp.where(lo, slab, swapped), jnp.where(lo, swapped, slab)

    col = 0
    for k, z in proj_slabs(col, WIDTH_A):
        qa_ref[:, k * LANES:(k + 1) * LANES] = (_rope_slab(z, tables) * qscale).astype(BF16)
    col += WIDTH_A
    for k, z in proj_slabs(col, 2 * WIDTH_KV_A):
        dst = ka_ref if k == 0 else va_ref
        h0, h1 = dup_heads(_rope_slab(z, tables) if k == 0 else z)
        dst[:, 0:LANES] = h0.astype(BF16)
        dst[:, LANES:2 * LANES] = h1.astype(BF16)
    col += 2 * WIDTH_KV_A
    for k, z in proj_slabs(col, WIDTH_B):
        qb_ref[:, k * LANES:(k + 1) * LANES] = _rope_slab(z, tables) * qscale
    col += WIDTH_B
    for k, z in proj_slabs(col, WIDTH_B):
        kb_ref[:, k * LANES:(k + 1) * LANES] = _rope_slab(z, tables)
    col += WIDTH_B
    for k, z in proj_slabs(col, WIDTH_B):
        vb_ref[:, k * LANES:(k + 1) * LANES] = z


def _inproj_call(x, mod, g, w_in, b_in, pos, invf, *, sub, seq, cast_weights=()):
    t, d = x.shape
    tm = TM_PROJ
    tiles_per_batch = seq // tm
    row = lambda i: (i, 0)
    const = lambda i: (0, 0)
    return pl.pallas_call(
        functools.partial(_inproj_kernel, sub=sub, n_casts=len(cast_weights)),
        grid=(t // tm,),
        in_specs=[
            pl.BlockSpec((tm, d), row),
            pl.BlockSpec((1, N_SUB * N_MOD, d), lambda i: (i // tiles_per_batch, 0, 0)),
            pl.BlockSpec((1, d), const),
            pl.BlockSpec((d, IN_WIDTH), const, pipeline_mode=pl.Buffered(1)),
            pl.BlockSpec((1, IN_WIDTH), const),
            pl.BlockSpec((tm * ROT_DIM // LANES, LANES), row),
            pl.BlockSpec((1, LANES), const),
        ] + _cast_specs(cast_weights, t // tm),
        out_specs=[
            pl.BlockSpec((tm, WIDTH_A), row),
            pl.BlockSpec((tm, 2 * LANES), row),
            pl.BlockSpec((tm, 2 * LANES), row),
            pl.BlockSpec((tm, WIDTH_B), row),
            pl.BlockSpec((tm, WIDTH_B), row),
            pl.BlockSpec((tm, WIDTH_B), row),
        ] + _cast_specs(cast_weights, t // tm),
        out_shape=[
            jax.ShapeDtypeStruct((t, WIDTH_A), BF16),
            jax.ShapeDtypeStruct((t, 2 * LANES), BF16),
            jax.ShapeDtypeStruct((t, 2 * LANES), BF16),
            jax.ShapeDtypeStruct((t, WIDTH_B), F32),
            jax.ShapeDtypeStruct((t, WIDTH_B), F32),
            jax.ShapeDtypeStruct((t, WIDTH_B), F32),
        ] + _cast_shapes(cast_weights),
        scratch_shapes=[pltpu.VMEM((2, tm, LANES), F32)],
        compiler_params=pltpu.CompilerParams(
            dimension_semantics=("arbitrary",), vmem_limit_bytes=VMEM_LIMIT),
        name="inproj_rope",
    )(x, mod, g.reshape(1, d), w_in, b_in.reshape(1, IN_WIDTH), pos, invf, *cast_weights)


def _attn_a_kernel(sink_ref, q_ref, kp_ref, kc_ref, vp_ref, vc_ref, o_ref, t_sc, es_sc):
    i = pl.program_id(1)
    lane = lax.broadcasted_iota(jnp.int32, (1, LANES), 1)
    lo = lane < HEAD_DIM
    qi = lax.broadcasted_iota(jnp.int32, (BLOCK, 2 * BLOCK), 0)
    kj = lax.broadcasted_iota(jnp.int32, (BLOCK, 2 * BLOCK), 1)
    dist = qi + BLOCK - kj
    band = (dist >= 0) & (dist <= WINDOW_A - 1)
    bias_inner = jnp.where(band, 0.0, NEG_BIG).astype(F32)
    bias_first = jnp.where(band & ((kj >= BLOCK) | (i > 0)), 0.0, NEG_BIG).astype(F32)
    zero = jnp.zeros((BLOCK, LANES), BF16)
    ones = jnp.ones((2 * BLOCK, LANES), BF16)
    pairs = GROUP_A // 2
    units = [(u, hk) for u in range(QBLOCKS_A) for hk in range(N_KV_A)]

    def keys_values(ref_prev, ref_cur, u, hk):
        r0 = u * BLOCK
        cols = slice(hk * LANES, (hk + 1) * LANES)
        prev = ref_prev[:, cols] if u == 0 else ref_cur[r0 - BLOCK:r0, cols]
        return jnp.concatenate([prev, ref_cur[r0:r0 + BLOCK, cols]], axis=0)

    def scores(n):
        u, hk = units[n]
        r0, slot = u * BLOCK, n % 2
        rows = []
        for jp in range(pairs):
            c0 = (hk * pairs + jp) * LANES
            qp = q_ref[r0:r0 + BLOCK, c0:c0 + LANES]
            rows.append(jnp.where(lo, qp, zero))
            rows.append(jnp.where(lo, zero, qp))
        lhs = jnp.concatenate(rows, axis=0)
        s_all = lax.dot_general(lhs, keys_values(kp_ref, kc_ref, u, hk), (((1,), (1,)), ((), ())),
                                preferred_element_type=F32)
        bias = bias_first if u == 0 else bias_inner
        for g in range(GROUP_A):
            sink = sink_ref[hk * GROUP_A + g] * LOG2_E
            s = s_all[g * BLOCK:(g + 1) * BLOCK, :] + bias
            m = jnp.maximum(jnp.max(s, axis=-1, keepdims=True), sink)
            t_sc[slot, g * BLOCK:(g + 1) * BLOCK, :] = s - m
            es_sc[slot, g] = jnp.broadcast_to(jnp.exp2(sink - m), (BLOCK, LANES))

    def values(n):
        u, hk = units[n]
        r0, slot = u * BLOCK, n % 2
        v3 = jnp.concatenate([keys_values(vp_ref, vc_ref, u, hk), ones], axis=1)
        pv = jnp.dot(jnp.exp2(t_sc[slot]).astype(BF16), v3, preferred_element_type=F32)
        outs = []
        for g in range(GROUP_A):
            blk = pv[g * BLOCK:(g + 1) * BLOCK, :]
            outs.append(blk[:, :LANES] * (1.0 / (blk[:, LANES:] + es_sc[slot, g])))
        for jp in range(pairs):
            c0 = (hk * pairs + jp) * LANES
            o_ref[r0:r0 + BLOCK, c0:c0 + LANES] = jnp.where(lo, outs[2 * jp], outs[2 * jp + 1])

    scores(0)
    for n in range(len(units)):
        values(n)
        if n + 1 < len(units):
            scores(n + 1)


def _attn_a_call(sinks, qa, ka, va, *, batch, seq):
    nb = seq // BLOCK
    rows = QBLOCKS_A * BLOCK
    steps = nb // QBLOCKS_A
    cur = lambda b, i: (b * steps + i, 0)
    prev = lambda b, i: (b * nb + jnp.maximum(QBLOCKS_A * i - 1, 0), 0)
    return pl.pallas_call(
        _attn_a_kernel,
        grid=(batch, steps),
        in_specs=[
            pl.BlockSpec(memory_space=pltpu.SMEM),
            pl.BlockSpec((rows, WIDTH_A), cur),
            pl.BlockSpec((BLOCK, 2 * LANES), prev),
            pl.BlockSpec((rows, 2 * LANES), cur),
            pl.BlockSpec((BLOCK, 2 * LANES), prev),
            pl.BlockSpec((rows, 2 * LANES), cur),
        ],
        out_specs=pl.BlockSpec((rows, WIDTH_A), cur),
        out_shape=jax.ShapeDtypeStruct((batch * seq, WIDTH_A), F32),
        scratch_shapes=[pltpu.VMEM((2, GROUP_A * BLOCK, 2 * BLOCK), F32),
                        pltpu.VMEM((2, GROUP_A, BLOCK, LANES), F32)],
        compiler_params=pltpu.CompilerParams(
            dimension_semantics=("arbitrary", "arbitrary"), vmem_limit_bytes=VMEM_LIMIT),
        name="attn_swa_gqa",
    )(sinks, qa, ka, ka, va, va)


def _attn_b_kernel(q_ref, k_ref, v_ref, *rest, seq, n_casts):
    cast_in, rest = rest[:n_casts], rest[n_casts:]
    o_ref, cast_out = rest[0], rest[1:1 + n_casts]
    x4_ref, t_sc, m_sc, l_sc, acc_sc = rest[1 + n_casts:]
    _cast_blocks(cast_in, cast_out)
    (_, d1), (_, d2), (_, d3) = DILATED
    assert d1 == 1 and d3 == d2 * d2 and all(w // dil == BLOCK for w, dil in DILATED)
    n_iter = seq // BLOCK
    cls = seq // d2
    nblk2 = cls // BLOCK
    assert seq // d3 == BLOCK and n_iter == d2 * nblk2 == d3

    lane = lax.broadcasted_iota(jnp.int32, (1, LANES), 1)
    lo = lane < HEAD_DIM
    qi = lax.broadcasted_iota(jnp.int32, (BLOCK, BLOCK), 0)
    kj = lax.broadcasted_iota(jnp.int32, (BLOCK, BLOCK), 1)
    bias_cur = jnp.where(kj <= qi, 0.0, NEG_BIG).astype(F32)
    bias_prev = jnp.where(kj >= qi, 0.0, NEG_BIG).astype(F32)
    zero = jnp.zeros((BLOCK, LANES), BF16)
    ones = jnp.ones((2 * BLOCK, LANES), BF16)

    for a, ref in enumerate((q_ref, k_ref, v_ref)):
        for r in range(d2):
            x4_ref[a, r * cls:(r + 1) * cls, :] = ref[0, pl.ds(r, cls, stride=d2), :]

    natural = lambda a, rows: (q_ref, k_ref, v_ref)[a][0, rows, :]
    by_class = lambda a, rows: x4_ref[a, rows, :]

    def block_rows(cfg, idx):
        static = isinstance(idx, int)
        block = lambda start: pl.ds(start if static else pl.multiple_of(start, BLOCK), BLOCK)
        biggest = max if static else jnp.maximum
        start = idx * BLOCK
        own = block(start)
        if cfg == 0:
            return natural, own, block(biggest(start - BLOCK, 0)), idx > 0
        if cfg == 1:
            n = idx % nblk2
            return by_class, own, block(biggest(start - BLOCK, start - n * BLOCK)), n > 0
        return by_class, pl.ds((idx % d2) * cls + idx // d2, BLOCK, stride=d2), None, None

    def scores(cfg, idx, slot):
        load, own, prev, has_prev = block_rows(cfg, idx)
        qb = load(0, own).astype(BF16)
        lhs = jnp.concatenate([jnp.where(lo, qb, zero), jnp.where(lo, zero, qb)], axis=0)
        if prev is None:
            keys, bias = load(1, own), bias_cur
        else:
            keys = jnp.concatenate([load(1, prev), load(1, own)], axis=0)
            off = jnp.where(has_prev, 0.0, NEG_BIG).astype(F32)
            bias = jnp.concatenate([bias_prev + off, bias_cur], axis=1)
        nk = keys.shape[0]
        s = lax.dot_general(lhs, keys.astype(BF16), (((1,), (1,)), ((), ())),
                            preferred_element_type=F32)
        s0 = s[:BLOCK, :] + bias
        s1 = s[BLOCK:, :] + bias
        m0 = jnp.max(s0, axis=-1, keepdims=True)
        m1 = jnp.max(s1, axis=-1, keepdims=True)
        t_sc[cfg, slot, :BLOCK, :nk] = s0 - m0
        t_sc[cfg, slot, BLOCK:, :nk] = s1 - m1
        m_sc[cfg, own, :] = jnp.where(lo, m0, m1)

    def values(cfg, idx, slot):
        load, own, prev, _ = block_rows(cfg, idx)
        vals = load(2, own) if prev is None else jnp.concatenate([load(2, prev), load(2, own)], axis=0)
        nk = vals.shape[0]
        p = jnp.exp2(t_sc[cfg, slot, :, :nk]).astype(BF16)
        v3 = jnp.concatenate([vals.astype(BF16), ones[:nk, :]], axis=1)
        pv = jnp.dot(p, v3, preferred_element_type=F32)
        acc_sc[cfg, own, :] = jnp.where(lo, pv[:BLOCK, :LANES], pv[BLOCK:, :LANES])
        l_sc[cfg, own, :] = jnp.where(lo, pv[:BLOCK, LANES:], pv[BLOCK:, LANES:])

    n_cfg = len(DILATED)
    for cfg in range(n_cfg):
        scores(cfg, 0, 0)

    def body(k, carry):
        for half in range(2):
            b = 2 * k + half
            for cfg in range(n_cfg):
                values(cfg, b, half)
                scores(cfg, b + 1, 1 - half)
        return carry

    assert n_iter % 2 == 0
    lax.fori_loop(0, n_iter // 2 - 1, body, 0, unroll=UNROLL_B)
    for cfg in range(n_cfg):
        values(cfg, n_iter - 2, 0)
        scores(cfg, n_iter - 1, 1)
    for cfg in range(n_cfg):
        values(cfg, n_iter - 1, 1)

    def merge(i, carry):
        start = i * BLOCK
        xr = pl.ds(pl.multiple_of(start, BLOCK), BLOCK)
        nat = pl.ds(i // nblk2 + (i % nblk2) * (BLOCK * d2), BLOCK, stride=d2)
        sel = (nat, xr, xr)
        ms = [m_sc[c, sel[c], :] for c in range(len(DILATED))]
        m_all = functools.reduce(jnp.maximum, ms)
        ws = [jnp.exp2(m - m_all) for m in ms]
        l_all = sum(w * l_sc[c, sel[c], :] for c, w in enumerate(ws))
        acc_all = sum(w * acc_sc[c, sel[c], :] for c, w in enumerate(ws))
        o_ref[0, nat, :] = acc_all * (1.0 / l_all)
        return carry

    lax.fori_loop(0, n_iter, merge, 0, unroll=True)


def _attn_b_call(qb, kb, vb, *, batch, seq, cast_weights=()):
    spec = pl.BlockSpec((1, seq, LANES), lambda b, hp: (b, 0, hp))
    shape3 = (batch, seq, WIDTH_B)
    pairs = WIDTH_B // LANES
    cast_specs = _cast_specs(cast_weights, batch * pairs, lambda b, hp: b * pairs + hp)
    return pl.pallas_call(
        functools.partial(_attn_b_kernel, seq=seq, n_casts=len(cast_weights)),
        grid=(batch, pairs),
        in_specs=[spec, spec, spec] + cast_specs,
        out_specs=[spec] + cast_specs,
        out_shape=[jax.ShapeDtypeStruct(shape3, F32)] + _cast_shapes(cast_weights),
        scratch_shapes=[pltpu.VMEM((3, seq, LANES), F32),
                        pltpu.VMEM((len(DILATED), 2, 2 * BLOCK, 2 * BLOCK), F32)]
        + [pltpu.VMEM((len(DILATED), seq, LANES), F32)] * 3,
        compiler_params=pltpu.CompilerParams(
            dimension_semantics=("arbitrary", "arbitrary"), vmem_limit_bytes=VMEM_LIMIT),
        name="attn_dilated",
    )(qb.reshape(shape3), kb.reshape(shape3), vb.reshape(shape3), *cast_weights)


def _outproj_kernel(oa_ref, ob_ref, x_ref, mod_ref, ga_ref, gb_ref, w_ref, b_ref, *rest,
                    sub, n_casts):
    o_ref = rest[n_casts]
    _cast_blocks(rest[:n_casts], rest[n_casts + 1:])
    ya = _rms(oa_ref[...], ga_ref[...]).astype(BF16)
    yb = _rms(ob_ref[...], gb_ref[...]).astype(BF16)
    y = (jnp.dot(ya, w_ref[0:WIDTH_A, :], preferred_element_type=F32)
         + jnp.dot(yb, w_ref[WIDTH_A:WIDTH_A + WIDTH_B, :], preferred_element_type=F32)
         + b_ref[...])
    gmod = mod_ref[0, pl.ds(3 * sub + 2, 1), :]
    o_ref[...] = x_ref[...] + gmod * y


def _outproj_call(out_a, out_b, x, mod, g_a, g_b, w_out, b_out, *, sub, seq, cast_weights=()):
    t, d = x.shape
    tm = TM_PROJ
    tiles_per_batch = seq // tm
    row = lambda i: (i, 0)
    const = lambda i: (0, 0)
    return pl.pallas_call(
        functools.partial(_outproj_kernel, sub=sub, n_casts=len(cast_weights)),
        grid=(t // tm,),
        in_specs=[
            pl.BlockSpec((tm, WIDTH_A), row),
            pl.BlockSpec((tm, WIDTH_B), row),
            pl.BlockSpec((tm, d), row),
            pl.BlockSpec((1, N_SUB * N_MOD, d), lambda i: (i // tiles_per_batch, 0, 0)),
            pl.BlockSpec((1, WIDTH_A), const),
            pl.BlockSpec((1, WIDTH_B), const),
            pl.BlockSpec((WIDTH_A + WIDTH_B, d), const),
            pl.BlockSpec((1, d), const),
        ] + _cast_specs(cast_weights, t // tm),
        out_specs=[pl.BlockSpec((tm, d), row)] + _cast_specs(cast_weights, t // tm),
        out_shape=[jax.ShapeDtypeStruct((t, d), F32)] + _cast_shapes(cast_weights),
        compiler_params=pltpu.CompilerParams(
            dimension_semantics=("arbitrary",), vmem_limit_bytes=VMEM_LIMIT),
        name="outproj_residual",
    )(out_a, out_b, x, mod, g_a.reshape(1, WIDTH_A), g_b.reshape(1, WIDTH_B), w_out,
      b_out.reshape(1, d), *cast_weights)


def kernel(x, c, positions, w_ada, b_ada, g_ffn1, w_ffn1_in, w_ffn1_out, g_mix, w_in, b_in, sinks,
           g_out_a, g_out_b, w_out, b_out, g_ffn2, w_ffn2_in, w_ffn2_out, g_final):
    batch, seq, d = x.shape
    depth = w_ada.shape[0]
    t = batch * seq
    xt = x.reshape(t, d)
    pos = _packed_positions(positions.reshape(t))
    lane_dim = jnp.arange(LANES, dtype=jnp.int32) % (ROT_DIM // 2)
    invf = (ROPE_THETA ** (-(2.0 * lane_dim.astype(F32)) / ROT_DIM)).reshape(1, LANES)
    for layer in range(depth):
        mod = _ada_call(c, w_ada[layer], b_ada[layer]).reshape(batch, N_SUB * N_MOD, d)
        last = layer == depth - 1
        xt, w_in_bf = _ffn_call(
            xt, mod, g_ffn1[layer], w_ffn1_in[layer].astype(BF16), w_ffn1_out[layer].astype(BF16),
            g_final, sub=0, seq=seq, final_norm=False, cast_weights=(w_in[layer],))
        qa, ka, va, qb, kb, vb, w_out_bf, w_down2_bf = _inproj_call(
            xt, mod, g_mix[layer], w_in_bf, b_in[layer], pos, invf, sub=1, seq=seq,
            cast_weights=(w_out[layer], w_ffn2_out[layer]))
        out_a = _attn_a_call(sinks[layer], qa, ka, va, batch=batch, seq=seq)
        out_b, w_gu2_bf = _attn_b_call(qb, kb, vb, batch=batch, seq=seq,
                                       cast_weights=(w_ffn2_in[layer],))
        xt, = _outproj_call(out_a, out_b.reshape(t, WIDTH_B), xt, mod, g_out_a[layer],
                            g_out_b[layer], w_out_bf, b_out[layer], sub=1, seq=seq)
        xt, = _ffn_call(xt, mod, g_ffn2[layer], w_gu2_bf, w_down2_bf, g_final, sub=2, seq=seq,
                        final_norm=last)
    if depth == 0:
        raise ValueError("depth must be >= 1")
    return xt.reshape(batch, seq, d)
```

```python
import functools

import jax
import jax.numpy as jnp
from jax import lax
from jax.experimental import pallas as pl
from jax.experimental.pallas import tpu as pltpu

F32 = jnp.float32
BF16 = jnp.bfloat16

D_MODEL = 2048
HEAD_DIM = 64
N_HEADS_A = 16
N_KV_A = 2
GROUP_A = N_HEADS_A // N_KV_A
N_HEADS_B = 16
WIDTH_A = N_HEADS_A * HEAD_DIM
WIDTH_KV_A = N_KV_A * HEAD_DIM
WIDTH_B = N_HEADS_B * HEAD_DIM
IN_WIDTH = WIDTH_A + 2 * WIDTH_KV_A + 3 * WIDTH_B
WINDOW_A = 128
DILATED = ((128, 1), (512, 4), (2048, 16))
BLOCK = 128
ROPE_THETA = 500000.0
ROT_DIM = HEAD_DIM // 4
D_FF = 5632
FFN_RES = 0.5
N_SUB = 3
N_MOD = 3
EPS = 1e-5

LANES = 128
MXU_COLS = 256
NEG_BIG = -1e30
LOG2_E = 1.4426950408889634
VMEM_LIMIT = 56 * 1024 * 1024

TM_FFN = 512
TF_FFN = 512
TM_PROJ = 512
TN_ADA = 1024
GROUP_CHUNKS = 2
BF16_ROWS = 16
QBLOCKS_A = 8
UNROLL_B = 7


def _silu(x):
    return x * (1.0 / (1.0 + jnp.exp(-x)))


def _rms(x, g):
    ms = jnp.mean(x * x, axis=-1, keepdims=True)
    return (x * lax.rsqrt(ms + EPS)) * g


def _modulated_norm(x, g_ref, mod_ref, sub):
    shift = mod_ref[0, pl.ds(3 * sub + 0, 1), :]
    scale = mod_ref[0, pl.ds(3 * sub + 1, 1), :]
    return (_rms(x, g_ref[...] * (1.0 + scale)) + shift).astype(BF16)


def _ada_kernel(c_ref, w_ref, b_ref, o_ref):
    cond = _silu(c_ref[...]).astype(BF16)
    o_ref[...] = jnp.dot(cond, w_ref[...].astype(BF16), preferred_element_type=F32) + b_ref[...]


def _ada_call(c, w_ada, b_ada):
    b, d = c.shape
    n = w_ada.shape[1]
    return pl.pallas_call(
        _ada_kernel,
        grid=(n // TN_ADA,),
        in_specs=[
            pl.BlockSpec((b, d), lambda j: (0, 0)),
            pl.BlockSpec((d, TN_ADA), lambda j: (0, j)),
            pl.BlockSpec((1, TN_ADA), lambda j: (0, j)),
        ],
        out_specs=pl.BlockSpec((b, TN_ADA), lambda j: (0, j)),
        out_shape=jax.ShapeDtypeStruct((b, n), F32),
        compiler_params=pltpu.CompilerParams(
            dimension_semantics=("arbitrary",), vmem_limit_bytes=VMEM_LIMIT),
        name="adaln_mod",
    )(c, w_ada, b_ada.reshape(1, n))


def _ffn_kernel(x_hbm, mod_ref, modn_ref, g_ref, wgu_hbm, wd_hbm, gfin_ref, *rest,
                sub, n_tiles, n_chunks, final_norm, n_casts):
    cast_in, rest = rest[:n_casts], rest[n_casts:]
    o_hbm, cast_out = rest[0], rest[1:1 + n_casts]
    (xbuf, h_ref, hn_ref, acc_ref, obuf, wg_buf, wu_buf, wd_buf,
     xsem, wsem, osem) = rest[1 + n_casts:]
    i = pl.program_id(0)
    tm = h_ref.shape[0]
    tf = wd_buf.shape[1] // GROUP_CHUNKS
    dff = n_chunks * tf
    cur = i % 2
    nxt = 1 - cur
    n_groups = -(-n_chunks // GROUP_CHUNKS)
    norm_groups = min(n_groups - 1, 4)
    norm_rows = tm // norm_groups
    assert n_groups % 2 == 0 and norm_rows * norm_groups == tm and norm_rows % BF16_ROWS == 0

    def x_copy(tile, slot):
        start = tile * tm if isinstance(tile, int) else pl.multiple_of(tile * tm, tm)
        return pltpu.make_async_copy(x_hbm.at[pl.ds(start, tm), :], xbuf.at[slot], xsem.at[slot])

    def out_copy(tile):
        return pltpu.make_async_copy(obuf, o_hbm.at[pl.ds(pl.multiple_of(tile * tm, tm), tm), :],
                                     osem.at[0])

    def w_copies(group):
        slot = group % 2
        col = group * GROUP_CHUNKS * tf
        width = min(GROUP_CHUNKS, n_chunks - group * GROUP_CHUNKS) * tf
        return (
            pltpu.make_async_copy(wgu_hbm.at[:, pl.ds(col, width)],
                                  wg_buf.at[slot, :, pl.ds(0, width)], wsem.at[slot]),
            pltpu.make_async_copy(wgu_hbm.at[:, pl.ds(dff + col, width)],
                                  wu_buf.at[slot, :, pl.ds(0, width)], wsem.at[2 + slot]),
            pltpu.make_async_copy(wd_hbm.at[pl.ds(col, width), :],
                                  wd_buf.at[slot, pl.ds(0, width), :], wsem.at[4 + slot]),
        )

    @pl.when(i == 0)
    def _():
        x_copy(0, 0).start()
        for c in w_copies(0):
            c.start()
        x_copy(0, 0).wait()
        h_ref[...] = _modulated_norm(xbuf[0], g_ref, mod_ref, sub)

    @pl.when(i > 0)
    def _():
        h_ref[...] = hn_ref[...]

    next_tile = jnp.minimum(i + 1, n_tiles - 1)
    x_copy(next_tile, nxt).start()
    _cast_blocks(cast_in, cast_out)

    for group in range(n_groups):
        slot = group % 2
        for c in w_copies(group):
            c.wait()
        for c in w_copies((group + 1) % n_groups):
            c.start()
        if group == n_groups - norm_groups:
            x_copy(next_tile, nxt).wait()
        for k in range(min(GROUP_CHUNKS, n_chunks - group * GROUP_CHUNKS)):
            cols = slice(k * tf, (k + 1) * tf)
            h = h_ref[...]
            gate = jnp.dot(h, wg_buf[slot, :, cols], preferred_element_type=F32)
            up = jnp.dot(h, wu_buf[slot, :, cols], preferred_element_type=F32)
            act = (_silu(gate) * up).astype(BF16)
            down = jnp.dot(act, wd_buf[slot, cols, :], preferred_element_type=F32)
            if group == 0 and k == 0:
                acc_ref[...] = down
            else:
                acc_ref[...] += down
        if group >= n_groups - norm_groups:
            rows = pl.ds((group - (n_groups - norm_groups)) * norm_rows, norm_rows)
            hn_ref[rows, :] = _modulated_norm(xbuf[nxt, rows, :], g_ref, modn_ref, sub)

    @pl.when(i == n_tiles - 1)
    def _():
        for c in w_copies(0):
            c.wait()

    @pl.when(i > 0)
    def _():
        out_copy(i - 1).wait()

    gmod = mod_ref[0, pl.ds(3 * sub + 2, 1), :]
    out = xbuf[cur] + (FFN_RES * gmod) * acc_ref[...]
    if final_norm:
        out = _rms(out, gfin_ref[...])
    obuf[...] = out
    out_copy(i).start()

    @pl.when(i == n_tiles - 1)
    def _():
        out_copy(i).wait()


def _cast_specs(weights, n_steps, step=lambda i: i):
    specs = []
    for w in weights:
        rows, cols = w.shape
        assert rows % (n_steps * BF16_ROWS) == 0
        specs.append(pl.BlockSpec((rows // n_steps, cols), lambda *ids: (step(*ids), 0)))
    return specs


def _cast_shapes(weights):
    return [jax.ShapeDtypeStruct(w.shape, BF16) for w in weights]


def _cast_blocks(cast_in, cast_out):
    for src, dst in zip(cast_in, cast_out):
        dst[...] = src[...].astype(BF16)


def _ffn_call(x, mod, g, w_gu, w_down, g_final, *, sub, seq, final_norm, cast_weights=()):
    t, d = x.shape
    dff = w_down.shape[0]
    tm, tf = TM_FFN, TF_FFN
    n_tiles, n_chunks = t // tm, dff // tf
    tiles_per_batch = seq // tm
    group = GROUP_CHUNKS * tf
    nxt = lambda i: jnp.minimum(i + 1, n_tiles - 1)
    const = lambda i: (0, 0)
    kern = functools.partial(_ffn_kernel, sub=sub, n_tiles=n_tiles, n_chunks=n_chunks,
                             final_norm=final_norm, n_casts=len(cast_weights))
    return pl.pallas_call(
        kern,
        grid=(n_tiles,),
        in_specs=[
            pl.BlockSpec(memory_space=pl.ANY),
            pl.BlockSpec((1, N_SUB * N_MOD, d), lambda i: (i // tiles_per_batch, 0, 0)),
            pl.BlockSpec((1, N_SUB * N_MOD, d), lambda i: (nxt(i) // tiles_per_batch, 0, 0)),
            pl.BlockSpec((1, d), const),
            pl.BlockSpec(memory_space=pl.ANY),
            pl.BlockSpec(memory_space=pl.ANY),
            pl.BlockSpec((1, d), const),
        ] + _cast_specs(cast_weights, n_tiles),
        out_specs=[pl.BlockSpec(memory_space=pl.ANY)] + _cast_specs(cast_weights, n_tiles),
        out_shape=[jax.ShapeDtypeStruct((t, d), F32)] + _cast_shapes(cast_weights),
        scratch_shapes=[
            pltpu.VMEM((2, tm, d), F32),
            pltpu.VMEM((tm, d), BF16),
            pltpu.VMEM((tm, d), BF16),
            pltpu.VMEM((tm, d), F32),
            pltpu.VMEM((tm, d), F32),
            pltpu.VMEM((2, d, group), BF16),
            pltpu.VMEM((2, d, group), BF16),
            pltpu.VMEM((2, group, d), BF16),
            pltpu.SemaphoreType.DMA((2,)),
            pltpu.SemaphoreType.DMA((6,)),
            pltpu.SemaphoreType.DMA((1,)),
        ],
        compiler_params=pltpu.CompilerParams(
            dimension_semantics=("arbitrary",), vmem_limit_bytes=VMEM_LIMIT),
        name="ffn_sub%d" % sub,
    )(x, mod, mod, g.reshape(1, d), w_gu, w_down, g_final.reshape(1, d), *cast_weights)


def _packed_positions(positions):
    per_row = LANES // ROT_DIM
    return jnp.repeat(positions.reshape(-1, per_row), ROT_DIM, axis=1)


def _rope_tables(pos_ref, invf_ref, unpack_ref):
    per_row = LANES // ROT_DIM
    lane = lax.broadcasted_iota(jnp.int32, (1, LANES), 1)
    d = lane % HEAD_DIM
    ang = pos_ref[...].astype(F32) * invf_ref[...]
    n_packed = ang.shape[0]
    for t, packed in enumerate((jnp.cos(ang), jnp.sin(ang))):
        for u in range(per_row):
            shifted = packed if u == 0 else pltpu.roll(packed, LANES - ROT_DIM * u, 1)
            unpack_ref[t, pl.ds(u, n_packed, stride=per_row), :] = shifted

    def both_heads(a):
        a = jnp.where(lane < ROT_DIM, a, 0.0)
        return a + pltpu.roll(a, HEAD_DIM, 1)

    c, s = both_heads(unpack_ref[0]), both_heads(unpack_ref[1])
    half = ROT_DIM // 2
    coef_self = jnp.where(d < ROT_DIM, c, 1.0)
    coef_next = jnp.where(d < half, -s, 0.0)
    coef_prev = jnp.where((d >= half) & (d < ROT_DIM), s, 0.0)
    return coef_self, coef_next, coef_prev


def _rope_slab(z, tables):
    coef_self, coef_next, coef_prev = tables
    half = ROT_DIM // 2
    nxt = pltpu.roll(z, LANES - half, axis=1)
    prv = pltpu.roll(z, half, axis=1)
    return z * coef_self + nxt * coef_next + prv * coef_prev


def _inproj_kernel(x_ref, mod_ref, g_ref, w_ref, b_ref, pos_ref, invf_ref, *rest, sub, n_casts):
    cast_in, rest = rest[:n_casts], rest[n_casts:]
    qa_ref, ka_ref, va_ref, qb_ref, kb_ref, vb_ref = rest[:6]
    _cast_blocks(cast_in, rest[6:6 + n_casts])
    h = _modulated_norm(x_ref[...], g_ref, mod_ref, sub)
    tables = _rope_tables(pos_ref, invf_ref, rest[6 + n_casts])
    lane = lax.broadcasted_iota(jnp.int32, (1, LANES), 1)
    lo = lane < HEAD_DIM
    qscale = HEAD_DIM ** -0.5 * LOG2_E

    def proj_slabs(col, width):
        for c0 in range(0, width, MXU_COLS):
            z = (jnp.dot(h, w_ref[:, col + c0:col + c0 + MXU_COLS], preferred_element_type=F32)
                 + b_ref[:, col + c0:col + c0 + MXU_COLS])
            for half in range(MXU_COLS // LANES):
                yield c0 // LANES + half, z[:, half * LANES:(half + 1) * LANES]

    def dup_heads(slab):
        swapped = pltpu.roll(slab, HEAD_DIM, axis=1)
        return jnp.where(lo, slab, swapped), jnp.where(lo, swapped, slab)

    col = 0
    for k, z in proj_slabs(col, WIDTH_A):
        qa_ref[:, k * LANES:(k + 1) * LANES] = (_rope_slab(z, tables) * qscale).astype(BF16)
    col += WIDTH_A
    for k, z in proj_slabs(col, 2 * WIDTH_KV_A):
        dst = ka_ref if k == 0 else va_ref
        h0, h1 = dup_heads(_rope_slab(z, tables) if k == 0 else z)
        dst[:, 0:LANES] = h0.astype(BF16)
        dst[:, LANES:2 * LANES] = h1.astype(BF16)
    col += 2 * WIDTH_KV_A
    for k, z in proj_slabs(col, WIDTH_B):
        qb_ref[:, k * LANES:(k + 1) * LANES] = _rope_slab(z, tables) * qscale
    col += WIDTH_B
    for k, z in proj_slabs(col, WIDTH_B):
        kb_ref[:, k * LANES:(k + 1) * LANES] = _rope_slab(z, tables)
    col += WIDTH_B
    for k, z in proj_slabs(col, WIDTH_B):
        vb_ref[:, k * LANES:(k + 1) * LANES] = z


def _inproj_call(x, mod, g, w_in, b_in, pos, invf, *, sub, seq, cast_weights=()):
    t, d = x.shape
    tm = TM_PROJ
    tiles_per_batch = seq // tm
    row = lambda i: (i, 0)
    const = lambda i: (0, 0)
    return pl.pallas_call(
        functools.partial(_inproj_kernel, sub=sub, n_casts=len(cast_weights)),
        grid=(t // tm,),
        in_specs=[
            pl.BlockSpec((tm, d), row),
            pl.BlockSpec((1, N_SUB * N_MOD, d), lambda i: (i // tiles_per_batch, 0, 0)),
            pl.BlockSpec((1, d), const),
            pl.BlockSpec((d, IN_WIDTH), const, pipeline_mode=pl.Buffered(1)),
            pl.BlockSpec((1, IN_WIDTH), const),
            pl.BlockSpec((tm * ROT_DIM // LANES, LANES), row),
            pl.BlockSpec((1, LANES), const),
        ] + _cast_specs(cast_weights, t // tm),
        out_specs=[
            pl.BlockSpec((tm, WIDTH_A), row),
            pl.BlockSpec((tm, 2 * LANES), row),
            pl.BlockSpec((tm, 2 * LANES), row),
            pl.BlockSpec((tm, WIDTH_B), row),
            pl.BlockSpec((tm, WIDTH_B), row),
            pl.BlockSpec((tm, WIDTH_B), row),
        ] + _cast_specs(cast_weights, t // tm),
        out_shape=[
            jax.ShapeDtypeStruct((t, WIDTH_A), BF16),
            jax.ShapeDtypeStruct((t, 2 * LANES), BF16),
            jax.ShapeDtypeStruct((t, 2 * LANES), BF16),
            jax.ShapeDtypeStruct((t, WIDTH_B), F32),
            jax.ShapeDtypeStruct((t, WIDTH_B), F32),
            jax.ShapeDtypeStruct((t, WIDTH_B), F32),
        ] + _cast_shapes(cast_weights),
        scratch_shapes=[pltpu.VMEM((2, tm, LANES), F32)],
        compiler_params=pltpu.CompilerParams(
            dimension_semantics=("arbitrary",), vmem_limit_bytes=VMEM_LIMIT),
        name="inproj_rope",
    )(x, mod, g.reshape(1, d), w_in, b_in.reshape(1, IN_WIDTH), pos, invf, *cast_weights)


def _attn_a_kernel(sink_ref, q_ref, kp_ref, kc_ref, vp_ref, vc_ref, o_ref, t_sc, es_sc):
    i = pl.program_id(1)
    lane = lax.broadcasted_iota(jnp.int32, (1, LANES), 1)
    lo = lane < HEAD_DIM
    qi = lax.broadcasted_iota(jnp.int32, (BLOCK, 2 * BLOCK), 0)
    kj = lax.broadcasted_iota(jnp.int32, (BLOCK, 2 * BLOCK), 1)
    dist = qi + BLOCK - kj
    band = (dist >= 0) & (dist <= WINDOW_A - 1)
    bias_inner = jnp.where(band, 0.0, NEG_BIG).astype(F32)
    bias_first = jnp.where(band & ((kj >= BLOCK) | (i > 0)), 0.0, NEG_BIG).astype(F32)
    zero = jnp.zeros((BLOCK, LANES), BF16)
    ones = jnp.ones((2 * BLOCK, LANES), BF16)
    pairs = GROUP_A // 2
    units = [(u, hk) for u in range(QBLOCKS_A) for hk in range(N_KV_A)]

    def keys_values(ref_prev, ref_cur, u, hk):
        r0 = u * BLOCK
        cols = slice(hk * LANES, (hk + 1) * LANES)
        prev = ref_prev[:, cols] if u == 0 else ref_cur[r0 - BLOCK:r0, cols]
        return jnp.concatenate([prev, ref_cur[r0:r0 + BLOCK, cols]], axis=0)

    def scores(n):
        u, hk = units[n]
        r0, slot = u * BLOCK, n % 2
        rows = []
        for jp in range(pairs):
            c0 = (hk * pairs + jp) * LANES
            qp = q_ref[r0:r0 + BLOCK, c0:c0 + LANES]
            rows.append(jnp.where(lo, qp, zero))
            rows.append(jnp.where(lo, zero, qp))
        lhs = jnp.concatenate(rows, axis=0)
        s_all = lax.dot_general(lhs, keys_values(kp_ref, kc_ref, u, hk), (((1,), (1,)), ((), ())),
                                preferred_element_type=F32)
        bias = bias_first if u == 0 else bias_inner
        for g in range(GROUP_A):
            sink = sink_ref[hk * GROUP_A + g] * LOG2_E
            s = s_all[g * BLOCK:(g + 1) * BLOCK, :] + bias
            m = jnp.maximum(jnp.max(s, axis=-1, keepdims=True), sink)
            t_sc[slot, g * BLOCK:(g + 1) * BLOCK, :] = s - m
            es_sc[slot, g] = jnp.broadcast_to(jnp.exp2(sink - m), (BLOCK, LANES))

    def values(n):
        u, hk = units[n]
        r0, slot = u * BLOCK, n % 2
        v3 = jnp.concatenate([keys_values(vp_ref, vc_ref, u, hk), ones], axis=1)
        pv = jnp.dot(jnp.exp2(t_sc[slot]).astype(BF16), v3, preferred_element_type=F32)
        outs = []
        for g in range(GROUP_A):
            blk = pv[g * BLOCK:(g + 1) * BLOCK, :]
            outs.append(blk[:, :LANES] * (1.0 / (blk[:, LANES:] + es_sc[slot, g])))
        for jp in range(pairs):
            c0 = (hk * pairs + jp) * LANES
            o_ref[r0:r0 + BLOCK, c0:c0 + LANES] = jnp.where(lo, outs[2 * jp], outs[2 * jp + 1])

    scores(0)
    for n in range(len(units)):
        values(n)
        if n + 1 < len(units):
            scores(n + 1)


def _attn_a_call(sinks, qa, ka, va, *, batch, seq):
    nb = seq // BLOCK
    rows = QBLOCKS_A * BLOCK
    steps = nb // QBLOCKS_A
    cur = lambda b, i: (b * steps + i, 0)
    prev = lambda b, i: (b * nb + jnp.maximum(QBLOCKS_A * i - 1, 0), 0)
    return pl.pallas_call(
        _attn_a_kernel,
        grid=(batch, steps),
        in_specs=[
            pl.BlockSpec(memory_space=pltpu.SMEM),
            pl.BlockSpec((rows, WIDTH_A), cur),
            pl.BlockSpec((BLOCK, 2 * LANES), prev),
            pl.BlockSpec((rows, 2 * LANES), cur),
            pl.BlockSpec((BLOCK, 2 * LANES), prev),
            pl.BlockSpec((rows, 2 * LANES), cur),
        ],
        out_specs=pl.BlockSpec((rows, WIDTH_A), cur),
        out_shape=jax.ShapeDtypeStruct((batch * seq, WIDTH_A), F32),
        scratch_shapes=[pltpu.VMEM((2, GROUP_A * BLOCK, 2 * BLOCK), F32),
                        pltpu.VMEM((2, GROUP_A, BLOCK, LANES), F32)],
        compiler_params=pltpu.CompilerParams(
            dimension_semantics=("arbitrary", "arbitrary"), vmem_limit_bytes=VMEM_LIMIT),
        name="attn_swa_gqa",
    )(sinks, qa, ka, ka, va, va)


def _attn_b_kernel(q_ref, k_ref, v_ref, *rest, seq, n_casts):
    cast_in, rest = rest[:n_casts], rest[n_casts:]
    o_ref, cast_out = rest[0], rest[1:1 + n_casts]
    x4_ref, t_sc, m_sc, l_sc, acc_sc = rest[1 + n_casts:]
    _cast_blocks(cast_in, cast_out)
    (_, d1), (_, d2), (_, d3) = DILATED
    assert d1 == 1 and d3 == d2 * d2 and all(w // dil == BLOCK for w, dil in DILATED)
    n_iter = seq // BLOCK
    cls = seq // d2
    nblk2 = cls // BLOCK
    assert seq // d3 == BLOCK and n_iter == d2 * nblk2 == d3

    lane = lax.broadcasted_iota(jnp.int32, (1, LANES), 1)
    lo = lane < HEAD_DIM
    qi = lax.broadcasted_iota(jnp.int32, (BLOCK, BLOCK), 0)
    kj = lax.broadcasted_iota(jnp.int32, (BLOCK, BLOCK), 1)
    bias_cur = jnp.where(kj <= qi, 0.0, NEG_BIG).astype(F32)
    bias_prev = jnp.where(kj >= qi, 0.0, NEG_BIG).astype(F32)
    zero = jnp.zeros((BLOCK, LANES), BF16)
    ones = jnp.ones((2 * BLOCK, LANES), BF16)

    for a, ref in enumerate((q_ref, k_ref, v_ref)):
        for r in range(d2):
            x4_ref[a, r * cls:(r + 1) * cls, :] = ref[0, pl.ds(r, cls, stride=d2), :]

    natural = lambda a, rows: (q_ref, k_ref, v_ref)[a][0, rows, :]
    by_class = lambda a, rows: x4_ref[a, rows, :]

    def block_rows(cfg, idx):
        static = isinstance(idx, int)
        block = lambda start: pl.ds(start if static else pl.multiple_of(start, BLOCK), BLOCK)
        biggest = max if static else jnp.maximum
        start = idx * BLOCK
        own = block(start)
        if cfg == 0:
            return natural, own, block(biggest(start - BLOCK, 0)), idx > 0
        if cfg == 1:
            n = idx % nblk2
            return by_class, own, block(biggest(start - BLOCK, start - n * BLOCK)), n > 0
        return by_class, pl.ds((idx % d2) * cls + idx // d2, BLOCK, stride=d2), None, None

    def scores(cfg, idx, slot):
        load, own, prev, has_prev = block_rows(cfg, idx)
        qb = load(0, own).astype(BF16)
        lhs = jnp.concatenate([jnp.where(lo, qb, zero), jnp.where(lo, zero, qb)], axis=0)
        if prev is None:
            keys, bias = load(1, own), bias_cur
        else:
            keys = jnp.concatenate([load(1, prev), load(1, own)], axis=0)
            off = jnp.where(has_prev, 0.0, NEG_BIG).astype(F32)
            bias = jnp.concatenate([bias_prev + off, bias_cur], axis=1)
        nk = keys.shape[0]
        s = lax.dot_general(lhs, keys.astype(BF16), (((1,), (1,)), ((), ())),
                            preferred_element_type=F32)
        s0 = s[:BLOCK, :] + bias
        s1 = s[BLOCK:, :] + bias
        m0 = jnp.max(s0, axis=-1, keepdims=True)
        m1 = jnp.max(s1, axis=-1, keepdims=True)
        t_sc[cfg, slot, :BLOCK, :nk] = s0 - m0
        t_sc[cfg, slot, BLOCK:, :nk] = s1 - m1
        m_sc[cfg, own, :] = jnp.where(lo, m0, m1)

    def values(cfg, idx, slot):
        load, own, prev, _ = block_rows(cfg, idx)
        vals = load(2, own) if prev is None else jnp.concatenate([load(2, prev), load(2, own)], axis=0)
        nk = vals.shape[0]
        p = jnp.exp2(t_sc[cfg, slot, :, :nk]).astype(BF16)
        v3 = jnp.concatenate([vals.astype(BF16), ones[:nk, :]], axis=1)
        pv = jnp.dot(p, v3, preferred_element_type=F32)
        acc_sc[cfg, own, :] = jnp.where(lo, pv[:BLOCK, :LANES], pv[BLOCK:, :LANES])
        l_sc[cfg, own, :] = jnp.where(lo, pv[:BLOCK, LANES:], pv[BLOCK:, LANES:])

    n_cfg = len(DILATED)
    for cfg in range(n_cfg):
        scores(cfg, 0, 0)

    def body(k, carry):
        for half in range(2):
            b = 2 * k + half
            for cfg in range(n_cfg):
                values(cfg, b, half)
                scores(cfg, b + 1, 1 - half)
        return carry

    assert n_iter % 2 == 0
    lax.fori_loop(0, n_iter // 2 - 1, body, 0, unroll=UNROLL_B)
    for cfg in range(n_cfg):
        values(cfg, n_iter - 2, 0)
        scores(cfg, n_iter - 1, 1)
    for cfg in range(n_cfg):
        values(cfg, n_iter - 1, 1)

    def merge(i, carry):
        start = i * BLOCK
        xr = pl.ds(pl.multiple_of(start, BLOCK), BLOCK)
        nat = pl.ds(i // nblk2 + (i % nblk2) * (BLOCK * d2), BLOCK, stride=d2)
        sel = (nat, xr, xr)
        ms = [m_sc[c, sel[c], :] for c in range(len(DILATED))]
        m_all = functools.reduce(jnp.maximum, ms)
        ws = [jnp.exp2(m - m_all) for m in ms]
        l_all = sum(w * l_sc[c, sel[c], :] for c, w in enumerate(ws))
        acc_all = sum(w * acc_sc[c, sel[c], :] for c, w in enumerate(ws))
        o_ref[0, nat, :] = acc_all * (1.0 / l_all)
        return carry

    lax.fori_loop(0, n_iter, merge, 0, unroll=True)


def _attn_b_call(qb, kb, vb, *, batch, seq, cast_weights=()):
    spec = pl.BlockSpec((1, seq, LANES), lambda b, hp: (b, 0, hp))
    shape3 = (batch, seq, WIDTH_B)
    pairs = WIDTH_B // LANES
    cast_specs = _cast_specs(cast_weights, batch * pairs, lambda b, hp: b * pairs + hp)
    return pl.pallas_call(
        functools.partial(_attn_b_kernel, seq=seq, n_casts=len(cast_weights)),
        grid=(batch, pairs),
        in_specs=[spec, spec, spec] + cast_specs,
        out_specs=[spec] + cast_specs,
        out_shape=[jax.ShapeDtypeStruct(shape3, F32)] + _cast_shapes(cast_weights),
        scratch_shapes=[pltpu.VMEM((3, seq, LANES), F32),
                        pltpu.VMEM((len(DILATED), 2, 2 * BLOCK, 2 * BLOCK), F32)]
        + [pltpu.VMEM((len(DILATED), seq, LANES), F32)] * 3,
        compiler_params=pltpu.CompilerParams(
            dimension_semantics=("arbitrary", "arbitrary"), vmem_limit_bytes=VMEM_LIMIT),
        name="attn_dilated",
    )(qb.reshape(shape3), kb.reshape(shape3), vb.reshape(shape3), *cast_weights)


def _outproj_kernel(oa_ref, ob_ref, x_ref, mod_ref, ga_ref, gb_ref, w_ref, b_ref, *rest,
                    sub, n_casts):
    o_ref = rest[n_casts]
    _cast_blocks(rest[:n_casts], rest[n_casts + 1:])
    ya = _rms(oa_ref[...], ga_ref[...]).astype(BF16)
    yb = _rms(ob_ref[...], gb_ref[...]).astype(BF16)
    y = (jnp.dot(ya, w_ref[0:WIDTH_A, :], preferred_element_type=F32)
         + jnp.dot(yb, w_ref[WIDTH_A:WIDTH_A + WIDTH_B, :], preferred_element_type=F32)
         + b_ref[...])
    gmod = mod_ref[0, pl.ds(3 * sub + 2, 1), :]
    o_ref[...] = x_ref[...] + gmod * y


def _outproj_call(out_a, out_b, x, mod, g_a, g_b, w_out, b_out, *, sub, seq, cast_weights=()):
    t, d = x.shape
    tm = TM_PROJ
    tiles_per_batch = seq // tm
    row = lambda i: (i, 0)
    const = lambda i: (0, 0)
    return pl.pallas_call(
        functools.partial(_outproj_kernel, sub=sub, n_casts=len(cast_weights)),
        grid=(t // tm,),
        in_specs=[
            pl.BlockSpec((tm, WIDTH_A), row),
            pl.BlockSpec((tm, WIDTH_B), row),
            pl.BlockSpec((tm, d), row),
            pl.BlockSpec((1, N_SUB * N_MOD, d), lambda i: (i // tiles_per_batch, 0, 0)),
            pl.BlockSpec((1, WIDTH_A), const),
            pl.BlockSpec((1, WIDTH_B), const),
            pl.BlockSpec((WIDTH_A + WIDTH_B, d), const),
            pl.BlockSpec((1, d), const),
        ] + _cast_specs(cast_weights, t // tm),
        out_specs=[pl.BlockSpec((tm, d), row)] + _cast_specs(cast_weights, t // tm),
        out_shape=[jax.ShapeDtypeStruct((t, d), F32)] + _cast_shapes(cast_weights),
        compiler_params=pltpu.CompilerParams(
            dimension_semantics=("arbitrary",), vmem_limit_bytes=VMEM_LIMIT),
        name="outproj_residual",
    )(out_a, out_b, x, mod, g_a.reshape(1, WIDTH_A), g_b.reshape(1, WIDTH_B), w_out,
      b_out.reshape(1, d), *cast_weights)


def kernel(x, c, positions, w_ada, b_ada, g_ffn1, w_ffn1_in, w_ffn1_out, g_mix, w_in, b_in, sinks,
           g_out_a, g_out_b, w_out, b_out, g_ffn2, w_ffn2_in, w_ffn2_out, g_final):
    batch, seq, d = x.shape
    depth = w_ada.shape[0]
    t = batch * seq
    xt = x.reshape(t, d)
    pos = _packed_positions(positions.reshape(t))
    lane_dim = jnp.arange(LANES, dtype=jnp.int32) % (ROT_DIM // 2)
    invf = (ROPE_THETA ** (-(2.0 * lane_dim.astype(F32)) / ROT_DIM)).reshape(1, LANES)
    for layer in range(depth):
        mod = _ada_call(c, w_ada[layer], b_ada[layer]).reshape(batch, N_SUB * N_MOD, d)
        last = layer == depth - 1
        xt, w_in_bf = _ffn_call(
            xt, mod, g_ffn1[layer], w_ffn1_in[layer].astype(BF16), w_ffn1_out[layer].astype(BF16),
            g_final, sub=0, seq=seq, final_norm=False, cast_weights=(w_in[layer],))
        qa, ka, va, qb, kb, vb, w_out_bf, w_down2_bf = _inproj_call(
            xt, mod, g_mix[layer], w_in_bf, b_in[layer], pos, invf, sub=1, seq=seq,
            cast_weights=(w_out[layer], w_ffn2_out[layer]))
        out_a = _attn_a_call(sinks[layer], qa, ka, va, batch=batch, seq=seq)
        out_b, w_gu2_bf = _attn_b_call(qb, kb, vb, batch=batch, seq=seq,
                                       cast_weights=(w_ffn2_in[layer],))
        xt, = _outproj_call(out_a, out_b.reshape(t, WIDTH_B), xt, mod, g_out_a[layer],
                            g_out_b[layer], w_out_bf, b_out[layer], sub=1, seq=seq)
        xt, = _ffn_call(xt, mod, g_ffn2[layer], w_gu2_bf, w_down2_bf, g_final, sub=2, seq=seq,
                        final_norm=last)
    if depth == 0:
        raise ValueError("depth must be >= 1")
    return xt.reshape(batch, seq, d)
```

```python
import functools

import jax
import jax.numpy as jnp
from jax import lax
from jax.experimental import pallas as pl
from jax.experimental.pallas import tpu as pltpu

F32 = jnp.float32
BF16 = jnp.bfloat16

D_MODEL = 2048
HEAD_DIM = 64
N_HEADS_A = 16
N_KV_A = 2
GROUP_A = N_HEADS_A // N_KV_A
N_HEADS_B = 16
WIDTH_A = N_HEADS_A * HEAD_DIM
WIDTH_KV_A = N_KV_A * HEAD_DIM
WIDTH_B = N_HEADS_B * HEAD_DIM
IN_WIDTH = WIDTH_A + 2 * WIDTH_KV_A + 3 * WIDTH_B
WINDOW_A = 128
DILATED = ((128, 1), (512, 4), (2048, 16))
BLOCK = 128
ROPE_THETA = 500000.0
ROT_DIM = HEAD_DIM // 4
D_FF = 5632
FFN_RES = 0.5
N_SUB = 3
N_MOD = 3
EPS = 1e-5

LANES = 128
MXU_COLS = 256
NEG_BIG = -1e30
LOG2_E = 1.4426950408889634
VMEM_LIMIT = 56 * 1024 * 1024

TM_FFN = 512
TF_FFN = 512
TM_PROJ = 512
TN_ADA = 1024
GROUP_CHUNKS = 2
BF16_ROWS = 16
QBLOCKS_A = 16
UNROLL_B = 7


def _silu(x):
    return x * (1.0 / (1.0 + jnp.exp(-x)))


def _rms(x, g):
    ms = jnp.mean(x * x, axis=-1, keepdims=True)
    return (x * lax.rsqrt(ms + EPS)) * g


def _modulated_norm(x, g_ref, mod_ref, sub):
    shift = mod_ref[0, pl.ds(3 * sub + 0, 1), :]
    scale = mod_ref[0, pl.ds(3 * sub + 1, 1), :]
    return (_rms(x, g_ref[...] * (1.0 + scale)) + shift).astype(BF16)


def _ada_kernel(c_ref, w_ref, b_ref, o_ref):
    cond = _silu(c_ref[...]).astype(BF16)
    o_ref[...] = jnp.dot(cond, w_ref[...].astype(BF16), preferred_element_type=F32) + b_ref[...]


def _ada_call(c, w_ada, b_ada):
    b, d = c.shape
    n = w_ada.shape[1]
    return pl.pallas_call(
        _ada_kernel,
        grid=(n // TN_ADA,),
        in_specs=[
            pl.BlockSpec((b, d), lambda j: (0, 0)),
            pl.BlockSpec((d, TN_ADA), lambda j: (0, j)),
            pl.BlockSpec((1, TN_ADA), lambda j: (0, j)),
        ],
        out_specs=pl.BlockSpec((b, TN_ADA), lambda j: (0, j)),
        out_shape=jax.ShapeDtypeStruct((b, n), F32),
        compiler_params=pltpu.CompilerParams(
            dimension_semantics=("arbitrary",), vmem_limit_bytes=VMEM_LIMIT),
        name="adaln_mod",
    )(c, w_ada, b_ada.reshape(1, n))


def _ffn_kernel(x_hbm, mod_ref, modn_ref, g_ref, wgu_hbm, wd_hbm, gfin_ref, *rest,
                sub, n_tiles, n_chunks, final_norm, n_casts):
    cast_in, rest = rest[:n_casts], rest[n_casts:]
    o_hbm, cast_out = rest[0], rest[1:1 + n_casts]
    (xbuf, h_ref, hn_ref, acc_ref, obuf, wg_buf, wu_buf, wd_buf,
     xsem, wsem, osem) = rest[1 + n_casts:]
    i = pl.program_id(0)
    tm = h_ref.shape[0]
    tf = wd_buf.shape[1] // GROUP_CHUNKS
    dff = n_chunks * tf
    cur = i % 2
    nxt = 1 - cur
    n_groups = -(-n_chunks // GROUP_CHUNKS)
    norm_groups = min(n_groups - 1, 4)
    norm_rows = tm // norm_groups
    assert n_groups % 2 == 0 and norm_rows * norm_groups == tm and norm_rows % BF16_ROWS == 0

    def x_copy(tile, slot):
        start = tile * tm if isinstance(tile, int) else pl.multiple_of(tile * tm, tm)
        return pltpu.make_async_copy(x_hbm.at[pl.ds(start, tm), :], xbuf.at[slot], xsem.at[slot])

    def out_copy(tile):
        return pltpu.make_async_copy(obuf, o_hbm.at[pl.ds(pl.multiple_of(tile * tm, tm), tm), :],
                                     osem.at[0])

    def w_copies(group):
        slot = group % 2
        col = group * GROUP_CHUNKS * tf
        width = min(GROUP_CHUNKS, n_chunks - group * GROUP_CHUNKS) * tf
        return (
            pltpu.make_async_copy(wgu_hbm.at[:, pl.ds(col, width)],
                                  wg_buf.at[slot, :, pl.ds(0, width)], wsem.at[slot]),
            pltpu.make_async_copy(wgu_hbm.at[:, pl.ds(dff + col, width)],
                                  wu_buf.at[slot, :, pl.ds(0, width)], wsem.at[2 + slot]),
            pltpu.make_async_copy(wd_hbm.at[pl.ds(col, width), :],
                                  wd_buf.at[slot, pl.ds(0, width), :], wsem.at[4 + slot]),
        )

    @pl.when(i == 0)
    def _():
        x_copy(0, 0).start()
        for c in w_copies(0):
            c.start()
        x_copy(0, 0).wait()
        h_ref[...] = _modulated_norm(xbuf[0], g_ref, mod_ref, sub)

    @pl.when(i > 0)
    def _():
        h_ref[...] = hn_ref[...]

    next_tile = jnp.minimum(i + 1, n_tiles - 1)
    x_copy(next_tile, nxt).start()
    _cast_blocks(cast_in, cast_out)

    for group in range(n_groups):
        slot = group % 2
        for c in w_copies(group):
            c.wait()
        for c in w_copies((group + 1) % n_groups):
            c.start()
        if group == n_groups - norm_groups:
            x_copy(next_tile, nxt).wait()
        for k in range(min(GROUP_CHUNKS, n_chunks - group * GROUP_CHUNKS)):
            cols = slice(k * tf, (k + 1) * tf)
            h = h_ref[...]
            gate = jnp.dot(h, wg_buf[slot, :, cols], preferred_element_type=F32)
            up = jnp.dot(h, wu_buf[slot, :, cols], preferred_element_type=F32)
            act = (_silu(gate) * up).astype(BF16)
            down = jnp.dot(act, wd_buf[slot, cols, :], preferred_element_type=F32)
            if group == 0 and k == 0:
                acc_ref[...] = down
            else:
                acc_ref[...] += down
        if group >= n_groups - norm_groups:
            rows = pl.ds((group - (n_groups - norm_groups)) * norm_rows, norm_rows)
            hn_ref[rows, :] = _modulated_norm(xbuf[nxt, rows, :], g_ref, modn_ref, sub)

    @pl.when(i == n_tiles - 1)
    def _():
        for c in w_copies(0):
            c.wait()

    @pl.when(i > 0)
    def _():
        out_copy(i - 1).wait()

    gmod = mod_ref[0, pl.ds(3 * sub + 2, 1), :]
    out = xbuf[cur] + (FFN_RES * gmod) * acc_ref[...]
    if final_norm:
        out = _rms(out, gfin_ref[...])
    obuf[...] = out
    out_copy(i).start()

    @pl.when(i == n_tiles - 1)
    def _():
        out_copy(i).wait()


def _cast_specs(weights, n_steps, step=lambda i: i):
    specs = []
    for w in weights:
        rows, cols = w.shape
        assert rows % (n_steps * BF16_ROWS) == 0
        specs.append(pl.BlockSpec((rows // n_steps, cols), lambda *ids: (step(*ids), 0)))
    return specs


def _cast_shapes(weights):
    return [jax.ShapeDtypeStruct(w.shape, BF16) for w in weights]


def _cast_blocks(cast_in, cast_out):
    for src, dst in zip(cast_in, cast_out):
        dst[...] = src[...].astype(BF16)


def _ffn_call(x, mod, g, w_gu, w_down, g_final, *, sub, seq, final_norm, cast_weights=()):
    t, d = x.shape
    dff = w_down.shape[0]
    tm, tf = TM_FFN, TF_FFN
    n_tiles, n_chunks = t // tm, dff // tf
    tiles_per_batch = seq // tm
    group = GROUP_CHUNKS * tf
    nxt = lambda i: jnp.minimum(i + 1, n_tiles - 1)
    const = lambda i: (0, 0)
    kern = functools.partial(_ffn_kernel, sub=sub, n_tiles=n_tiles, n_chunks=n_chunks,
                             final_norm=final_norm, n_casts=len(cast_weights))
    return pl.pallas_call(
        kern,
        grid=(n_tiles,),
        in_specs=[
            pl.BlockSpec(memory_space=pl.ANY),
            pl.BlockSpec((1, N_SUB * N_MOD, d), lambda i: (i // tiles_per_batch, 0, 0)),
            pl.BlockSpec((1, N_SUB * N_MOD, d), lambda i: (nxt(i) // tiles_per_batch, 0, 0)),
            pl.BlockSpec((1, d), const),
            pl.BlockSpec(memory_space=pl.ANY),
            pl.BlockSpec(memory_space=pl.ANY),
            pl.BlockSpec((1, d), const),
        ] + _cast_specs(cast_weights, n_tiles),
        out_specs=[pl.BlockSpec(memory_space=pl.ANY)] + _cast_specs(cast_weights, n_tiles),
        out_shape=[jax.ShapeDtypeStruct((t, d), F32)] + _cast_shapes(cast_weights),
        scratch_shapes=[
            pltpu.VMEM((2, tm, d), F32),
            pltpu.VMEM((tm, d), BF16),
            pltpu.VMEM((tm, d), BF16),
            pltpu.VMEM((tm, d), F32),
            pltpu.VMEM((tm, d), F32),
            pltpu.VMEM((2, d, group), BF16),
            pltpu.VMEM((2, d, group), BF16),
            pltpu.VMEM((2, group, d), BF16),
            pltpu.SemaphoreType.DMA((2,)),
            pltpu.SemaphoreType.DMA((6,)),
            pltpu.SemaphoreType.DMA((1,)),
        ],
        compiler_params=pltpu.CompilerParams(
            dimension_semantics=("arbitrary",), vmem_limit_bytes=VMEM_LIMIT),
        name="ffn_sub%d" % sub,
    )(x, mod, mod, g.reshape(1, d), w_gu, w_down, g_final.reshape(1, d), *cast_weights)


def _packed_positions(positions):
    per_row = LANES // ROT_DIM
    return jnp.repeat(positions.reshape(-1, per_row), ROT_DIM, axis=1)


def _rope_tables(pos_ref, invf_ref, unpack_ref):
    per_row = LANES // ROT_DIM
    lane = lax.broadcasted_iota(jnp.int32, (1, LANES), 1)
    d = lane % HEAD_DIM
    ang = pos_ref[...].astype(F32) * invf_ref[...]
    n_packed = ang.shape[0]
    for t, packed in enumerate((jnp.cos(ang), jnp.sin(ang))):
        for u in range(per_row):
            shifted = packed if u == 0 else pltpu.roll(packed, LANES - ROT_DIM * u, 1)
            unpack_ref[t, pl.ds(u, n_packed, stride=per_row), :] = shifted

    def both_heads(a):
        a = jnp.where(lane < ROT_DIM, a, 0.0)
        return a + pltpu.roll(a, HEAD_DIM, 1)

    c, s = both_heads(unpack_ref[0]), both_heads(unpack_ref[1])
    half = ROT_DIM // 2
    coef_self = jnp.where(d < ROT_DIM, c, 1.0)
    coef_next = jnp.where(d < half, -s, 0.0)
    coef_prev = jnp.where((d >= half) & (d < ROT_DIM), s, 0.0)
    return coef_self, coef_next, coef_prev


def _rope_slab(z, tables):
    coef_self, coef_next, coef_prev = tables
    half = ROT_DIM // 2
    nxt = pltpu.roll(z, LANES - half, axis=1)
    prv = pltpu.roll(z, half, axis=1)
    return z * coef_self + nxt * coef_next + prv * coef_prev


def _inproj_kernel(x_ref, mod_ref, g_ref, w_ref, b_ref, pos_ref, invf_ref, *rest, sub, n_casts):
    cast_in, rest = rest[:n_casts], rest[n_casts:]
    qa_ref, ka_ref, va_ref, qb_ref, kb_ref, vb_ref = rest[:6]
    _cast_blocks(cast_in, rest[6:6 + n_casts])
    h = _modulated_norm(x_ref[...], g_ref, mod_ref, sub)
    tables = _rope_tables(pos_ref, invf_ref, rest[6 + n_casts])
    lane = lax.broadcasted_iota(jnp.int32, (1, LANES), 1)
    lo = lane < HEAD_DIM
    qscale = HEAD_DIM ** -0.5 * LOG2_E

    def proj_slabs(col, width):
        for c0 in range(0, width, MXU_COLS):
            z = (jnp.dot(h, w_ref[:, col + c0:col + c0 + MXU_COLS], preferred_element_type=F32)
                 + b_ref[:, col + c0:col + c0 + MXU_COLS])
            for half in range(MXU_COLS // LANES):
                yield c0 // LANES + half, z[:, half * LANES:(half + 1) * LANES]

    def dup_heads(slab):
        swapped = pltpu.roll(slab, HEAD_DIM, axis=1)
        return jnp.where(lo, slab, swapped), jnp.where(lo, swapped, slab)

    col = 0
    for k, z in proj_slabs(col, WIDTH_A):
        qa_ref[:, k * LANES:(k + 1) * LANES] = (_rope_slab(z, tables) * qscale).astype(BF16)
    col += WIDTH_A
    for k, z in proj_slabs(col, 2 * WIDTH_KV_A):
        dst = ka_ref if k == 0 else va_ref
        h0, h1 = dup_heads(_rope_slab(z, tables) if k == 0 else z)
        dst[:, 0:LANES] = h0.astype(BF16)
        dst[:, LANES:2 * LANES] = h1.astype(BF16)
    col += 2 * WIDTH_KV_A
    for k, z in proj_slabs(col, WIDTH_B):
        qb_ref[:, k * LANES:(k + 1) * LANES] = _rope_slab(z, tables) * qscale
    col += WIDTH_B
    for k, z in proj_slabs(col, WIDTH_B):
        kb_ref[:, k * LANES:(k + 1) * LANES] = _rope_slab(z, tables)
    col += WIDTH_B
    for k, z in proj_slabs(col, WIDTH_B):
        vb_ref[:, k * LANES:(k + 1) * LANES] = z


def _inproj_call(x, mod, g, w_in, b_in, pos, invf, *, sub, seq, cast_weights=()):
    t, d = x.shape
    tm = TM_PROJ
    tiles_per_batch = seq // tm
    row = lambda i: (i, 0)
    const = lambda i: (0, 0)
    return pl.pallas_call(
        functools.partial(_inproj_kernel, sub=sub, n_casts=len(cast_weights)),
        grid=(t // tm,),
        in_specs=[
            pl.BlockSpec((tm, d), row),
            pl.BlockSpec((1, N_SUB * N_MOD, d), lambda i: (i // tiles_per_batch, 0, 0)),
            pl.BlockSpec((1, d), const),
            pl.BlockSpec((d, IN_WIDTH), const, pipeline_mode=pl.Buffered(1)),
            pl.BlockSpec((1, IN_WIDTH), const),
            pl.BlockSpec((tm * ROT_DIM // LANES, LANES), row),
            pl.BlockSpec((1, LANES), const),
        ] + _cast_specs(cast_weights, t // tm),
        out_specs=[
            pl.BlockSpec((tm, WIDTH_A), row),
            pl.BlockSpec((tm, 2 * LANES), row),
            pl.BlockSpec((tm, 2 * LANES), row),
            pl.BlockSpec((tm, WIDTH_B), row),
            pl.BlockSpec((tm, WIDTH_B), row),
            pl.BlockSpec((tm, WIDTH_B), row),
        ] + _cast_specs(cast_weights, t // tm),
        out_shape=[
            jax.ShapeDtypeStruct((t, WIDTH_A), BF16),
            jax.ShapeDtypeStruct((t, 2 * LANES), BF16),
            jax.ShapeDtypeStruct((t, 2 * LANES), BF16),
            jax.ShapeDtypeStruct((t, WIDTH_B), F32),
            jax.ShapeDtypeStruct((t, WIDTH_B), F32),
            jax.ShapeDtypeStruct((t, WIDTH_B), F32),
        ] + _cast_shapes(cast_weights),
        scratch_shapes=[pltpu.VMEM((2, tm, LANES), F32)],
        compiler_params=pltpu.CompilerParams(
            dimension_semantics=("arbitrary",), vmem_limit_bytes=VMEM_LIMIT),
        name="inproj_rope",
    )(x, mod, g.reshape(1, d), w_in, b_in.reshape(1, IN_WIDTH), pos, invf, *cast_weights)


def _attn_a_kernel(sink_ref, q_ref, kp_ref, kc_ref, vp_ref, vc_ref, o_ref, t_sc, es_sc):
    i = pl.program_id(1)
    lane = lax.broadcasted_iota(jnp.int32, (1, LANES), 1)
    lo = lane < HEAD_DIM
    qi = lax.broadcasted_iota(jnp.int32, (BLOCK, 2 * BLOCK), 0)
    kj = lax.broadcasted_iota(jnp.int32, (BLOCK, 2 * BLOCK), 1)
    dist = qi + BLOCK - kj
    band = (dist >= 0) & (dist <= WINDOW_A - 1)
    bias_inner = jnp.where(band, 0.0, NEG_BIG).astype(F32)
    bias_first = jnp.where(band & ((kj >= BLOCK) | (i > 0)), 0.0, NEG_BIG).astype(F32)
    zero = jnp.zeros((BLOCK, LANES), BF16)
    ones = jnp.ones((2 * BLOCK, LANES), BF16)
    pairs = GROUP_A // 2
    units = [(u, hk) for u in range(QBLOCKS_A) for hk in range(N_KV_A)]

    def keys_values(ref_prev, ref_cur, u, hk):
        r0 = u * BLOCK
        cols = slice(hk * LANES, (hk + 1) * LANES)
        prev = ref_prev[:, cols] if u == 0 else ref_cur[r0 - BLOCK:r0, cols]
        return jnp.concatenate([prev, ref_cur[r0:r0 + BLOCK, cols]], axis=0)

    def scores(n):
        u, hk = units[n]
        r0, slot = u * BLOCK, n % 2
        rows = []
        for jp in range(pairs):
            c0 = (hk * pairs + jp) * LANES
            qp = q_ref[r0:r0 + BLOCK, c0:c0 + LANES]
            rows.append(jnp.where(lo, qp, zero))
            rows.append(jnp.where(lo, zero, qp))
        lhs = jnp.concatenate(rows, axis=0)
        s_all = lax.dot_general(lhs, keys_values(kp_ref, kc_ref, u, hk), (((1,), (1,)), ((), ())),
                                preferred_element_type=F32)
        bias = bias_first if u == 0 else bias_inner
        for g in range(GROUP_A):
            sink = sink_ref[hk * GROUP_A + g] * LOG2_E
            s = s_all[g * BLOCK:(g + 1) * BLOCK, :] + bias
            m = jnp.maximum(jnp.max(s, axis=-1, keepdims=True), sink)
            t_sc[slot, g * BLOCK:(g + 1) * BLOCK, :] = s - m
            es_sc[slot, g] = jnp.broadcast_to(jnp.exp2(sink - m), (BLOCK, LANES))

    def values(n):
        u, hk = units[n]
        r0, slot = u * BLOCK, n % 2
        v3 = jnp.concatenate([keys_values(vp_ref, vc_ref, u, hk), ones], axis=1)
        pv = jnp.dot(jnp.exp2(t_sc[slot]).astype(BF16), v3, preferred_element_type=F32)
        outs = []
        for g in range(GROUP_A):
            blk = pv[g * BLOCK:(g + 1) * BLOCK, :]
            outs.append(blk[:, :LANES] * (1.0 / (blk[:, LANES:] + es_sc[slot, g])))
        for jp in range(pairs):
            c0 = (hk * pairs + jp) * LANES
            o_ref[r0:r0 + BLOCK, c0:c0 + LANES] = jnp.where(lo, outs[2 * jp], outs[2 * jp + 1])

    scores(0)
    for n in range(len(units)):
        values(n)
        if n + 1 < len(units):
            scores(n + 1)


def _attn_a_call(sinks, qa, ka, va, *, batch, seq):
    nb = seq // BLOCK
    rows = QBLOCKS_A * BLOCK
    steps = nb // QBLOCKS_A
    cur = lambda b, i: (b * steps + i, 0)
    prev = lambda b, i: (b * nb + jnp.maximum(QBLOCKS_A * i - 1, 0), 0)
    return pl.pallas_call(
        _attn_a_kernel,
        grid=(batch, steps),
        in_specs=[
            pl.BlockSpec(memory_space=pltpu.SMEM),
            pl.BlockSpec((rows, WIDTH_A), cur),
            pl.BlockSpec((BLOCK, 2 * LANES), prev),
            pl.BlockSpec((rows, 2 * LANES), cur),
            pl.BlockSpec((BLOCK, 2 * LANES), prev),
            pl.BlockSpec((rows, 2 * LANES), cur),
        ],
        out_specs=pl.BlockSpec((rows, WIDTH_A), cur),
        out_shape=jax.ShapeDtypeStruct((batch * seq, WIDTH_A), F32),
        scratch_shapes=[pltpu.VMEM((2, GROUP_A * BLOCK, 2 * BLOCK), F32),
                        pltpu.VMEM((2, GROUP_A, BLOCK, LANES), F32)],
        compiler_params=pltpu.CompilerParams(
            dimension_semantics=("arbitrary", "arbitrary"), vmem_limit_bytes=VMEM_LIMIT),
        name="attn_swa_gqa",
    )(sinks, qa, ka, ka, va, va)


def _attn_b_kernel(q_ref, k_ref, v_ref, *rest, seq, n_casts):
    cast_in, rest = rest[:n_casts], rest[n_casts:]
    o_ref, cast_out = rest[0], rest[1:1 + n_casts]
    x4_ref, t_sc, m_sc, l_sc, acc_sc = rest[1 + n_casts:]
    _cast_blocks(cast_in, cast_out)
    (_, d1), (_, d2), (_, d3) = DILATED
    assert d1 == 1 and d3 == d2 * d2 and all(w // dil == BLOCK for w, dil in DILATED)
    n_iter = seq // BLOCK
    cls = seq // d2
    nblk2 = cls // BLOCK
    assert seq // d3 == BLOCK and n_iter == d2 * nblk2 == d3

    lane = lax.broadcasted_iota(jnp.int32, (1, LANES), 1)
    lo = lane < HEAD_DIM
    qi = lax.broadcasted_iota(jnp.int32, (BLOCK, BLOCK), 0)
    kj = lax.broadcasted_iota(jnp.int32, (BLOCK, BLOCK), 1)
    bias_cur = jnp.where(kj <= qi, 0.0, NEG_BIG).astype(F32)
    bias_prev = jnp.where(kj >= qi, 0.0, NEG_BIG).astype(F32)
    zero = jnp.zeros((BLOCK, LANES), BF16)
    ones = jnp.ones((2 * BLOCK, LANES), BF16)

    for a, ref in enumerate((q_ref, k_ref, v_ref)):
        for r in range(d2):
            x4_ref[a, r * cls:(r + 1) * cls, :] = ref[0, pl.ds(r, cls, stride=d2), :]

    natural = lambda a, rows: (q_ref, k_ref, v_ref)[a][0, rows, :]
    by_class = lambda a, rows: x4_ref[a, rows, :]

    def block_rows(cfg, idx):
        static = isinstance(idx, int)
        block = lambda start: pl.ds(start if static else pl.multiple_of(start, BLOCK), BLOCK)
        biggest = max if static else jnp.maximum
        start = idx * BLOCK
        own = block(start)
        if cfg == 0:
            return natural, own, block(biggest(start - BLOCK, 0)), idx > 0
        if cfg == 1:
            n = idx % nblk2
            return by_class, own, block(biggest(start - BLOCK, start - n * BLOCK)), n > 0
        return by_class, pl.ds((idx % d2) * cls + idx // d2, BLOCK, stride=d2), None, None

    def scores(cfg, idx, slot):
        load, own, prev, has_prev = block_rows(cfg, idx)
        qb = load(0, own).astype(BF16)
        lhs = jnp.concatenate([jnp.where(lo, qb, zero), jnp.where(lo, zero, qb)], axis=0)
        if prev is None:
            keys, bias = load(1, own), bias_cur
        else:
            keys = jnp.concatenate([load(1, prev), load(1, own)], axis=0)
            off = jnp.where(has_prev, 0.0, NEG_BIG).astype(F32)
            bias = jnp.concatenate([bias_prev + off, bias_cur], axis=1)
        nk = keys.shape[0]
        s = lax.dot_general(lhs, keys.astype(BF16), (((1,), (1,)), ((), ())),
                            preferred_element_type=F32)
        s0 = s[:BLOCK, :] + bias
        s1 = s[BLOCK:, :] + bias
        m0 = jnp.max(s0, axis=-1, keepdims=True)
        m1 = jnp.max(s1, axis=-1, keepdims=True)
        t_sc[cfg, slot, :BLOCK, :nk] = s0 - m0
        t_sc[cfg, slot, BLOCK:, :nk] = s1 - m1
        m_sc[cfg, own, :] = jnp.where(lo, m0, m1)

    def values(cfg, idx, slot):
        load, own, prev, _ = block_rows(cfg, idx)
        vals = load(2, own) if prev is None else jnp.concatenate([load(2, prev), load(2, own)], axis=0)
        nk = vals.shape[0]
        p = jnp.exp2(t_sc[cfg, slot, :, :nk]).astype(BF16)
        v3 = jnp.concatenate([vals.astype(BF16), ones[:nk, :]], axis=1)
        pv = jnp.dot(p, v3, preferred_element_type=F32)
        acc_sc[cfg, own, :] = jnp.where(lo, pv[:BLOCK, :LANES], pv[BLOCK:, :LANES])
        l_sc[cfg, own, :] = jnp.where(lo, pv[:BLOCK, LANES:], pv[BLOCK:, LANES:])

    n_cfg = len(DILATED)
    for cfg in range(n_cfg):
        scores(cfg, 0, 0)

    def body(k, carry):
        for half in range(2):
            b = 2 * k + half
            for cfg in range(n_cfg):
                values(cfg, b, half)
                scores(cfg, b + 1, 1 - half)
        return carry

    assert n_iter % 2 == 0
    lax.fori_loop(0, n_iter // 2 - 1, body, 0, unroll=UNROLL_B)
    for cfg in range(n_cfg):
        values(cfg, n_iter - 2, 0)
        scores(cfg, n_iter - 1, 1)
    for cfg in range(n_cfg):
        values(cfg, n_iter - 1, 1)

    def merge(i, carry):
        start = i * BLOCK
        xr = pl.ds(pl.multiple_of(start, BLOCK), BLOCK)
        nat = pl.ds(i // nblk2 + (i % nblk2) * (BLOCK * d2), BLOCK, stride=d2)
        sel = (nat, xr, xr)
        ms = [m_sc[c, sel[c], :] for c in range(len(DILATED))]
        m_all = functools.reduce(jnp.maximum, ms)
        ws = [jnp.exp2(m - m_all) for m in ms]
        l_all = sum(w * l_sc[c, sel[c], :] for c, w in enumerate(ws))
        acc_all = sum(w * acc_sc[c, sel[c], :] for c, w in enumerate(ws))
        o_ref[0, nat, :] = acc_all * (1.0 / l_all)
        return carry

    lax.fori_loop(0, n_iter, merge, 0, unroll=True)


def _attn_b_call(qb, kb, vb, *, batch, seq, cast_weights=()):
    spec = pl.BlockSpec((1, seq, LANES), lambda b, hp: (b, 0, hp))
    shape3 = (batch, seq, WIDTH_B)
    pairs = WIDTH_B // LANES
    cast_specs = _cast_specs(cast_weights, batch * pairs, lambda b, hp: b * pairs + hp)
    return pl.pallas_call(
        functools.partial(_attn_b_kernel, seq=seq, n_casts=len(cast_weights)),
        grid=(batch, pairs),
        in_specs=[spec, spec, spec] + cast_specs,
        out_specs=[spec] + cast_specs,
        out_shape=[jax.ShapeDtypeStruct(shape3, F32)] + _cast_shapes(cast_weights),
        scratch_shapes=[pltpu.VMEM((3, seq, LANES), F32),
                        pltpu.VMEM((len(DILATED), 2, 2 * BLOCK, 2 * BLOCK), F32)]
        + [pltpu.VMEM((len(DILATED), seq, LANES), F32)] * 3,
        compiler_params=pltpu.CompilerParams(
            dimension_semantics=("arbitrary", "arbitrary"), vmem_limit_bytes=VMEM_LIMIT),
        name="attn_dilated",
    )(qb.reshape(shape3), kb.reshape(shape3), vb.reshape(shape3), *cast_weights)


def _outproj_kernel(oa_ref, ob_ref, x_ref, mod_ref, ga_ref, gb_ref, w_ref, b_ref, *rest,
                    sub, n_casts):
    o_ref = rest[n_casts]
    _cast_blocks(rest[:n_casts], rest[n_casts + 1:])
    ya = _rms(oa_ref[...], ga_ref[...]).astype(BF16)
    yb = _rms(ob_ref[...], gb_ref[...]).astype(BF16)
    y = (jnp.dot(ya, w_ref[0:WIDTH_A, :], preferred_element_type=F32)
         + jnp.dot(yb, w_ref[WIDTH_A:WIDTH_A + WIDTH_B, :], preferred_element_type=F32)
         + b_ref[...])
    gmod = mod_ref[0, pl.ds(3 * sub + 2, 1), :]
    o_ref[...] = x_ref[...] + gmod * y


def _outproj_call(out_a, out_b, x, mod, g_a, g_b, w_out, b_out, *, sub, seq, cast_weights=()):
    t, d = x.shape
    tm = TM_PROJ
    tiles_per_batch = seq // tm
    row = lambda i: (i, 0)
    const = lambda i: (0, 0)
    return pl.pallas_call(
        functools.partial(_outproj_kernel, sub=sub, n_casts=len(cast_weights)),
        grid=(t // tm,),
        in_specs=[
            pl.BlockSpec((tm, WIDTH_A), row),
            pl.BlockSpec((tm, WIDTH_B), row),
            pl.BlockSpec((tm, d), row),
            pl.BlockSpec((1, N_SUB * N_MOD, d), lambda i: (i // tiles_per_batch, 0, 0)),
            pl.BlockSpec((1, WIDTH_A), const),
            pl.BlockSpec((1, WIDTH_B), const),
            pl.BlockSpec((WIDTH_A + WIDTH_B, d), const),
            pl.BlockSpec((1, d), const),
        ] + _cast_specs(cast_weights, t // tm),
        out_specs=[pl.BlockSpec((tm, d), row)] + _cast_specs(cast_weights, t // tm),
        out_shape=[jax.ShapeDtypeStruct((t, d), F32)] + _cast_shapes(cast_weights),
        compiler_params=pltpu.CompilerParams(
            dimension_semantics=("arbitrary",), vmem_limit_bytes=VMEM_LIMIT),
        name="outproj_residual",
    )(out_a, out_b, x, mod, g_a.reshape(1, WIDTH_A), g_b.reshape(1, WIDTH_B), w_out,
      b_out.reshape(1, d), *cast_weights)


def kernel(x, c, positions, w_ada, b_ada, g_ffn1, w_ffn1_in, w_ffn1_out, g_mix, w_in, b_in, sinks,
           g_out_a, g_out_b, w_out, b_out, g_ffn2, w_ffn2_in, w_ffn2_out, g_final):
    batch, seq, d = x.shape
    depth = w_ada.shape[0]
    t = batch * seq
    xt = x.reshape(t, d)
    pos = _packed_positions(positions.reshape(t))
    lane_dim = jnp.arange(LANES, dtype=jnp.int32) % (ROT_DIM // 2)
    invf = (ROPE_THETA ** (-(2.0 * lane_dim.astype(F32)) / ROT_DIM)).reshape(1, LANES)
    for layer in range(depth):
        mod = _ada_call(c, w_ada[layer], b_ada[layer]).reshape(batch, N_SUB * N_MOD, d)
        last = layer == depth - 1
        xt, w_in_bf = _ffn_call(
            xt, mod, g_ffn1[layer], w_ffn1_in[layer].astype(BF16), w_ffn1_out[layer].astype(BF16),
            g_final, sub=0, seq=seq, final_norm=False, cast_weights=(w_in[layer],))
        qa, ka, va, qb, kb, vb, w_out_bf, w_down2_bf = _inproj_call(
            xt, mod, g_mix[layer], w_in_bf, b_in[layer], pos, invf, sub=1, seq=seq,
            cast_weights=(w_out[layer], w_ffn2_out[layer]))
        out_a = _attn_a_call(sinks[layer], qa, ka, va, batch=batch, seq=seq)
        out_b, w_gu2_bf = _attn_b_call(qb, kb, vb, batch=batch, seq=seq,
                                       cast_weights=(w_ffn2_in[layer],))
        xt, = _outproj_call(out_a, out_b.reshape(t, WIDTH_B), xt, mod, g_out_a[layer],
                            g_out_b[layer], w_out_bf, b_out[layer], sub=1, seq=seq)
        xt, = _ffn_call(xt, mod, g_ffn2[layer], w_gu2_bf, w_down2_bf, g_final, sub=2, seq=seq,
                        final_norm=last)
    if depth == 0:
        raise ValueError("depth must be >= 1")
    return xt.reshape(batch, seq, d)
```
